```python
import math
import jax
import jax.numpy as jnp
from jax import lax
import numpy as np

D_MODEL = 2048
BATCH = 1
SEQ = 8192
DEPTH = 4

GRID_W = 64
CTX_LEN = 256
DA_HEADS = 8
DA_HEAD_DIM = 64
DA_WIDTH = DA_HEADS * 2 * DA_HEAD_DIM
GQ_HEADS = 8
GQ_KV_HEADS = 2
GQ_GROUP = GQ_HEADS // GQ_KV_HEADS
GQ_HEAD_DIM = 128
GQ_WIDTH = GQ_HEADS * GQ_HEAD_DIM
GQ_KV_WIDTH = GQ_KV_HEADS * GQ_HEAD_DIM
S5_WIDTH = 1024
S5_GROUP = 16
S5_GROUPS = S5_WIDTH // S5_GROUP
S5_STATE = 64
S5_MAX_REAL = -1e-4
FFN_HIDDEN = -(-8 * D_MODEL // 768) * 256
N_BRANCH = 3
IN_SPLITS = (DA_WIDTH, 2 * DA_WIDTH, 3 * DA_WIDTH, 3 * DA_WIDTH + GQ_WIDTH, 3 * DA_WIDTH + GQ_WIDTH + GQ_KV_WIDTH, 3 * DA_WIDTH + GQ_WIDTH + 2 * GQ_KV_WIDTH, 3 * DA_WIDTH + GQ_WIDTH + 2 * GQ_KV_WIDTH + S5_WIDTH)
IN_WIDTH = 3 * DA_WIDTH + GQ_WIDTH + 2 * GQ_KV_WIDTH + S5_WIDTH + N_BRANCH * D_MODEL
Q_BLOCK = 128
ROPE_THETA = 10000.0
DEEPNORM_ALPHA = (2 * DEPTH) ** 0.25
DEEPNORM_BETA = (8 * DEPTH) ** -0.25
LN_EPS = 1e-6
RMS_EPS = 1e-6
DA_SUBLN_EPS = 1e-5

kernel_name = 'hybrid_diffattn_gqa_s5_prefix_dit'


def layer_norm(x, g, b):
    xf = x.astype(jnp.float32)
    mu = jnp.mean(xf, axis=-1, keepdims=True)
    var = jnp.mean(jnp.square(xf - mu), axis=-1, keepdims=True)
    y = (xf - mu) * lax.rsqrt(var + LN_EPS)
    return (y * g.astype(jnp.float32) + b.astype(jnp.float32)).astype(x.dtype)


def rms_norm(x, w, eps):
    xf = x.astype(jnp.float32)
    y = xf * lax.rsqrt(jnp.mean(jnp.square(xf), axis=-1, keepdims=True) + eps)
    return (y * w.astype(jnp.float32)).astype(x.dtype)


def axial_rope_tables(rows, dim):
    row = jnp.broadcast_to(jnp.arange(rows, dtype=jnp.float32)[:, None], (rows, GRID_W)).reshape(-1)
    col = jnp.broadcast_to(jnp.arange(GRID_W, dtype=jnp.float32)[None, :], (rows, GRID_W)).reshape(-1)
    axis_dim = dim // 2
    inv_freq = ROPE_THETA ** (-jnp.arange(0, axis_dim, 2, dtype=jnp.float32) / axis_dim)
    ang_r = row[:, None] * inv_freq[None, :]
    ang_c = col[:, None] * inv_freq[None, :]
    ang = jnp.concatenate([ang_r, ang_r, ang_c, ang_c], axis=-1)
    return jnp.cos(ang), jnp.sin(ang)


def apply_axial_rope(z, cos, sin):
    quarter = z.shape[-1] // 4
    z4 = z.reshape(z.shape[:-1] + (2, 2, quarter))
    rot = jnp.stack([-z4[..., 1, :], z4[..., 0, :]], axis=-2).reshape(z.shape)
    cos_b = cos[None, :, None, :]
    sin_b = sin[None, :, None, :]
    return (z.astype(jnp.float32) * cos_b + rot.astype(jnp.float32) * sin_b).astype(z.dtype)


def sweep_query_blocks(fn, qs):
    n = qs[0].shape[-2]
    nb = n // Q_BLOCK
    blocks = tuple(jnp.moveaxis(q.reshape(q.shape[:-2] + (nb, Q_BLOCK, q.shape[-1])), -3, 0) for q in qs)
    out = lax.map(fn, blocks)
    out = jnp.moveaxis(out, 0, -3)
    return out.reshape(out.shape[:-3] + (n, out.shape[-1]))


def diff_attn_core(q1, q2, k1, k2, v, lam):
    scale = DA_HEAD_DIM ** -0.5
    s1 = jnp.einsum('bhqd,bhsd->bhqs', q1, k1).astype(jnp.float32) * scale
    s2 = jnp.einsum('bhqd,bhsd->bhqs', q2, k2).astype(jnp.float32) * scale
    p = jax.nn.softmax(s1, axis=-1) - lam * jax.nn.softmax(s2, axis=-1)
    return jnp.einsum('bhqs,bhsd->bhqd', p.astype(v.dtype), v)


def gqa_core(q, k, v):
    s = jnp.einsum('bkgqd,bksd->bkgqs', q, k).astype(jnp.float32) * (GQ_HEAD_DIM ** -0.5)
    p = jax.nn.softmax(s, axis=-1)
    return jnp.einsum('bkgqs,bksd->bkgqd', p.astype(v.dtype), v)


def s5_discretize(a_re, a_im, log_dt, b_re, b_im):
    a_re = jnp.minimum(a_re, S5_MAX_REAL)
    dt = jnp.exp(log_dt)[:, None]
    mag = jnp.exp(a_re * dt)
    lr = mag * jnp.cos(a_im * dt)
    li = mag * jnp.sin(a_im * dt)
    den = jnp.square(a_re) + jnp.square(a_im)
    nr = lr - 1.0
    cr = (nr * a_re + li * a_im) / den
    ci = (li * a_re - nr * a_im) / den
    bbr = cr[..., None] * b_re - ci[..., None] * b_im
    bbi = cr[..., None] * b_im + ci[..., None] * b_re
    return lr, li, bbr, bbi


def complex_scan_op(e1, e2):
    a1r, a1i, b1r, b1i = e1
    a2r, a2i, b2r, b2i = e2
    return (a1r * a2r - a1i * a2i, a1r * a2i + a1i * a2r,
            a2r * b1r - a2i * b1i + b2r, a2r * b1i + a2i * b1r + b2i)


def s5_scan(u, lr, li, bbr, bbi, s0, reverse):
    br = jnp.einsum('blgh,gph->blgp', u, bbr)
    bi = jnp.einsum('blgh,gph->blgp', u, bbi)
    if s0 is not None:
        s0r, s0i = s0
        idx = u.shape[1] - 1 if reverse else 0
        br = br.at[:, idx].add(lr * s0r - li * s0i)
        bi = bi.at[:, idx].add(lr * s0i + li * s0r)
    ar = jnp.broadcast_to(lr, br.shape)
    ai = jnp.broadcast_to(li, bi.shape)
    _, _, xr, xi = lax.associative_scan(complex_scan_op, (ar, ai, br, bi), reverse=reverse, axis=1)
    return xr, xi


def s5_readout(xr, xi, c_re, c_im):
    return jnp.einsum('blgp,ghp->blgh', xr, c_re) - jnp.einsum('blgp,ghp->blgh', xi, c_im)


def s5_mixer(u, uc, p, need_ctx_out):
    b, n, _ = u.shape
    n_ctx = uc.shape[1]
    uf = u.astype(jnp.float32).reshape(b, n, S5_GROUPS, S5_GROUP)
    ucf = uc.astype(jnp.float32).reshape(b, n_ctx, S5_GROUPS, S5_GROUP)
    d_skip = p['s5_d'].astype(jnp.float32).reshape(S5_GROUPS, S5_GROUP)
    y = uf * d_skip
    yc = ucf * d_skip
    for direction in range(2):
        reverse = direction == 1
        lr, li, bbr, bbi = s5_discretize(p['s5_a_re'][direction].astype(jnp.float32), p['s5_a_im'][direction].astype(jnp.float32), p['s5_log_dt'][direction].astype(jnp.float32), p['s5_b_re'][direction].astype(jnp.float32), p['s5_b_im'][direction].astype(jnp.float32))
        c_re = p['s5_c_re'][direction].astype(jnp.float32)
        c_im = p['s5_c_im'][direction].astype(jnp.float32)
        cxr, cxi = s5_scan(ucf, lr, li, bbr, bbi, None, reverse)
        last = 0 if reverse else n_ctx - 1
        xr, xi = s5_scan(uf, lr, li, bbr, bbi, (cxr[:, last], cxi[:, last]), reverse)
        y = y + s5_readout(xr, xi, c_re, c_im)
        if need_ctx_out:
            yc = yc + s5_readout(cxr, cxi, c_re, c_im)
    y = y.reshape(b, n, S5_WIDTH).astype(u.dtype)
    yc = yc.reshape(b, n_ctx, S5_WIDTH).astype(uc.dtype) if need_ctx_out else None
    return y, yc


def s5_glu(y, p):
    a, g = jnp.split(jax.nn.gelu(y) @ p['w_glu'], 2, axis=-1)
    return a * jax.nn.sigmoid(g)


def swiglu(h, p):
    return (jax.nn.silu(h @ p['w_ffn_gate']) * (h @ p['w_ffn_up'])) @ p['w_ffn_down']


def token_mixer(h, hc, p, lam_init, rope, need_ctx_out):
    b, n, _ = h.shape
    n_ctx = hc.shape[1]
    cos_da, sin_da, cos_gq, sin_gq = rope
    qa, ka, va, qb, kb, vb, u, gt = jnp.split(h @ p['w_in'], IN_SPLITS, axis=-1)
    qac, kac, vac, qbc, kbc, vbc, uc, gtc = jnp.split(hc @ p['w_in'], IN_SPLITS, axis=-1)

    def heads_first(z, n_heads):
        return z.reshape(b, z.shape[1], n_heads, -1).transpose(0, 2, 1, 3)

    def da_qk(z, cos=None, sin=None):
        m = z.shape[1]
        z = z.reshape(b, m, 2 * DA_HEADS, DA_HEAD_DIM)
        if cos is not None:
            z = apply_axial_rope(z, cos, sin)
        z = z.reshape(b, m, DA_HEADS, 2, DA_HEAD_DIM).transpose(0, 3, 2, 1, 4)
        return z[:, 0], z[:, 1]

    q1, q2 = da_qk(qa, cos_da, sin_da)
    k1, k2 = da_qk(ka, cos_da, sin_da)
    q1c, q2c = da_qk(qac)
    k1c, k2c = da_qk(kac)
    v_a = heads_first(va, DA_HEADS)
    v_ac = heads_first(vac, DA_HEADS)
    k1_all = jnp.concatenate([k1, k1c], axis=2)
    k2_all = jnp.concatenate([k2, k2c], axis=2)
    va_all = jnp.concatenate([v_a, v_ac], axis=2)
    lam = (jnp.exp(jnp.sum(p['da_lam_q1'].astype(jnp.float32) * p['da_lam_k1'].astype(jnp.float32)))
           - jnp.exp(jnp.sum(p['da_lam_q2'].astype(jnp.float32) * p['da_lam_k2'].astype(jnp.float32))) + lam_init)

    def da_out(o):
        m = o.shape[2]
        o = rms_norm(o.transpose(0, 2, 1, 3), p['da_subln_w'], DA_SUBLN_EPS) * (1.0 - lam_init)
        return o.reshape(b, m, DA_WIDTH) @ p['w_pa']

    o_a = sweep_query_blocks(lambda qs: diff_attn_core(qs[0], qs[1], k1_all, k2_all, va_all, lam), (q1, q2))
    pa = da_out(o_a)

    def gq_heads(z, n_heads, norm_w, cos=None, sin=None):
        m = z.shape[1]
        z = rms_norm(z.reshape(b, m, n_heads, GQ_HEAD_DIM), norm_w, RMS_EPS)
        if cos is not None:
            z = apply_axial_rope(z, cos, sin)
        return z.transpose(0, 2, 1, 3)

    qg = gq_heads(qb, GQ_HEADS, p['gq_qnorm_w'], cos_gq, sin_gq).reshape(b, GQ_KV_HEADS, GQ_GROUP, n, GQ_HEAD_DIM)
    kg = gq_heads(kb, GQ_KV_HEADS, p['gq_knorm_w'], cos_gq, sin_gq)
    kgc = gq_heads(kbc, GQ_KV_HEADS, p['gq_knorm_w'])
    kg_all = jnp.concatenate([kg, kgc], axis=2)
    vg_c = heads_first(vbc, GQ_KV_HEADS)
    vg_all = jnp.concatenate([heads_first(vb, GQ_KV_HEADS), vg_c], axis=2)

    def gq_out(o):
        m = o.shape[3]
        return o.reshape(b, GQ_HEADS, m, GQ_HEAD_DIM).transpose(0, 2, 1, 3).reshape(b, m, GQ_WIDTH) @ p['w_pb']

    o_b = sweep_query_blocks(lambda qs: gqa_core(qs[0], kg_all, vg_all), (qg,))
    pb = gq_out(o_b)

    y_s5, yc_s5 = s5_mixer(u, uc, p, need_ctx_out)
    pc = s5_glu(y_s5, p)

    g = jax.nn.sigmoid(gt.reshape(b, n, N_BRANCH, D_MODEL))
    out = (g[:, :, 0] * pa + g[:, :, 1] * pb + g[:, :, 2] * pc) @ p['w_o']
    if not need_ctx_out:
        return out, None
    qgc = gq_heads(qbc, GQ_HEADS, p['gq_qnorm_w']).reshape(b, GQ_KV_HEADS, GQ_GROUP, n_ctx, GQ_HEAD_DIM)
    pa_c = da_out(diff_attn_core(q1c, q2c, k1c, k2c, v_ac, lam))
    pb_c = gq_out(gqa_core(qgc, kgc, vg_c))
    pc_c = s5_glu(yc_s5, p)
    gc = jax.nn.sigmoid(gtc.reshape(b, n_ctx, N_BRANCH, D_MODEL))
    out_c = (gc[:, :, 0] * pa_c + gc[:, :, 1] * pb_c + gc[:, :, 2] * pc_c) @ p['w_o']
    return out, out_c


def trunk_layer(x, ctx, c, c_ctx, p, layer_idx, need_ctx_out, rope):
    lam_init = 0.8 - 0.6 * math.exp(-0.3 * layer_idx)
    mod = jax.nn.silu(c) @ p['w_ada'] + p['b_ada']
    mod_c = jax.nn.silu(c_ctx) @ p['w_ada'] + p['b_ada']
    sh_m, sc_m, g_m, sh_f, sc_f, g_f = jnp.split(mod[:, None, :], 6, axis=-1)
    shc_m, scc_m, gc_m, shc_f, scc_f, gc_f = jnp.split(mod_c[None, None, :], 6, axis=-1)
    y, yc = token_mixer(x * (1.0 + sc_m) + sh_m, ctx * (1.0 + scc_m) + shc_m, p, lam_init, rope, need_ctx_out)
    x = layer_norm(DEEPNORM_ALPHA * x + g_m * y, p['ln_mix_g'], p['ln_mix_b'])
    x = layer_norm(DEEPNORM_ALPHA * x + g_f * swiglu(x * (1.0 + sc_f) + sh_f, p), p['ln_ffn_g'], p['ln_ffn_b'])
    if need_ctx_out:
        ctx = layer_norm(DEEPNORM_ALPHA * ctx + gc_m * yc, p['ln_mix_g'], p['ln_mix_b'])
        ctx = layer_norm(DEEPNORM_ALPHA * ctx + gc_f * swiglu(ctx * (1.0 + scc_f) + shc_f, p), p['ln_ffn_g'], p['ln_ffn_b'])
    return x, ctx


def setup_inputs(seed: int = 0) -> dict:
    key = jax.random.key(seed)
    ks = jax.random.split(key, 40)
    f32 = jnp.float32

    def nrm(k, shape, scale):
        return jax.random.normal(k, shape, f32) * scale

    L, D, F = DEPTH, D_MODEL, FFN_HIDDEN
    G, P, H = S5_GROUPS, S5_STATE, S5_GROUP
    a_im_base = jnp.pi * jnp.arange(P, dtype=f32)
    return {
        'x': nrm(ks[0], (BATCH, SEQ, D), 1.0),
        'c': nrm(ks[1], (BATCH, D), 1.0),
        'ctx': nrm(ks[2], (BATCH, CTX_LEN, D), 1.0),
        'c_ctx': nrm(ks[3], (D,), 1.0),
        'w_ada': nrm(ks[4], (L, D, 6 * D), 0.5 * D ** -0.5),
        'b_ada': nrm(ks[5], (L, 6 * D), 0.02),
        'w_in': nrm(ks[6], (L, D, IN_WIDTH), D ** -0.5),
        'da_lam_q1': nrm(ks[7], (L, DA_HEAD_DIM), 0.1),
        'da_lam_k1': nrm(ks[8], (L, DA_HEAD_DIM), 0.1),
        'da_lam_q2': nrm(ks[9], (L, DA_HEAD_DIM), 0.1),
        'da_lam_k2': nrm(ks[10], (L, DA_HEAD_DIM), 0.1),
        'da_subln_w': 1.0 + nrm(ks[11], (L, 2 * DA_HEAD_DIM), 0.02),
        'w_pa': nrm(ks[12], (L, DA_WIDTH, D), DA_WIDTH ** -0.5),
        'gq_qnorm_w': 1.0 + nrm(ks[13], (L, GQ_HEAD_DIM), 0.02),
        'gq_knorm_w': 1.0 + nrm(ks[14], (L, GQ_HEAD_DIM), 0.02),
        'w_pb': nrm(ks[15], (L, GQ_WIDTH, D), GQ_WIDTH ** -0.5),
        's5_a_re': -0.5 + nrm(ks[16], (L, 2, G, P), 0.01),
        's5_a_im': a_im_base + nrm(ks[17], (L, 2, G, P), 0.01),
        's5_log_dt': jax.random.uniform(ks[18], (L, 2, G), f32, math.log(1e-3), math.log(1e-1)),
        's5_b_re': nrm(ks[19], (L, 2, G, P, H), (2.0 * H) ** -0.5),
        's5_b_im': nrm(ks[20], (L, 2, G, P, H), (2.0 * H) ** -0.5),
        's5_c_re': nrm(ks[21], (L, 2, G, H, P), (2.0 * P) ** -0.5),
        's5_c_im': nrm(ks[22], (L, 2, G, H, P), (2.0 * P) ** -0.5),
        's5_d': nrm(ks[23], (L, S5_WIDTH), 1.0),
        'w_glu': nrm(ks[24], (L, S5_WIDTH, 2 * D), S5_WIDTH ** -0.5),
        'w_o': nrm(ks[25], (L, D, D), DEEPNORM_BETA * D ** -0.5),
        'ln_mix_g': 1.0 + nrm(ks[26], (L, D), 0.02),
        'ln_mix_b': nrm(ks[27], (L, D), 0.02),
        'w_ffn_gate': nrm(ks[28], (L, D, F), D ** -0.5),
        'w_ffn_up': nrm(ks[29], (L, D, F), D ** -0.5),
        'w_ffn_down': nrm(ks[30], (L, F, D), DEEPNORM_BETA * F ** -0.5),
        'ln_ffn_g': 1.0 + nrm(ks[31], (L, D), 0.02),
        'ln_ffn_b': nrm(ks[32], (L, D), 0.02),
    }


def reference(x, c, ctx, c_ctx, w_ada, b_ada, w_in, da_lam_q1, da_lam_k1, da_lam_q2, da_lam_k2, da_subln_w, w_pa, gq_qnorm_w, gq_knorm_w, w_pb, s5_a_re, s5_a_im, s5_log_dt, s5_b_re, s5_b_im, s5_c_re, s5_c_im, s5_d, w_glu, w_o, ln_mix_g, ln_mix_b, w_ffn_gate, w_ffn_up, w_ffn_down, ln_ffn_g, ln_ffn_b):
    n_tokens = x.shape[1]
    rows = n_tokens // GRID_W
    cos_da, sin_da = axial_rope_tables(rows, DA_HEAD_DIM)
    cos_gq, sin_gq = axial_rope_tables(rows, GQ_HEAD_DIM)
    rope = (cos_da, sin_da, cos_gq, sin_gq)
    for l in range(DEPTH):
        p = {
            'w_ada': w_ada[l], 'b_ada': b_ada[l], 'w_in': w_in[l],
            'da_lam_q1': da_lam_q1[l], 'da_lam_k1': da_lam_k1[l],
            'da_lam_q2': da_lam_q2[l], 'da_lam_k2': da_lam_k2[l],
            'da_subln_w': da_subln_w[l], 'w_pa': w_pa[l],
            'gq_qnorm_w': gq_qnorm_w[l], 'gq_knorm_w': gq_knorm_w[l], 'w_pb': w_pb[l],
            's5_a_re': s5_a_re[l], 's5_a_im': s5_a_im[l], 's5_log_dt': s5_log_dt[l],
            's5_b_re': s5_b_re[l], 's5_b_im': s5_b_im[l],
            's5_c_re': s5_c_re[l], 's5_c_im': s5_c_im[l], 's5_d': s5_d[l],
            'w_glu': w_glu[l], 'w_o': w_o[l],
            'ln_mix_g': ln_mix_g[l], 'ln_mix_b': ln_mix_b[l],
            'w_ffn_gate': w_ffn_gate[l], 'w_ffn_up': w_ffn_up[l], 'w_ffn_down': w_ffn_down[l],
            'ln_ffn_g': ln_ffn_g[l], 'ln_ffn_b': ln_ffn_b[l],
        }
        x, ctx = trunk_layer(x, ctx, c, c_ctx, p, l, l < DEPTH - 1, rope)
    return x
```

```python
import functools
import math

import jax
import jax.numpy as jnp
from jax import lax
from jax.experimental import pallas as pl
from jax.experimental.pallas import tpu as pltpu

F32 = jnp.float32
BF16 = jnp.bfloat16

GRID_W = 64
DA_HEADS = 8
DA_HEAD_DIM = 64
DA_WIDTH = DA_HEADS * 2 * DA_HEAD_DIM
GQ_HEADS = 8
GQ_KV_HEADS = 2
GQ_GROUP = GQ_HEADS // GQ_KV_HEADS
GQ_HEAD_DIM = 128
GQ_WIDTH = GQ_HEADS * GQ_HEAD_DIM
GQ_KV_WIDTH = GQ_KV_HEADS * GQ_HEAD_DIM
S5_WIDTH = 1024
S5_GROUP = 16
S5_GROUPS = S5_WIDTH // S5_GROUP
S5_STATE = 64
S5_MAX_REAL = -1e-4
N_BRANCH = 3
ROPE_THETA = 10000.0
LN_EPS = 1e-6
RMS_EPS = 1e-6
DA_SUBLN_EPS = 1e-5

OFF_QA = 0
OFF_KA = OFF_QA + DA_WIDTH
OFF_VA = OFF_KA + DA_WIDTH
OFF_QB = OFF_VA + DA_WIDTH
OFF_KB = OFF_QB + GQ_WIDTH
OFF_VB = OFF_KB + GQ_KV_WIDTH
OFF_U = OFF_VB + GQ_KV_WIDTH
OFF_GATE = OFF_U + S5_WIDTH

LANES = 128
V7X_VMEM_BYTES = 64 * 1024 * 1024
VMEM_LIMIT = V7X_VMEM_BYTES - 8 * 1024 * 1024

S5_CHUNK = 16
S5_CW = S5_CHUNK * S5_GROUP


def _cparams(*sem):
    return pltpu.CompilerParams(dimension_semantics=sem, vmem_limit_bytes=VMEM_LIMIT)


def _tile(n, target, mult):
    best = None
    for t in range(mult, min(n, target) + 1, mult):
        if n % t == 0:
            best = t
    assert best is not None, (n, target, mult)
    return best


def _sigmoid(x):
    return 1.0 / (1.0 + jnp.exp(-x))


def _gelu_tanh(x):
    return 0.5 * x * (1.0 + jnp.tanh(math.sqrt(2.0 / math.pi) * (x + 0.044715 * (x * x * x))))


def _row_select(vec_ref, row0, tm, n_ctx):
    rows = row0 + lax.broadcasted_iota(jnp.int32, (tm, 1), 0)
    return jnp.where(rows < n_ctx, vec_ref[1:2, :], vec_ref[0:1, :])


def _layer_norm(z, g, b):
    mu = jnp.mean(z, axis=-1, keepdims=True)
    zc = z - mu
    var = jnp.mean(zc * zc, axis=-1, keepdims=True)
    return zc * lax.rsqrt(var + LN_EPS) * g + b


def _mod_kernel(c_ref, w_ref, b_ref, o_ref):
    a = c_ref[...]
    a = a * _sigmoid(a)
    o_ref[0] = jnp.dot(a.astype(BF16), w_ref[0].astype(BF16), preferred_element_type=F32) + b_ref[0]


def _ada_mod(c_rows, w_ada, b_ada):
    depth, d, n6 = w_ada.shape
    tn = _tile(n6, 1024, LANES)
    return pl.pallas_call(
        _mod_kernel,
        out_shape=jax.ShapeDtypeStruct((depth, 8, n6), F32),
        grid=(depth, n6 // tn),
        in_specs=[
            pl.BlockSpec((8, d), lambda l, j: (0, 0)),
            pl.BlockSpec((1, d, tn), lambda l, j: (l, 0, j)),
            pl.BlockSpec((1, 1, tn), lambda l, j: (l, 0, j)),
        ],
        out_specs=pl.BlockSpec((1, 8, tn), lambda l, j: (l, 0, j)),
        compiler_params=_cparams("parallel", "parallel"),
        name="ada_mod",
    )(c_rows, w_ada, b_ada.reshape(depth, 1, n6))


def _in_proj_kernel(x_ref, sh_ref, sc_ref, w_ref, o_ref, h_scr, *, tm, n_ctx):
    i = pl.program_id(0)

    @pl.when(pl.program_id(1) == 0)
    def _():
        sh = _row_select(sh_ref, i * tm, tm, n_ctx)
        sc = _row_select(sc_ref, i * tm, tm, n_ctx)
        h_scr[...] = (x_ref[...] * (1.0 + sc) + sh).astype(BF16)

    o_ref[...] = jnp.dot(h_scr[...], w_ref[...], preferred_element_type=F32).astype(o_ref.dtype)


def _in_proj(xs, mod, w_in, n_ctx):
    r, d = xs.shape
    n = w_in.shape[1]
    tm = _tile(r, 1056, 16)
    tn = _tile(n, 512, LANES)
    return pl.pallas_call(
        functools.partial(_in_proj_kernel, tm=tm, n_ctx=n_ctx),
        out_shape=jax.ShapeDtypeStruct((r, n), BF16),
        grid=(r // tm, n // tn),
        in_specs=[
            pl.BlockSpec((tm, d), lambda i, j: (i, 0)),
            pl.BlockSpec((8, d), lambda i, j: (0, 0)),
            pl.BlockSpec((8, d), lambda i, j: (0, 1)),
            pl.BlockSpec((d, tn), lambda i, j: (0, j)),
        ],
        out_specs=pl.BlockSpec((tm, tn), lambda i, j: (i, j)),
        scratch_shapes=[pltpu.VMEM((tm, d), BF16)],
        compiler_params=_cparams("parallel", "arbitrary"),
        name="in_proj",
    )(xs, mod, mod, w_in)


def _rope(z, cos, sin_a, sin_b, quarter):
    return z * cos + pltpu.roll(z, LANES - quarter, 1) * sin_a + pltpu.roll(z, quarter, 1) * sin_b


def _rms(z, w, eps):
    return z * lax.rsqrt(jnp.mean(z * z, axis=-1, keepdims=True) + eps) * w


def _prep_kernel(qa_ref, ka_ref, qb_ref, kb_ref, tda_ref, tgq_ref, qn_ref, kn_ref,
                 qa_o, ka_o, qb_o, kb_o):
    cda, sda_a, sda_b = tda_ref[0], tda_ref[1], tda_ref[2]
    cgq, sgq_a, sgq_b = tgq_ref[0], tgq_ref[1], tgq_ref[2]
    da_scale = DA_HEAD_DIM ** -0.5
    gq_scale = GQ_HEAD_DIM ** -0.5
    for h in range(DA_HEADS):
        sl = slice(h * LANES, (h + 1) * LANES)
        z = qa_ref[:, sl].astype(F32)
        qa_o[:, sl] = (_rope(z, cda, sda_a, sda_b, DA_HEAD_DIM // 4) * da_scale).astype(BF16)
        z = ka_ref[:, sl].astype(F32)
        ka_o[:, sl] = _rope(z, cda, sda_a, sda_b, DA_HEAD_DIM // 4).astype(BF16)
    for h in range(GQ_HEADS):
        sl = slice(h * LANES, (h + 1) * LANES)
        z = _rms(qb_ref[:, sl].astype(F32), qn_ref[...], RMS_EPS)
        qb_o[:, sl] = (_rope(z, cgq, sgq_a, sgq_b, GQ_HEAD_DIM // 4) * gq_scale).astype(BF16)
    for h in range(GQ_KV_HEADS):
        sl = slice(h * LANES, (h + 1) * LANES)
        z = _rms(kb_ref[:, sl].astype(F32), kn_ref[...], RMS_EPS)
        kb_o[:, sl] = _rope(z, cgq, sgq_a, sgq_b, GQ_HEAD_DIM // 4).astype(BF16)


def _prep_qk(proj, tab_da, tab_gq, qn_w, kn_w):
    r = proj.shape[0]
    tr = _tile(r, 528, 16)
    blk = lambda w, off: pl.BlockSpec((tr, w), lambda i: (i, off // w))
    out = lambda w: pl.BlockSpec((tr, w), lambda i: (i, 0))
    return pl.pallas_call(
        _prep_kernel,
        out_shape=(
            jax.ShapeDtypeStruct((r, DA_WIDTH), BF16),
            jax.ShapeDtypeStruct((r, DA_WIDTH), BF16),
            jax.ShapeDtypeStruct((r, GQ_WIDTH), BF16),
            jax.ShapeDtypeStruct((r, GQ_KV_WIDTH), BF16),
        ),
        grid=(r // tr,),
        in_specs=[
            blk(DA_WIDTH, OFF_QA), blk(DA_WIDTH, OFF_KA), blk(GQ_WIDTH, OFF_QB), blk(GQ_KV_WIDTH, OFF_KB),
            pl.BlockSpec((3, tr, LANES), lambda i: (0, i, 0)),
            pl.BlockSpec((3, tr, LANES), lambda i: (0, i, 0)),
            pl.BlockSpec((1, LANES), lambda i: (0, 0)),
            pl.BlockSpec((1, LANES), lambda i: (0, 0)),
        ],
        out_specs=(out(DA_WIDTH), out(DA_WIDTH), out(GQ_WIDTH), out(GQ_KV_WIDTH)),
        compiler_params=_cparams("parallel"),
        name="prep_qk",
    )(proj, proj, proj, proj, tab_da, tab_gq, qn_w.reshape(1, LANES), kn_w.reshape(1, LANES))


def _rope_tables(n_lat, n_ctx, dim):
    rows = n_lat // GRID_W
    row = jnp.broadcast_to(jnp.arange(rows, dtype=F32)[:, None], (rows, GRID_W)).reshape(-1)
    col = jnp.broadcast_to(jnp.arange(GRID_W, dtype=F32)[None, :], (rows, GRID_W)).reshape(-1)
    axis_dim = dim // 2
    inv_freq = ROPE_THETA ** (-jnp.arange(0, axis_dim, 2, dtype=F32) / axis_dim)
    ang_r = row[:, None] * inv_freq[None, :]
    ang_c = col[:, None] * inv_freq[None, :]
    ang = jnp.concatenate([ang_r, ang_r, ang_c, ang_c], axis=-1)
    cos, sin = jnp.cos(ang), jnp.sin(ang)
    first_half = (jnp.arange(dim) % (dim // 2)) < (dim // 4)
    sin_a = jnp.where(first_half[None, :], -sin, 0.0)
    sin_b = jnp.where(first_half[None, :], 0.0, sin)
    tab = jnp.stack([cos, sin_a, sin_b])
    ident = jnp.stack([jnp.ones((n_ctx, dim), F32), jnp.zeros((n_ctx, dim), F32), jnp.zeros((n_ctx, dim), F32)])
    tab = jnp.concatenate([ident, tab], axis=1)
    return jnp.tile(tab, (1, 1, LANES // dim))


def _nt_dot(a, b):
    return lax.dot_general(a, b, (((1,), (1,)), ((), ())), preferred_element_type=F32)


def _softmax_first(s, v, m_scr, l_scr, a_scr):
    m = jnp.max(s, axis=-1, keepdims=True)
    p = jnp.exp(s - m)
    m_scr[...] = m
    l_scr[...] = jnp.sum(p, axis=-1, keepdims=True)
    a_scr[...] = jnp.dot(p.astype(BF16), v, preferred_element_type=F32)


def _softmax_next(s, v, m_scr, l_scr, a_scr):
    m_old = m_scr[...]
    m = jnp.maximum(m_old, jnp.max(s, axis=-1, keepdims=True))
    alpha = jnp.exp(m_old - m)
    p = jnp.exp(s - m)
    m_scr[...] = m
    l_scr[...] = alpha * l_scr[...] + jnp.sum(p, axis=-1, keepdims=True)
    a_scr[...] = alpha * a_scr[...] + jnp.dot(p.astype(BF16), v, preferred_element_type=F32)


def _da_kernel(lam_ref, q_ref, k_ref, v_ref, w_ref, o_ref, m1, l1, a1, m2, l2, a2,
               *, tq, tk, n_ctx, n_chunks, lam_init):
    q = q_ref[...]
    lane = lax.broadcasted_iota(jnp.int32, (1, LANES), 1)
    zero = jnp.zeros_like(q)
    q1 = jnp.where(lane < DA_HEAD_DIM, q, zero)
    q2 = jnp.where(lane < DA_HEAD_DIM, zero, q)

    kc = k_ref[0:n_ctx, :]
    vc = v_ref[0:n_ctx, :]
    _softmax_first(_nt_dot(q1, kc), vc, m1, l1, a1)
    _softmax_first(_nt_dot(q2, kc), vc, m2, l2, a2)

    @pl.when((pl.program_id(1) + 1) * tq > n_ctx)
    def _():
        def body(c, carry):
            start = pl.multiple_of(n_ctx + c * tk, 16)
            kc = k_ref[pl.ds(start, tk), :]
            vc = v_ref[pl.ds(start, tk), :]
            _softmax_next(_nt_dot(q1, kc), vc, m1, l1, a1)
            _softmax_next(_nt_dot(q2, kc), vc, m2, l2, a2)
            return carry
        lax.fori_loop(0, n_chunks, body, 0)

    lp = lam_ref[...]
    lam = (jnp.exp(jnp.sum(lp[0:1] * lp[1:2], axis=-1, keepdims=True))
           - jnp.exp(jnp.sum(lp[2:3] * lp[3:4], axis=-1, keepdims=True)) + lam_init)
    o = a1[...] / l1[...] - lam * (a2[...] / l2[...])
    o = _rms(o, w_ref[...], DA_SUBLN_EPS) * (1.0 - lam_init)
    o_ref[...] = o.astype(o_ref.dtype)


def _gq_kernel(q_ref, k_ref, v_ref, o_ref, m1, l1, a1, *, tq, tk, n_ctx, n_chunks):
    q = q_ref[...]
    _softmax_first(_nt_dot(q, k_ref[0:n_ctx, :]), v_ref[0:n_ctx, :], m1, l1, a1)

    @pl.when((pl.program_id(1) + 1) * tq > n_ctx)
    def _():
        def body(c, carry):
            start = pl.multiple_of(n_ctx + c * tk, 16)
            _softmax_next(_nt_dot(q, k_ref[pl.ds(start, tk), :]), v_ref[pl.ds(start, tk), :], m1, l1, a1)
            return carry
        lax.fori_loop(0, n_chunks, body, 0)

    o_ref[...] = (a1[...] / l1[...]).astype(o_ref.dtype)


def _attn_tiles(r, n_ctx):
    tq = _tile(n_ctx, 256, 16)
    tk = _tile(r - n_ctx, 1024, LANES)
    return tq, tk


def _diff_attention(lam_params, qa, ka, proj, subln_w, n_ctx, lam_init):
    r = qa.shape[0]
    tq, tk = _attn_tiles(r, n_ctx)
    head = lambda rows: pl.BlockSpec((rows, LANES), lambda h, i: (0, h))
    stat = lambda w: pltpu.VMEM((tq, w), F32)
    return pl.pallas_call(
        functools.partial(_da_kernel, tq=tq, tk=tk, n_ctx=n_ctx, n_chunks=(r - n_ctx) // tk, lam_init=lam_init),
        out_shape=jax.ShapeDtypeStruct((r, DA_WIDTH), BF16),
        grid=(DA_HEADS, r // tq),
        in_specs=[
            pl.BlockSpec((4, DA_HEAD_DIM), lambda h, i: (0, 0)),
            pl.BlockSpec((tq, LANES), lambda h, i: (i, h)),
            head(r),
            pl.BlockSpec((r, LANES), lambda h, i: (0, OFF_VA // LANES + h)),
            pl.BlockSpec((1, LANES), lambda h, i: (0, 0)),
        ],
        out_specs=pl.BlockSpec((tq, LANES), lambda h, i: (i, h)),
        scratch_shapes=[stat(1), stat(1), stat(LANES), stat(1), stat(1), stat(LANES)],
        compiler_params=_cparams("parallel", "parallel"),
        name="diff_attention",
    )(lam_params, qa, ka, proj, subln_w.reshape(1, LANES))


def _gq_attention(qb, kb, proj, n_ctx):
    r = qb.shape[0]
    tq, tk = _attn_tiles(r, n_ctx)
    stat = lambda w: pltpu.VMEM((tq, w), F32)
    return pl.pallas_call(
        functools.partial(_gq_kernel, tq=tq, tk=tk, n_ctx=n_ctx, n_chunks=(r - n_ctx) // tk),
        out_shape=jax.ShapeDtypeStruct((r, GQ_WIDTH), BF16),
        grid=(GQ_HEADS, r // tq),
        in_specs=[
            pl.BlockSpec((tq, LANES), lambda h, i: (i, h)),
            pl.BlockSpec((r, LANES), lambda h, i: (0, h // GQ_GROUP)),
            pl.BlockSpec((r, LANES), lambda h, i: (0, OFF_VB // LANES + h // GQ_GROUP)),
        ],
        out_specs=pl.BlockSpec((tq, LANES), lambda h, i: (i, h)),
        scratch_shapes=[stat(1), stat(1), stat(LANES)],
        compiler_params=_cparams("parallel", "parallel"),
        name="gq_attention",
    )(qb, kb, proj)


def _s5_matrices(a_re, a_im, log_dt, b_re, b_im, c_re, c_im, d_skip):
    t = S5_CHUNK
    g, p, h = S5_GROUPS, S5_STATE, S5_GROUP
    a_re = jnp.minimum(a_re.astype(F32), S5_MAX_REAL)
    a_im = a_im.astype(F32)
    dt = jnp.exp(log_dt.astype(F32))[..., None]
    mag = jnp.exp(a_re * dt)
    lr = mag * jnp.cos(a_im * dt)
    li = mag * jnp.sin(a_im * dt)
    den = jnp.square(a_re) + jnp.square(a_im)
    nr = lr - 1.0
    cr = (nr * a_re + li * a_im) / den
    ci = (li * a_re - nr * a_im) / den
    bbr = cr[..., None] * b_re - ci[..., None] * b_im
    bbi = cr[..., None] * b_im + ci[..., None] * b_re
    j = jnp.arange(t + 1, dtype=F32)[:, None, None, None]
    pmag = jnp.exp(j * (a_re * dt)[None])
    pr = pmag * jnp.cos(j * (a_im * dt)[None])
    pi = pmag * jnp.sin(j * (a_im * dt)[None])
    clr = c_re[None] * pr[:, :, :, None, :] - c_im[None] * pi[:, :, :, None, :]
    cli = c_re[None] * pi[:, :, :, None, :] + c_im[None] * pr[:, :, :, None, :]
    kern = (jnp.einsum('jdghp,dgpk->jdghk', clr[:t], bbr) - jnp.einsum('jdghp,dgpk->jdghk', cli[:t], bbi))
    eye = jnp.eye(h, dtype=F32)[None] * d_skip.astype(F32).reshape(g, h)[:, :, None]
    k0 = kern[0, 0] + kern[0, 1] + eye
    kcat = jnp.concatenate([kern[1:, 1][::-1], k0[None], kern[1:, 0]], axis=0)
    idx = (jnp.arange(t)[None, :] - jnp.arange(t)[:, None]) + (t - 1)
    m = kcat[idx]
    m = m.transpose(2, 0, 4, 1, 3).reshape(g, t * h, t * h)

    def to_state(d, pw):
        wr = pw[0][:, :, :, None] * bbr[d][None] - pw[1][:, :, :, None] * bbi[d][None]
        wi = pw[0][:, :, :, None] * bbi[d][None] + pw[1][:, :, :, None] * bbr[d][None]
        f = lambda w: w.transpose(1, 0, 3, 2).reshape(g, t * h, p)
        return f(wr), f(wi)

    wf = to_state(0, (pr[:t, 0][::-1], pi[:t, 0][::-1]))
    wb = to_state(1, (pr[:t, 1], pi[:t, 1]))
    mcat = jnp.concatenate([m, wf[0], wf[1], wb[0], wb[1]], axis=-1)

    def from_state(d, cl_r, cl_i):
        f = lambda c: c.transpose(1, 3, 0, 2).reshape(g, p, t * h)
        return f(cl_r), -f(cl_i)

    nf = from_state(0, clr[1:, 0], cli[1:, 0])
    nb = from_state(1, clr[1:, 1][::-1], cli[1:, 1][::-1])
    ncat = jnp.concatenate([nf[0], nf[1], nb[0], nb[1]], axis=1)

    def coef(d):
        r_, i_ = pr[t, d], pi[t, d]
        return [jnp.concatenate([r_, r_], -1), jnp.concatenate([-i_, i_], -1), jnp.concatenate([i_, -i_], -1)]

    coefs = jnp.stack(coef(0) + coef(1))
    return mcat.astype(BF16), ncat.astype(BF16), coefs


def _s5_a_kernel(u_ref, m_ref, y_ref, s_ref, *, gb):
    for gl in range(gb):
        r = jnp.dot(u_ref[gl], m_ref[gl], preferred_element_type=F32)
        y_ref[gl] = r[:, :S5_CW]
        s_ref[gl] = r[:, S5_CW:]


def _s5_b_kernel(s_ref, coef_ref, x_ref, *, n_chunks, n_ctx_chunks):
    half = 2 * S5_STATE
    af, bf, btf = coef_ref[0], coef_ref[1], coef_ref[2]
    ab, bb, btb = coef_ref[3], coef_ref[4], coef_ref[5]
    zero = jnp.zeros_like(af)

    def fwd(c, carry):
        x, xt = carry
        s = s_ref[c, :, 0:half]
        x_ref[c, :, 0:half] = x.astype(x_ref.dtype)
        return af * x + bf * xt + s, af * xt + btf * x + pltpu.roll(s, S5_STATE, 1)

    lax.fori_loop(0, n_chunks, fwd, (zero, zero))

    def bwd(c, carry):
        x, xt = carry
        s = s_ref[c, :, half:2 * half]
        x_ref[c, :, half:2 * half] = x.astype(x_ref.dtype)
        return ab * x + bb * xt + s, ab * xt + btb * x + pltpu.roll(s, S5_STATE, 1)

    carry = lax.fori_loop(0, n_ctx_chunks, lambda i, cr: bwd(n_ctx_chunks - 1 - i, cr), (zero, zero))
    lax.fori_loop(0, n_chunks - n_ctx_chunks, lambda i, cr: bwd(n_chunks - 1 - i, cr), carry)


def _s5_c_kernel(x_ref, n_ref, yi_ref, y_ref, *, gb):
    for gl in range(gb):
        y = yi_ref[gl] + jnp.dot(x_ref[gl], n_ref[gl], preferred_element_type=F32)
        y_ref[gl] = y.astype(y_ref.dtype)


def _s5_mixer(u, mcat, ncat, coefs, n_ctx):
    r = u.shape[0]
    g, t, h = S5_GROUPS, S5_CHUNK, S5_GROUP
    nc = r // t
    sw = 4 * S5_STATE
    gb = 8
    ug = u.reshape(nc, t, g, h).transpose(2, 0, 1, 3).reshape(g, nc, S5_CW)
    grp = lambda a, b: pl.BlockSpec((gb, a, b), lambda i: (i, 0, 0))
    y_intra, s = pl.pallas_call(
        functools.partial(_s5_a_kernel, gb=gb),
        out_shape=(jax.ShapeDtypeStruct((g, nc, S5_CW), F32), jax.ShapeDtypeStruct((g, nc, sw), F32)),
        grid=(g // gb,),
        in_specs=[grp(nc, S5_CW), grp(S5_CW, S5_CW + sw)],
        out_specs=(grp(nc, S5_CW), grp(nc, sw)),
        compiler_params=_cparams("parallel"),
        name="s5_chunk_local",
    )(ug, mcat)

    gs = 16
    st = s.transpose(1, 0, 2)
    xt = pl.pallas_call(
        functools.partial(_s5_b_kernel, n_chunks=nc, n_ctx_chunks=n_ctx // t),
        out_shape=jax.ShapeDtypeStruct((nc, g, sw), BF16),
        grid=(g // gs,),
        in_specs=[pl.BlockSpec((nc, gs, sw), lambda i: (0, i, 0)),
                  pl.BlockSpec((6, gs, 2 * S5_STATE), lambda i: (0, i, 0))],
        out_specs=pl.BlockSpec((nc, gs, sw), lambda i: (0, i, 0)),
        compiler_params=_cparams("parallel"),
        name="s5_chunk_scan",
    )(st, coefs)

    x_in = xt.transpose(1, 0, 2)
    yg = pl.pallas_call(
        functools.partial(_s5_c_kernel, gb=gb),
        out_shape=jax.ShapeDtypeStruct((g, nc, S5_CW), BF16),
        grid=(g // gb,),
        in_specs=[grp(nc, sw), grp(sw, S5_CW), grp(nc, S5_CW)],
        out_specs=grp(nc, S5_CW),
        compiler_params=_cparams("parallel"),
        name="s5_state_out",
    )(x_in, ncat, y_intra)
    return yg.reshape(g, nc, t, h).transpose(1, 2, 0, 3).reshape(r, S5_WIDTH)


def _merge_kernel(oa_ref, ob_ref, y_ref, wpa_ref, wpb_ref, wga_ref, wgg_ref, g0_ref, g1_ref, g2_ref,
                  o_ref, gy_scr):
    @pl.when(pl.program_id(1) == 0)
    def _():
        gy_scr[...] = _gelu_tanh(y_ref[...].astype(F32)).astype(BF16)

    dot = lambda a, b: jnp.dot(a, b, preferred_element_type=F32)
    pa = dot(oa_ref[...], wpa_ref[...])
    pb = dot(ob_ref[...], wpb_ref[...])
    gy = gy_scr[...]
    pc = dot(gy, wga_ref[...]) * _sigmoid(dot(gy, wgg_ref[...]))
    sg = lambda ref: _sigmoid(ref[...].astype(F32))
    o_ref[...] = (sg(g0_ref) * pa + sg(g1_ref) * pb + sg(g2_ref) * pc).astype(o_ref.dtype)


def _merge(o_a, o_b, y_s5, proj, w_pa, w_pb, w_glu, d):
    r = o_a.shape[0]
    tm = _tile(r, 1056, 16)
    tn = _tile(math.gcd(d, OFF_GATE), 512, LANES)
    row = lambda w: pl.BlockSpec((tm, w), lambda i, j: (i, 0))
    col = lambda k, off: pl.BlockSpec((k, tn), lambda i, j: (0, off // tn + j))
    gate = lambda b: pl.BlockSpec((tm, tn), lambda i, j: (i, (OFF_GATE + b * d) // tn + j))
    return pl.pallas_call(
        _merge_kernel,
        out_shape=jax.ShapeDtypeStruct((r, d), BF16),
        grid=(r // tm, d // tn),
        in_specs=[row(DA_WIDTH), row(GQ_WIDTH), row(S5_WIDTH),
                  col(DA_WIDTH, 0), col(GQ_WIDTH, 0), col(S5_WIDTH, 0), col(S5_WIDTH, d),
                  gate(0), gate(1), gate(2)],
        out_specs=pl.BlockSpec((tm, tn), lambda i, j: (i, j)),
        scratch_shapes=[pltpu.VMEM((tm, S5_WIDTH), BF16)],
        compiler_params=_cparams("parallel", "arbitrary"),
        name="branch_merge",
    )(o_a, o_b, y_s5, w_pa, w_pb, w_glu, w_glu, proj, proj, proj)


def _out_ln_kernel(m_ref, w_ref, x_ref, gm_ref, lg_ref, lb_ref, o_ref, *, tm, n_ctx, alpha):
    y = jnp.dot(m_ref[...], w_ref[...], preferred_element_type=F32)
    gate = _row_select(gm_ref, pl.program_id(0) * tm, tm, n_ctx)
    o_ref[...] = _layer_norm(alpha * x_ref[...] + gate * y, lg_ref[...], lb_ref[...])


def _out_ln(merged, w_o, xs, mod, ln_g, ln_b, n_ctx, alpha):
    r, d = xs.shape
    tm = _tile(r, 528, 16)
    vec = pl.BlockSpec((1, d), lambda i: (0, 0))
    return pl.pallas_call(
        functools.partial(_out_ln_kernel, tm=tm, n_ctx=n_ctx, alpha=alpha),
        out_shape=jax.ShapeDtypeStruct((r, d), F32),
        grid=(r // tm,),
        in_specs=[
            pl.BlockSpec((tm, d), lambda i: (i, 0)),
            pl.BlockSpec((d, d), lambda i: (0, 0)),
            pl.BlockSpec((tm, d), lambda i: (i, 0)),
            pl.BlockSpec((8, d), lambda i: (0, 2)),
            vec, vec,
        ],
        out_specs=pl.BlockSpec((tm, d), lambda i: (i, 0)),
        compiler_params=_cparams("parallel"),
        name="out_proj_ln",
    )(merged, w_o, xs, mod, ln_g.reshape(1, d), ln_b.reshape(1, d))


def _ffn_kernel(x_ref, sh_ref, sc_ref, gf_ref, wg_ref, wu_ref, wd_ref, lg_ref, lb_ref, o_ref,
                h_scr, acc_scr, *, tm, n_ctx, alpha):
    i = pl.program_id(0)
    f = pl.program_id(1)

    @pl.when(f == 0)
    def _():
        sh = _row_select(sh_ref, i * tm, tm, n_ctx)
        sc = _row_select(sc_ref, i * tm, tm, n_ctx)
        h_scr[...] = (x_ref[...] * (1.0 + sc) + sh).astype(BF16)
        acc_scr[...] = jnp.zeros_like(acc_scr)

    h = h_scr[...]
    gate = jnp.dot(h, wg_ref[...], preferred_element_type=F32)
    up = jnp.dot(h, wu_ref[...], preferred_element_type=F32)
    act = (gate * _sigmoid(gate) * up).astype(BF16)
    acc_scr[...] += jnp.dot(act, wd_ref[...], preferred_element_type=F32)

    @pl.when(f == pl.num_programs(1) - 1)
    def _():
        g = _row_select(gf_ref, i * tm, tm, n_ctx)
        o_ref[...] = _layer_norm(alpha * x_ref[...] + g * acc_scr[...], lg_ref[...], lb_ref[...])


def _ffn(xs, mod, w_gate, w_up, w_down, ln_g, ln_b, n_ctx, alpha):
    r, d = xs.shape
    f = w_gate.shape[1]
    tm = _tile(r, 528, 16)
    tf = _tile(f, 512, LANES)
    vec = pl.BlockSpec((1, d), lambda i, j: (0, 0))
    modv = lambda c: pl.BlockSpec((8, d), lambda i, j: (0, c))
    return pl.pallas_call(
        functools.partial(_ffn_kernel, tm=tm, n_ctx=n_ctx, alpha=alpha),
        out_shape=jax.ShapeDtypeStruct((r, d), F32),
        grid=(r // tm, f // tf),
        in_specs=[
            pl.BlockSpec((tm, d), lambda i, j: (i, 0)),
            modv(3), modv(4), modv(5),
            pl.BlockSpec((d, tf), lambda i, j: (0, j)),
            pl.BlockSpec((d, tf), lambda i, j: (0, j)),
            pl.BlockSpec((tf, d), lambda i, j: (j, 0)),
            vec, vec,
        ],
        out_specs=pl.BlockSpec((tm, d), lambda i, j: (i, 0)),
        scratch_shapes=[pltpu.VMEM((tm, d), BF16), pltpu.VMEM((tm, d), F32)],
        compiler_params=_cparams("parallel", "arbitrary"),
        name="swiglu_ln",
    )(xs, mod, mod, mod, w_gate, w_up, w_down, ln_g.reshape(1, d), ln_b.reshape(1, d))


def kernel(x, c, ctx, c_ctx, w_ada, b_ada, w_in, da_lam_q1, da_lam_k1, da_lam_q2, da_lam_k2, da_subln_w, w_pa, gq_qnorm_w, gq_knorm_w, w_pb, s5_a_re, s5_a_im, s5_log_dt, s5_b_re, s5_b_im, s5_c_re, s5_c_im, s5_d, w_glu, w_o, ln_mix_g, ln_mix_b, w_ffn_gate, w_ffn_up, w_ffn_down, ln_ffn_g, ln_ffn_b):
    batch, n_lat, d = x.shape
    n_ctx = ctx.shape[1]
    depth = w_in.shape[0]
    assert batch == 1 and n_lat % GRID_W == 0 and n_ctx % S5_CHUNK == 0 and n_lat % S5_CHUNK == 0
    alpha = (2 * depth) ** 0.25

    xs = jnp.concatenate([ctx[0], x[0]], axis=0).astype(F32)
    c_rows = jnp.zeros((8, d), F32).at[0].set(c[0]).at[1].set(c_ctx)
    mods = _ada_mod(c_rows, w_ada, b_ada)
    tab_da = _rope_tables(n_lat, n_ctx, DA_HEAD_DIM)
    tab_gq = _rope_tables(n_lat, n_ctx, GQ_HEAD_DIM)

    for l in range(depth):
        lam_init = 0.8 - 0.6 * math.exp(-0.3 * l)
        mod = mods[l]
        proj = _in_proj(xs, mod, w_in[l].astype(BF16), n_ctx)
        qa, ka, qb, kb = _prep_qk(proj, tab_da, tab_gq, gq_qnorm_w[l], gq_knorm_w[l])
        lam_params = jnp.stack([da_lam_q1[l], da_lam_k1[l], da_lam_q2[l], da_lam_k2[l]]).astype(F32)
        o_a = _diff_attention(lam_params, qa, ka, proj, da_subln_w[l], n_ctx, lam_init)
        o_b = _gq_attention(qb, kb, proj, n_ctx)
        mcat, ncat, coefs = _s5_matrices(s5_a_re[l], s5_a_im[l], s5_log_dt[l], s5_b_re[l], s5_b_im[l],
                                         s5_c_re[l], s5_c_im[l], s5_d[l])
        y_s5 = _s5_mixer(proj[:, OFF_U:OFF_U + S5_WIDTH], mcat, ncat, coefs, n_ctx)
        merged = _merge(o_a, o_b, y_s5, proj, w_pa[l].astype(BF16), w_pb[l].astype(BF16),
                        w_glu[l].astype(BF16), d)
        xs = _out_ln(merged, w_o[l].astype(BF16), xs, mod, ln_mix_g[l], ln_mix_b[l], n_ctx, alpha)
        xs = _ffn(xs, mod, w_ffn_gate[l].astype(BF16), w_ffn_up[l].astype(BF16), w_ffn_down[l].astype(BF16),
                  ln_ffn_g[l], ln_ffn_b[l], n_ctx, alpha)
    return xs[n_ctx:][None].astype(x.dtype)
```

```python
import functools
import math

import jax
import jax.numpy as jnp
from jax import lax
from jax.experimental import pallas as pl
from jax.experimental.pallas import tpu as pltpu

F32 = jnp.float32
BF16 = jnp.bfloat16

GRID_W = 64
DA_HEADS = 8
DA_HEAD_DIM = 64
DA_WIDTH = DA_HEADS * 2 * DA_HEAD_DIM
GQ_HEADS = 8
GQ_KV_HEADS = 2
GQ_GROUP = GQ_HEADS // GQ_KV_HEADS
GQ_HEAD_DIM = 128
GQ_WIDTH = GQ_HEADS * GQ_HEAD_DIM
GQ_KV_WIDTH = GQ_KV_HEADS * GQ_HEAD_DIM
S5_WIDTH = 1024
S5_GROUP = 16
S5_GROUPS = S5_WIDTH // S5_GROUP
S5_STATE = 64
S5_MAX_REAL = -1e-4
N_BRANCH = 3
ROPE_THETA = 10000.0
LN_EPS = 1e-6
RMS_EPS = 1e-6
DA_SUBLN_EPS = 1e-5

OFF_QA = 0
OFF_KA = OFF_QA + DA_WIDTH
OFF_VA = OFF_KA + DA_WIDTH
OFF_QB = OFF_VA + DA_WIDTH
OFF_KB = OFF_QB + GQ_WIDTH
OFF_VB = OFF_KB + GQ_KV_WIDTH
OFF_U = OFF_VB + GQ_KV_WIDTH
OFF_GATE = OFF_U + S5_WIDTH

LANES = 128
MXU_DIM = 256
V7X_VMEM_BYTES = 64 * 1024 * 1024
VMEM_LIMIT = V7X_VMEM_BYTES - 8 * 1024 * 1024

S5_CHUNK = 16
S5_CW = S5_CHUNK * S5_GROUP
S5_SCAN_UNROLL = 8

LOG2E = math.log2(math.e)
NEG_BIG = -1e30


def _cparams(*sem):
    return pltpu.CompilerParams(dimension_semantics=sem, vmem_limit_bytes=VMEM_LIMIT)


def _tile(n, target, mult):
    best = None
    for t in range(mult, min(n, target) + 1, mult):
        if n % t == 0:
            best = t
    assert best is not None, (n, target, mult)
    return best


def _const_spec(block_shape):
    return pl.BlockSpec(block_shape, lambda *_: (0,) * len(block_shape))


def _sigmoid(x):
    return 1.0 / (1.0 + jnp.exp(-x))


def _gelu_tanh(x):
    return 0.5 * x * (1.0 + jnp.tanh(math.sqrt(2.0 / math.pi) * (x + 0.044715 * (x * x * x))))


def _row_select(vec_ref, row0, tm, n_lat):
    rows = row0 + lax.broadcasted_iota(jnp.int32, (tm, 1), 0)
    return jnp.where(rows < n_lat, vec_ref[0:1, :], vec_ref[1:2, :])


def _layer_norm(z, g, b):
    mu = jnp.mean(z, axis=-1, keepdims=True)
    zc = z - mu
    var = jnp.mean(zc * zc, axis=-1, keepdims=True)
    return zc * lax.rsqrt(var + LN_EPS) * g + b


def _mod_kernel(c_ref, w_ref, b_ref, o_ref):
    a = c_ref[...]
    a = a * _sigmoid(a)
    o_ref[0] = jnp.dot(a.astype(BF16), w_ref[0].astype(BF16), preferred_element_type=F32) + b_ref[0]


def _ada_mod(c_rows, w_ada, b_ada):
    depth, d, n6 = w_ada.shape
    tn = _tile(n6, 1024, LANES)
    return pl.pallas_call(
        _mod_kernel,
        out_shape=jax.ShapeDtypeStruct((depth, 8, n6), F32),
        grid=(depth, n6 // tn),
        in_specs=[
            pl.BlockSpec((8, d), lambda l, j: (0, 0)),
            pl.BlockSpec((1, d, tn), lambda l, j: (l, 0, j)),
            pl.BlockSpec((1, 1, tn), lambda l, j: (l, 0, j)),
        ],
        out_specs=pl.BlockSpec((1, 8, tn), lambda l, j: (l, 0, j)),
        compiler_params=_cparams("parallel", "parallel"),
        name="ada_mod",
    )(c_rows, w_ada, b_ada.reshape(depth, 1, n6))


def _in_proj_kernel(x_ref, sh_ref, sc_ref, w_ref, o_ref, h_scr, *, tm, n_lat):
    i = pl.program_id(0)

    @pl.when(pl.program_id(1) == 0)
    def _():
        sh = _row_select(sh_ref, i * tm, tm, n_lat)
        sc = _row_select(sc_ref, i * tm, tm, n_lat)
        h_scr[...] = (x_ref[...] * (1.0 + sc) + sh).astype(BF16)

    o_ref[...] = jnp.dot(h_scr[...], w_ref[...], preferred_element_type=F32).astype(o_ref.dtype)


def _in_proj(xs, mod, w_in, n_lat):
    r, d = xs.shape
    n = w_in.shape[1]
    tm = _tile(r, 1056, 16)
    tn = _tile(n, 512, LANES)
    return pl.pallas_call(
        functools.partial(_in_proj_kernel, tm=tm, n_lat=n_lat),
        out_shape=jax.ShapeDtypeStruct((r, n), BF16),
        grid=(r // tm, n // tn),
        in_specs=[
            pl.BlockSpec((tm, d), lambda i, j: (i, 0)),
            pl.BlockSpec((8, d), lambda i, j: (0, 0)),
            pl.BlockSpec((8, d), lambda i, j: (0, 1)),
            pl.BlockSpec((d, tn), lambda i, j: (0, j)),
        ],
        out_specs=pl.BlockSpec((tm, tn), lambda i, j: (i, j)),
        scratch_shapes=[pltpu.VMEM((tm, d), BF16)],
        compiler_params=_cparams("parallel", "arbitrary"),
        name="in_proj",
    )(xs, mod, mod, w_in)


def _rope(z, cos, sin_a, sin_b, quarter):
    return z * cos + pltpu.roll(z, LANES - quarter, 1) * sin_a + pltpu.roll(z, quarter, 1) * sin_b


def _rms(z, w, eps):
    return z * lax.rsqrt(jnp.mean(z * z, axis=-1, keepdims=True) + eps) * w


def _prep_kernel(qa_ref, ka_ref, qb_ref, kb_ref, tda_ref, tgq_ref, qn_ref, kn_ref,
                 qa_o, kat_o, qb_o, kbt_o):
    cda, sda_a, sda_b = tda_ref[0], tda_ref[1], tda_ref[2]
    cgq, sgq_a, sgq_b = tgq_ref[0], tgq_ref[1], tgq_ref[2]
    da_scale = DA_HEAD_DIM ** -0.5 * LOG2E
    gq_scale = GQ_HEAD_DIM ** -0.5 * LOG2E
    for h in range(DA_HEADS):
        sl = slice(h * LANES, (h + 1) * LANES)
        z = qa_ref[:, sl].astype(F32)
        qa_o[:, sl] = (_rope(z, cda, sda_a, sda_b, DA_HEAD_DIM // 4) * da_scale).astype(BF16)
        z = ka_ref[:, sl].astype(F32)
        kat_o[h, 0] = _rope(z, cda, sda_a, sda_b, DA_HEAD_DIM // 4).T.astype(BF16)
    for h in range(GQ_HEADS):
        sl = slice(h * LANES, (h + 1) * LANES)
        z = _rms(qb_ref[:, sl].astype(F32), qn_ref[...], RMS_EPS)
        qb_o[:, sl] = (_rope(z, cgq, sgq_a, sgq_b, GQ_HEAD_DIM // 4) * gq_scale).astype(BF16)
    for h in range(GQ_KV_HEADS):
        sl = slice(h * LANES, (h + 1) * LANES)
        z = _rms(kb_ref[:, sl].astype(F32), kn_ref[...], RMS_EPS)
        kbt_o[h, 0] = _rope(z, cgq, sgq_a, sgq_b, GQ_HEAD_DIM // 4).T.astype(BF16)


def _prep_qk(proj, tab_da, tab_gq, qn_w, kn_w, tk):
    r = proj.shape[0]
    nk = r // tk
    blk = lambda w, off: pl.BlockSpec((tk, w), lambda i: (i, off // w))
    out = lambda w: pl.BlockSpec((tk, w), lambda i: (i, 0))
    outt = lambda heads: pl.BlockSpec((heads, 1, LANES, tk), lambda i: (0, i, 0, 0))
    return pl.pallas_call(
        _prep_kernel,
        out_shape=(
            jax.ShapeDtypeStruct((r, DA_WIDTH), BF16),
            jax.ShapeDtypeStruct((DA_HEADS, nk, LANES, tk), BF16),
            jax.ShapeDtypeStruct((r, GQ_WIDTH), BF16),
            jax.ShapeDtypeStruct((GQ_KV_HEADS, nk, LANES, tk), BF16),
        ),
        grid=(nk,),
        in_specs=[
            blk(DA_WIDTH, OFF_QA), blk(DA_WIDTH, OFF_KA), blk(GQ_WIDTH, OFF_QB), blk(GQ_KV_WIDTH, OFF_KB),
            pl.BlockSpec((3, tk, LANES), lambda i: (0, i, 0)),
            pl.BlockSpec((3, tk, LANES), lambda i: (0, i, 0)),
            _const_spec((1, LANES)),
            _const_spec((1, LANES)),
        ],
        out_specs=(out(DA_WIDTH), outt(DA_HEADS), out(GQ_WIDTH), outt(GQ_KV_HEADS)),
        compiler_params=_cparams("parallel"),
        name="prep_qk",
    )(proj, proj, proj, proj, tab_da, tab_gq, qn_w.reshape(1, LANES), kn_w.reshape(1, LANES))


def _rope_tables(n_lat, n_ctx, dim):
    rows = n_lat // GRID_W
    row = jnp.broadcast_to(jnp.arange(rows, dtype=F32)[:, None], (rows, GRID_W)).reshape(-1)
    col = jnp.broadcast_to(jnp.arange(GRID_W, dtype=F32)[None, :], (rows, GRID_W)).reshape(-1)
    axis_dim = dim // 2
    inv_freq = ROPE_THETA ** (-jnp.arange(0, axis_dim, 2, dtype=F32) / axis_dim)
    ang_r = row[:, None] * inv_freq[None, :]
    ang_c = col[:, None] * inv_freq[None, :]
    ang = jnp.concatenate([ang_r, ang_r, ang_c, ang_c], axis=-1)
    cos, sin = jnp.cos(ang), jnp.sin(ang)
    first_half = (jnp.arange(dim) % (dim // 2)) < (dim // 4)
    sin_a = jnp.where(first_half[None, :], -sin, 0.0)
    sin_b = jnp.where(first_half[None, :], 0.0, sin)
    tab = jnp.stack([cos, sin_a, sin_b])
    ident = jnp.stack([jnp.ones((n_ctx, dim), F32), jnp.zeros((n_ctx, dim), F32), jnp.zeros((n_ctx, dim), F32)])
    tab = jnp.concatenate([tab, ident], axis=1)
    return jnp.tile(tab, (1, 1, LANES // dim))


def _flash_step(s_buf, v, m_scr, l_scr, a_scr):
    s = s_buf[...]
    m_old = m_scr[...]
    m = jnp.maximum(m_old, jnp.max(s, axis=-1, keepdims=True))
    alpha = jnp.exp2(m_old - m)
    p = jnp.exp2(s - m)
    m_scr[...] = m
    l_scr[...] = alpha * l_scr[...] + jnp.sum(p, axis=-1, keepdims=True)
    a_scr[...] = alpha * a_scr[...] + jnp.dot(p.astype(BF16), v, preferred_element_type=F32)


def _flash_loop(qs, kt_ref, v_ref, bufs, stats, tk, n_chunks):
    for m_scr, l_scr, a_scr in stats:
        m_scr[...] = jnp.full_like(m_scr, NEG_BIG)
        l_scr[...] = jnp.zeros_like(l_scr)
        a_scr[...] = jnp.zeros_like(a_scr)

    def scores(c, slot):
        kt = kt_ref[c]
        for q, buf in zip(qs, bufs):
            buf[slot][...] = jnp.dot(q, kt, preferred_element_type=F32)

    def update(c, slot):
        v = v_ref[pl.ds(pl.multiple_of(c * tk, tk), tk), :]
        for buf, st in zip(bufs, stats):
            _flash_step(buf[slot], v, *st)

    scores(0, 0)
    pairs = (n_chunks - 1) // 2

    def body(j, carry):
        c = 2 * j
        scores(c + 1, 1)
        update(c, 0)
        scores(c + 2, 0)
        update(c + 1, 1)
        return carry

    lax.fori_loop(0, pairs, body, 0)
    c = 2 * pairs
    if n_chunks % 2 == 1:
        update(c, 0)
    else:
        scores(c + 1, 1)
        update(c, 0)
        update(c + 1, 1)


def _softmax_once(s, v):
    p = jnp.exp2(s - jnp.max(s, axis=-1, keepdims=True))
    return jnp.dot(p.astype(BF16), v, preferred_element_type=F32) / jnp.sum(p, axis=-1, keepdims=True)


def _da_split(q):
    lane = lax.broadcasted_iota(jnp.int32, (1, LANES), 1)
    zero = jnp.zeros_like(q)
    return jnp.where(lane < DA_HEAD_DIM, q, zero), jnp.where(lane < DA_HEAD_DIM, zero, q)


def _da_finish(o1, o2, lam_ref, w_ref, o_ref, lam_init):
    lp = lam_ref[...]
    lam = (jnp.exp(jnp.sum(lp[0:1] * lp[1:2], axis=-1, keepdims=True))
           - jnp.exp(jnp.sum(lp[2:3] * lp[3:4], axis=-1, keepdims=True)) + lam_init)
    o = _rms(o1 - lam * o2, w_ref[...], DA_SUBLN_EPS) * (1.0 - lam_init)
    o_ref[...] = o.astype(o_ref.dtype)


def _da_kernel(lam_ref, w_ref, q_ref, kt_ref, v_ref, o_ref, s1a, s1b, m1, l1, a1, s2a, s2b, m2, l2, a2,
               *, tk, n_chunks, lam_init):
    q1, q2 = _da_split(q_ref[...])
    _flash_loop((q1, q2), kt_ref, v_ref, ((s1a, s1b), (s2a, s2b)), ((m1, l1, a1), (m2, l2, a2)), tk, n_chunks)
    _da_finish(a1[...] / l1[...], a2[...] / l2[...], lam_ref, w_ref, o_ref, lam_init)


def _da_ctx_kernel(lam_ref, w_ref, q_ref, kt_ref, v_ref, latent_out_ref, o_ref, *, off, n_ctx, lam_init):
    del latent_out_ref
    q1, q2 = _da_split(q_ref[...])
    kt = kt_ref[:, off:off + n_ctx]
    v = v_ref[...]
    o1 = _softmax_once(jnp.dot(q1, kt, preferred_element_type=F32), v)
    o2 = _softmax_once(jnp.dot(q2, kt, preferred_element_type=F32), v)
    _da_finish(o1, o2, lam_ref, w_ref, o_ref, lam_init)


def _gq_kernel(q_ref, kt_ref, v_ref, o_ref, sa, sb, m1, l1, a1, *, tk, n_chunks):
    _flash_loop((q_ref[...],), kt_ref, v_ref, ((sa, sb),), ((m1, l1, a1),), tk, n_chunks)
    o_ref[...] = (a1[...] / l1[...]).astype(o_ref.dtype)


def _gq_ctx_kernel(q_ref, kt_ref, v_ref, latent_out_ref, o_ref, *, off, n_ctx):
    del latent_out_ref
    kt = kt_ref[:, off:off + n_ctx]
    o_ref[...] = _softmax_once(jnp.dot(q_ref[...], kt, preferred_element_type=F32), v_ref[...]).astype(o_ref.dtype)


def _attention(params, q, kt, proj, v_off, kv_group, n_lat, main_kernel, ctx_kernel, n_maps, name):
    r, width = q.shape
    n_heads = width // LANES
    n_ctx = r - n_lat
    _, n_chunks, _, tk = kt.shape
    tq = _tile(n_lat, 512, 16)
    ctx_blk = n_lat // n_ctx
    off = n_lat - (n_chunks - 1) * tk
    param_specs = [_const_spec(p.shape) for p in params]
    stat = pltpu.VMEM((tq, 1), F32)
    scores = pltpu.VMEM((tq, tk), F32)
    scratch = n_maps * [scores, scores, stat, stat, pltpu.VMEM((tq, LANES), F32)]
    out = pl.pallas_call(
        functools.partial(main_kernel, tk=tk, n_chunks=n_chunks),
        out_shape=jax.ShapeDtypeStruct((r, width), BF16),
        grid=(n_heads, n_lat // tq),
        in_specs=param_specs + [
            pl.BlockSpec((tq, LANES), lambda h, i: (i, h)),
            pl.BlockSpec((None, n_chunks, LANES, tk), lambda h, i: (h // kv_group, 0, 0, 0)),
            pl.BlockSpec((r, LANES), lambda h, i: (0, v_off // LANES + h // kv_group)),
        ],
        out_specs=pl.BlockSpec((tq, LANES), lambda h, i: (i, h)),
        scratch_shapes=scratch,
        compiler_params=_cparams("parallel", "arbitrary"),
        name=name,
    )(*params, q, kt, proj)
    return pl.pallas_call(
        functools.partial(ctx_kernel, off=off, n_ctx=n_ctx),
        out_shape=jax.ShapeDtypeStruct((r, width), BF16),
        grid=(n_heads,),
        in_specs=param_specs + [
            pl.BlockSpec((n_ctx, LANES), lambda h: (ctx_blk, h)),
            pl.BlockSpec((None, None, LANES, tk), lambda h: (h // kv_group, n_chunks - 1, 0, 0)),
            pl.BlockSpec((n_ctx, LANES), lambda h: (ctx_blk, v_off // LANES + h // kv_group)),
            pl.BlockSpec(memory_space=pl.ANY),
        ],
        out_specs=pl.BlockSpec((n_ctx, LANES), lambda h: (ctx_blk, h)),
        input_output_aliases={len(params) + 3: 0},
        compiler_params=_cparams("parallel"),
        name=name + "_ctx",
    )(*params, q, kt, proj, out)


def _diff_attention(lam_params, subln_w, qa, kat, proj, n_lat, lam_init):
    return _attention((lam_params, subln_w.reshape(1, LANES)), qa, kat, proj, OFF_VA, 1, n_lat,
                      functools.partial(_da_kernel, lam_init=lam_init),
                      functools.partial(_da_ctx_kernel, lam_init=lam_init), 2, "diff_attention")


def _gq_attention(qb, kbt, proj, n_lat):
    return _attention((), qb, kbt, proj, OFF_VB, GQ_GROUP, n_lat, _gq_kernel, _gq_ctx_kernel, 1, "gq_attention")


def _s5_matrices(a_re, a_im, log_dt, b_re, b_im, c_re, c_im, d_skip):
    t = S5_CHUNK
    g, p, h = S5_GROUPS, S5_STATE, S5_GROUP
    a_re = jnp.minimum(a_re.astype(F32), S5_MAX_REAL)
    a_im = a_im.astype(F32)
    dt = jnp.exp(log_dt.astype(F32))[..., None]
    mag = jnp.exp(a_re * dt)
    lr = mag * jnp.cos(a_im * dt)
    li = mag * jnp.sin(a_im * dt)
    den = jnp.square(a_re) + jnp.square(a_im)
    nr = lr - 1.0
    cr = (nr * a_re + li * a_im) / den
    ci = (li * a_re - nr * a_im) / den
    bbr = cr[..., None] * b_re - ci[..., None] * b_im
    bbi = cr[..., None] * b_im + ci[..., None] * b_re
    j = jnp.arange(t + 1, dtype=F32)[:, None, None, None]
    pmag = jnp.exp(j * (a_re * dt)[None])
    pr = pmag * jnp.cos(j * (a_im * dt)[None])
    pi = pmag * jnp.sin(j * (a_im * dt)[None])
    clr = c_re[None] * pr[:, :, :, None, :] - c_im[None] * pi[:, :, :, None, :]
    cli = c_re[None] * pi[:, :, :, None, :] + c_im[None] * pr[:, :, :, None, :]
    kern = (jnp.einsum('jdghp,dgpk->jdghk', clr[:t], bbr) - jnp.einsum('jdghp,dgpk->jdghk', cli[:t], bbi))
    eye = jnp.eye(h, dtype=F32)[None] * d_skip.astype(F32).reshape(g, h)[:, :, None]
    k0 = kern[0, 0] + kern[0, 1] + eye
    kcat = jnp.concatenate([kern[1:, 1][::-1], k0[None], kern[1:, 0]], axis=0)
    idx = (jnp.arange(t)[None, :] - jnp.arange(t)[:, None]) + (t - 1)
    m = kcat[idx]
    m = m.transpose(2, 0, 4, 1, 3).reshape(g, t * h, t * h)

    def to_state(d, pw):
        wr = pw[0][:, :, :, None] * bbr[d][None] - pw[1][:, :, :, None] * bbi[d][None]
        wi = pw[0][:, :, :, None] * bbi[d][None] + pw[1][:, :, :, None] * bbr[d][None]
        f = lambda w: w.transpose(1, 0, 3, 2).reshape(g, t * h, p)
        return f(wr), f(wi)

    wf = to_state(0, (pr[:t, 0][::-1], pi[:t, 0][::-1]))
    wb = to_state(1, (pr[:t, 1], pi[:t, 1]))
    mcat = jnp.concatenate([m, wf[0], wf[1], wb[0], wb[1]], axis=-1)

    def from_state(d, cl_r, cl_i):
        f = lambda c: c.transpose(1, 3, 0, 2).reshape(g, p, t * h)
        return f(cl_r), -f(cl_i)

    nf = from_state(0, clr[1:, 0], cli[1:, 0])
    nb = from_state(1, clr[1:, 1][::-1], cli[1:, 1][::-1])
    ncat = jnp.concatenate([nf[0], nf[1], nb[0], nb[1]], axis=1)

    def coef(d):
        r_, i_ = pr[t, d], pi[t, d]
        return [jnp.concatenate([r_, r_], -1), jnp.concatenate([-i_, i_], -1), jnp.concatenate([i_, -i_], -1)]

    coefs = jnp.stack(coef(0) + coef(1))
    return mcat.astype(BF16), ncat.astype(BF16), coefs


def _s5_a_kernel(u_ref, m_ref, y_ref, s_ref, *, gb):
    for gl in range(gb):
        r = jnp.dot(u_ref[gl], m_ref[gl], preferred_element_type=F32)
        y_ref[gl] = r[:, :S5_CW]
        s_ref[gl] = r[:, S5_CW:]


def _s5_b_kernel(s_ref, coef_ref, x_ref, sw_scr, *, n_chunks, n_lat_chunks):
    half = 2 * S5_STATE
    n_ctx_chunks = n_chunks - n_lat_chunks
    fwd_sl, bwd_sl = slice(0, half), slice(half, 2 * half)

    def swap(c, carry):
        sw_scr[c, :, fwd_sl] = pltpu.roll(s_ref[c, :, fwd_sl], S5_STATE, 1)
        sw_scr[c, :, bwd_sl] = pltpu.roll(s_ref[c, :, bwd_sl], S5_STATE, 1)
        return carry

    lax.fori_loop(0, n_chunks, swap, 0, unroll=S5_SCAN_UNROLL)

    def step(sl, a, b, bt):
        def f(c, carry):
            x, xt = carry
            x_ref[c, :, sl] = x.astype(x_ref.dtype)
            return a * x + b * xt + s_ref[c, :, sl], a * xt + bt * x + sw_scr[c, :, sl]
        return f

    fwd = step(fwd_sl, coef_ref[0], coef_ref[1], coef_ref[2])
    bwd = step(bwd_sl, coef_ref[3], coef_ref[4], coef_ref[5])
    zero = jnp.zeros_like(coef_ref[0])
    loop = functools.partial(lax.fori_loop, unroll=S5_SCAN_UNROLL)
    carry = loop(0, n_ctx_chunks, lambda i, cr: fwd(n_lat_chunks + i, cr), (zero, zero))
    loop(0, n_lat_chunks, fwd, carry)
    carry = loop(0, n_ctx_chunks, lambda i, cr: bwd(n_chunks - 1 - i, cr), (zero, zero))
    loop(0, n_lat_chunks, lambda i, cr: bwd(n_lat_chunks - 1 - i, cr), carry)


def _s5_c_kernel(x_ref, n_ref, yi_ref, y_ref, *, gb):
    for gl in range(gb):
        y = yi_ref[gl] + jnp.dot(x_ref[gl], n_ref[gl], preferred_element_type=F32)
        y_ref[gl] = y.astype(y_ref.dtype)


def _s5_mixer(u, mcat, ncat, coefs, n_lat):
    r = u.shape[0]
    g, t, h = S5_GROUPS, S5_CHUNK, S5_GROUP
    nc = r // t
    sw = 4 * S5_STATE
    gb = 8
    ug = u.reshape(nc, t, g, h).transpose(2, 0, 1, 3).reshape(g, nc, S5_CW)
    grp = lambda a, b: pl.BlockSpec((gb, a, b), lambda i: (i, 0, 0))
    y_intra, s = pl.pallas_call(
        functools.partial(_s5_a_kernel, gb=gb),
        out_shape=(jax.ShapeDtypeStruct((g, nc, S5_CW), F32), jax.ShapeDtypeStruct((g, nc, sw), F32)),
        grid=(g // gb,),
        in_specs=[grp(nc, S5_CW), grp(S5_CW, S5_CW + sw)],
        out_specs=(grp(nc, S5_CW), grp(nc, sw)),
        compiler_params=_cparams("parallel"),
        name="s5_chunk_local",
    )(ug, mcat)

    gs = 16
    st = s.transpose(1, 0, 2)
    xt = pl.pallas_call(
        functools.partial(_s5_b_kernel, n_chunks=nc, n_lat_chunks=n_lat // t),
        out_shape=jax.ShapeDtypeStruct((nc, g, sw), BF16),
        grid=(g // gs,),
        in_specs=[pl.BlockSpec((nc, gs, sw), lambda i: (0, i, 0)),
                  pl.BlockSpec((6, gs, 2 * S5_STATE), lambda i: (0, i, 0))],
        out_specs=pl.BlockSpec((nc, gs, sw), lambda i: (0, i, 0)),
        scratch_shapes=[pltpu.VMEM((nc, gs, sw), F32)],
        compiler_params=_cparams("parallel"),
        name="s5_chunk_scan",
    )(st, coefs)

    x_in = xt.transpose(1, 0, 2)
    yg = pl.pallas_call(
        functools.partial(_s5_c_kernel, gb=gb),
        out_shape=jax.ShapeDtypeStruct((g, nc, S5_CW), BF16),
        grid=(g // gb,),
        in_specs=[grp(nc, sw), grp(sw, S5_CW), grp(nc, S5_CW)],
        out_specs=grp(nc, S5_CW),
        compiler_params=_cparams("parallel"),
        name="s5_state_out",
    )(x_in, ncat, y_intra)
    return yg.reshape(g, nc, t, h).transpose(1, 2, 0, 3).reshape(r, S5_WIDTH)


def _merge_kernel(oa_ref, ob_ref, y_ref, wpa_ref, wpb_ref, wga_ref, wgg_ref, g0_ref, g1_ref, g2_ref,
                  o_ref, gy_scr):
    @pl.when(pl.program_id(1) == 0)
    def _():
        gy_scr[...] = _gelu_tanh(y_ref[...].astype(F32)).astype(BF16)

    dot = lambda a, b: jnp.dot(a, b, preferred_element_type=F32)
    pa = dot(oa_ref[...], wpa_ref[...])
    pb = dot(ob_ref[...], wpb_ref[...])
    gy = gy_scr[...]
    pc = dot(gy, wga_ref[...]) * _sigmoid(dot(gy, wgg_ref[...]))
    sg = lambda ref: _sigmoid(ref[...].astype(F32))
    o_ref[...] = (sg(g0_ref) * pa + sg(g1_ref) * pb + sg(g2_ref) * pc).astype(o_ref.dtype)


def _merge(o_a, o_b, y_s5, proj, w_pa, w_pb, w_glu, d):
    r = o_a.shape[0]
    tm = _tile(r, 1056, 16)
    tn = _tile(math.gcd(d, OFF_GATE), 512, LANES)
    row = lambda w: pl.BlockSpec((tm, w), lambda i, j: (i, 0))
    col = lambda k, off: pl.BlockSpec((k, tn), lambda i, j: (0, off // tn + j))
    gate = lambda b: pl.BlockSpec((tm, tn), lambda i, j: (i, (OFF_GATE + b * d) // tn + j))
    return pl.pallas_call(
        _merge_kernel,
        out_shape=jax.ShapeDtypeStruct((r, d), BF16),
        grid=(r // tm, d // tn),
        in_specs=[row(DA_WIDTH), row(GQ_WIDTH), row(S5_WIDTH),
                  col(DA_WIDTH, 0), col(GQ_WIDTH, 0), col(S5_WIDTH, 0), col(S5_WIDTH, d),
                  gate(0), gate(1), gate(2)],
        out_specs=pl.BlockSpec((tm, tn), lambda i, j: (i, j)),
        scratch_shapes=[pltpu.VMEM((tm, S5_WIDTH), BF16)],
        compiler_params=_cparams("parallel", "arbitrary"),
        name="branch_merge",
    )(o_a, o_b, y_s5, w_pa, w_pb, w_glu, w_glu, proj, proj, proj)


def _out_ln_kernel(m_ref, w_ref, x_ref, gm_ref, lg_ref, lb_ref, o_ref, *, tm, n_lat, alpha):
    y = jnp.dot(m_ref[...], w_ref[...], preferred_element_type=F32)
    gate = _row_select(gm_ref, pl.program_id(0) * tm, tm, n_lat)
    o_ref[...] = _layer_norm(alpha * x_ref[...] + gate * y, lg_ref[...], lb_ref[...])


def _out_ln(merged, w_o, xs, mod, ln_g, ln_b, n_lat, alpha):
    r, d = xs.shape
    tm = _tile(r, 528, 16)
    return pl.pallas_call(
        functools.partial(_out_ln_kernel, tm=tm, n_lat=n_lat, alpha=alpha),
        out_shape=jax.ShapeDtypeStruct((r, d), F32),
        grid=(r // tm,),
        in_specs=[
            pl.BlockSpec((tm, d), lambda i: (i, 0)),
            pl.BlockSpec((d, d), lambda i: (0, 0)),
            pl.BlockSpec((tm, d), lambda i: (i, 0)),
            pl.BlockSpec((8, d), lambda i: (0, 2)),
            _const_spec((1, d)), _const_spec((1, d)),
        ],
        out_specs=pl.BlockSpec((tm, d), lambda i: (i, 0)),
        compiler_params=_cparams("parallel"),
        name="out_proj_ln",
    )(merged, w_o, xs, mod, ln_g.reshape(1, d), ln_b.reshape(1, d))


def _ffn_kernel(x_ref, sh_ref, sc_ref, gf_ref, wg_ref, wu_ref, wd_ref, lg_ref, lb_ref, o_ref,
                h_scr, acc_scr, *, tm, n_lat, alpha):
    i = pl.program_id(0)
    f = pl.program_id(1)

    @pl.when(f == 0)
    def _():
        sh = _row_select(sh_ref, i * tm, tm, n_lat)
        sc = _row_select(sc_ref, i * tm, tm, n_lat)
        h_scr[...] = (x_ref[...] * (1.0 + sc) + sh).astype(BF16)
        acc_scr[...] = jnp.zeros_like(acc_scr)

    h = h_scr[...]
    gate = jnp.dot(h, wg_ref[...], preferred_element_type=F32)
    up = jnp.dot(h, wu_ref[...], preferred_element_type=F32)
    act = (gate * _sigmoid(gate) * up).astype(BF16)
    acc_scr[...] += jnp.dot(act, wd_ref[...], preferred_element_type=F32)

    @pl.when(f == pl.num_programs(1) - 1)
    def _():
        g = _row_select(gf_ref, i * tm, tm, n_lat)
        o_ref[...] = _layer_norm(alpha * x_ref[...] + g * acc_scr[...], lg_ref[...], lb_ref[...])


def _ffn(xs, mod, w_gate, w_up, w_down, ln_g, ln_b, n_lat, alpha):
    r, d = xs.shape
    f = w_gate.shape[1]
    tm = _tile(r, 528, 16)
    tf = _tile(f, 512, LANES)
    modv = lambda c: pl.BlockSpec((8, d), lambda i, j: (0, c))
    return pl.pallas_call(
        functools.partial(_ffn_kernel, tm=tm, n_lat=n_lat, alpha=alpha),
        out_shape=jax.ShapeDtypeStruct((r, d), F32),
        grid=(r // tm, f // tf),
        in_specs=[
            pl.BlockSpec((tm, d), lambda i, j: (i, 0)),
            modv(3), modv(4), modv(5),
            pl.BlockSpec((d, tf), lambda i, j: (0, j)),
            pl.BlockSpec((d, tf), lambda i, j: (0, j)),
            pl.BlockSpec((tf, d), lambda i, j: (j, 0)),
            _const_spec((1, d)), _const_spec((1, d)),
        ],
        out_specs=pl.BlockSpec((tm, d), lambda i, j: (i, 0)),
        scratch_shapes=[pltpu.VMEM((tm, d), BF16), pltpu.VMEM((tm, d), F32)],
        compiler_params=_cparams("parallel", "arbitrary"),
        name="swiglu_ln",
    )(xs, mod, mod, mod, w_gate, w_up, w_down, ln_g.reshape(1, d), ln_b.reshape(1, d))


def kernel(x, c, ctx, c_ctx, w_ada, b_ada, w_in, da_lam_q1, da_lam_k1, da_lam_q2, da_lam_k2, da_subln_w, w_pa, gq_qnorm_w, gq_knorm_w, w_pb, s5_a_re, s5_a_im, s5_log_dt, s5_b_re, s5_b_im, s5_c_re, s5_c_im, s5_d, w_glu, w_o, ln_mix_g, ln_mix_b, w_ffn_gate, w_ffn_up, w_ffn_down, ln_ffn_g, ln_ffn_b):
    batch, n_lat, d = x.shape
    n_ctx = ctx.shape[1]
    depth = w_in.shape[0]
    r = n_lat + n_ctx
    assert batch == 1 and n_lat % GRID_W == 0 and n_ctx % S5_CHUNK == 0 and n_lat % S5_CHUNK == 0
    assert n_lat % n_ctx == 0
    alpha = (2 * depth) ** 0.25
    tk = _tile(r, 768, MXU_DIM)
    assert n_ctx <= tk and (n_lat - (r // tk - 1) * tk) % LANES == 0

    xs = jnp.concatenate([x[0], ctx[0]], axis=0).astype(F32)
    c_rows = jnp.zeros((8, d), F32).at[0].set(c[0]).at[1].set(c_ctx)
    mods = _ada_mod(c_rows, w_ada, b_ada)
    tab_da = _rope_tables(n_lat, n_ctx, DA_HEAD_DIM)
    tab_gq = _rope_tables(n_lat, n_ctx, GQ_HEAD_DIM)

    for l in range(depth):
        lam_init = 0.8 - 0.6 * math.exp(-0.3 * l)
        mod = mods[l]
        proj = _in_proj(xs, mod, w_in[l].astype(BF16), n_lat)
        qa, kat, qb, kbt = _prep_qk(proj, tab_da, tab_gq, gq_qnorm_w[l], gq_knorm_w[l], tk)
        lam_params = jnp.stack([da_lam_q1[l], da_lam_k1[l], da_lam_q2[l], da_lam_k2[l]]).astype(F32)
        o_a = _diff_attention(lam_params, da_subln_w[l], qa, kat, proj, n_lat, lam_init)
        o_b = _gq_attention(qb, kbt, proj, n_lat)
        mcat, ncat, coefs = _s5_matrices(s5_a_re[l], s5_a_im[l], s5_log_dt[l], s5_b_re[l], s5_b_im[l],
                                         s5_c_re[l], s5_c_im[l], s5_d[l])
        y_s5 = _s5_mixer(proj[:, OFF_U:OFF_U + S5_WIDTH], mcat, ncat, coefs, n_lat)
        merged = _merge(o_a, o_b, y_s5, proj, w_pa[l].astype(BF16), w_pb[l].astype(BF16),
                        w_glu[l].astype(BF16), d)
        xs = _out_ln(merged, w_o[l].astype(BF16), xs, mod, ln_mix_g[l], ln_mix_b[l], n_lat, alpha)
        xs = _ffn(xs, mod, w_ffn_gate[l].astype(BF16), w_ffn_up[l].astype(BF16), w_ffn_down[l].astype(BF16),
                  ln_ffn_g[l], ln_ffn_b[l], n_lat, alpha)
    return xs[:n_lat][None].astype(x.dtype)
```

```python
import functools
import math

import jax
import jax.numpy as jnp
import numpy as np
from jax import lax
from jax.experimental import pallas as pl
from jax.experimental.pallas import tpu as pltpu

F32 = jnp.float32
BF16 = jnp.bfloat16

GRID_W = 64
DA_HEADS = 8
DA_HEAD_DIM = 64
DA_WIDTH = DA_HEADS * 2 * DA_HEAD_DIM
GQ_HEADS = 8
GQ_KV_HEADS = 2
GQ_GROUP = GQ_HEADS // GQ_KV_HEADS
GQ_HEAD_DIM = 128
GQ_WIDTH = GQ_HEADS * GQ_HEAD_DIM
GQ_KV_WIDTH = GQ_KV_HEADS * GQ_HEAD_DIM
S5_WIDTH = 1024
S5_GROUP = 16
S5_GROUPS = S5_WIDTH // S5_GROUP
S5_STATE = 64
S5_MAX_REAL = -1e-4
N_BRANCH = 3
ROPE_THETA = 10000.0
LN_EPS = 1e-6
RMS_EPS = 1e-6
DA_SUBLN_EPS = 1e-5

OFF_QA = 0
OFF_KA = OFF_QA + DA_WIDTH
OFF_VA = OFF_KA + DA_WIDTH
OFF_QB = OFF_VA + DA_WIDTH
OFF_KB = OFF_QB + GQ_WIDTH
OFF_VB = OFF_KB + GQ_KV_WIDTH
OFF_U = OFF_VB + GQ_KV_WIDTH
OFF_GATE = OFF_U + S5_WIDTH

LANES = 128
MXU_DIM = 256
V7X_VMEM_BYTES = 64 * 1024 * 1024
VMEM_LIMIT = V7X_VMEM_BYTES - 8 * 1024 * 1024

S5_CHUNK = 16
S5_CW = S5_CHUNK * S5_GROUP
S5_BLOCK_GROUPS = LANES // S5_GROUP
S5_SCAN_UNROLL = 8

LOG2E = math.log2(math.e)
NEG_BIG = -1e30


def _cparams(*sem):
    return pltpu.CompilerParams(dimension_semantics=sem, vmem_limit_bytes=VMEM_LIMIT)


def _tile(n, target, mult):
    best = None
    for t in range(mult, min(n, target) + 1, mult):
        if n % t == 0:
            best = t
    assert best is not None, (n, target, mult)
    return best


def _const_spec(block_shape):
    return pl.BlockSpec(block_shape, lambda *_: (0,) * len(block_shape))


def _sigmoid(x):
    return 1.0 / (1.0 + jnp.exp(-x))


def _gelu_tanh(x):
    return 0.5 * x * (1.0 + jnp.tanh(math.sqrt(2.0 / math.pi) * (x + 0.044715 * (x * x * x))))


def _row_select(vec_ref, row0, tm, n_lat):
    rows = row0 + lax.broadcasted_iota(jnp.int32, (tm, 1), 0)
    return jnp.where(rows < n_lat, vec_ref[0:1, :], vec_ref[1:2, :])


def _layer_norm(z, g, b):
    mu = jnp.mean(z, axis=-1, keepdims=True)
    zc = z - mu
    var = jnp.mean(zc * zc, axis=-1, keepdims=True)
    return zc * lax.rsqrt(var + LN_EPS) * g + b


def _mod_kernel(c_ref, w_ref, b_ref, o_ref):
    a = c_ref[...]
    a = a * _sigmoid(a)
    o_ref[0] = jnp.dot(a.astype(BF16), w_ref[0].astype(BF16), preferred_element_type=F32) + b_ref[0]


def _ada_mod(c_rows, w_ada, b_ada):
    depth, d, n6 = w_ada.shape
    tn = _tile(n6, 1024, LANES)
    return pl.pallas_call(
        _mod_kernel,
        out_shape=jax.ShapeDtypeStruct((depth, 8, n6), F32),
        grid=(depth, n6 // tn),
        in_specs=[
            pl.BlockSpec((8, d), lambda l, j: (0, 0)),
            pl.BlockSpec((1, d, tn), lambda l, j: (l, 0, j)),
            pl.BlockSpec((1, 1, tn), lambda l, j: (l, 0, j)),
        ],
        out_specs=pl.BlockSpec((1, 8, tn), lambda l, j: (l, 0, j)),
        compiler_params=_cparams("parallel", "parallel"),
        name="ada_mod",
    )(c_rows, w_ada, b_ada.reshape(depth, 1, n6))


def _in_proj_kernel(x_ref, sh_ref, sc_ref, w_ref, o_ref, h_scr, *, tm, n_lat):
    i = pl.program_id(0)

    @pl.when(pl.program_id(1) == 0)
    def _():
        sh = _row_select(sh_ref, i * tm, tm, n_lat)
        sc = _row_select(sc_ref, i * tm, tm, n_lat)
        h_scr[...] = (x_ref[...] * (1.0 + sc) + sh).astype(BF16)

    o_ref[...] = jnp.dot(h_scr[...], w_ref[...], preferred_element_type=F32).astype(o_ref.dtype)


def _in_proj(xs, mod, w_in, layer, n_lat):
    r, d = xs.shape
    n = w_in.shape[2]
    tm = _tile(r, 1056, 16)
    tn = _tile(n, 512, LANES)
    return pl.pallas_call(
        functools.partial(_in_proj_kernel, tm=tm, n_lat=n_lat),
        out_shape=jax.ShapeDtypeStruct((r, n), BF16),
        grid=(r // tm, n // tn),
        in_specs=[
            pl.BlockSpec((tm, d), lambda i, j: (i, 0)),
            pl.BlockSpec((8, d), lambda i, j: (0, 0)),
            pl.BlockSpec((8, d), lambda i, j: (0, 1)),
            pl.BlockSpec((None, d, tn), lambda i, j: (layer, 0, j)),
        ],
        out_specs=pl.BlockSpec((tm, tn), lambda i, j: (i, j)),
        scratch_shapes=[pltpu.VMEM((tm, d), BF16)],
        compiler_params=_cparams("parallel", "arbitrary"),
        name="in_proj",
    )(xs, mod, mod, w_in)


def _rope(z, cos, sin_a, sin_b, quarter):
    return z * cos + pltpu.roll(z, LANES - quarter, 1) * sin_a + pltpu.roll(z, quarter, 1) * sin_b


def _rms(z, w, eps):
    return z * lax.rsqrt(jnp.mean(z * z, axis=-1, keepdims=True) + eps) * w


def _prep_kernel(qa_ref, ka_ref, qb_ref, kb_ref, tda_ref, tgq_ref, qn_ref, kn_ref,
                 qa_o, kat_o, qb_o, kbt_o):
    cda, sda_a, sda_b = tda_ref[0], tda_ref[1], tda_ref[2]
    cgq, sgq_a, sgq_b = tgq_ref[0], tgq_ref[1], tgq_ref[2]
    da_scale = DA_HEAD_DIM ** -0.5 * LOG2E
    gq_scale = GQ_HEAD_DIM ** -0.5 * LOG2E
    for h in range(DA_HEADS):
        sl = slice(h * LANES, (h + 1) * LANES)
        z = qa_ref[:, sl].astype(F32)
        qa_o[:, sl] = (_rope(z, cda, sda_a, sda_b, DA_HEAD_DIM // 4) * da_scale).astype(BF16)
        z = ka_ref[:, sl].astype(F32)
        kat_o[h, 0] = _rope(z, cda, sda_a, sda_b, DA_HEAD_DIM // 4).T.astype(BF16)
    for h in range(GQ_HEADS):
        sl = slice(h * LANES, (h + 1) * LANES)
        z = _rms(qb_ref[:, sl].astype(F32), qn_ref[...], RMS_EPS)
        qb_o[:, sl] = (_rope(z, cgq, sgq_a, sgq_b, GQ_HEAD_DIM // 4) * gq_scale).astype(BF16)
    for h in range(GQ_KV_HEADS):
        sl = slice(h * LANES, (h + 1) * LANES)
        z = _rms(kb_ref[:, sl].astype(F32), kn_ref[...], RMS_EPS)
        kbt_o[h, 0] = _rope(z, cgq, sgq_a, sgq_b, GQ_HEAD_DIM // 4).T.astype(BF16)


def _prep_qk(proj, tab_da, tab_gq, qn_w, kn_w, tk):
    r = proj.shape[0]
    nk = r // tk
    blk = lambda w, off: pl.BlockSpec((tk, w), lambda i: (i, off // w))
    out = lambda w: pl.BlockSpec((tk, w), lambda i: (i, 0))
    outt = lambda heads: pl.BlockSpec((heads, 1, LANES, tk), lambda i: (0, i, 0, 0))
    return pl.pallas_call(
        _prep_kernel,
        out_shape=(
            jax.ShapeDtypeStruct((r, DA_WIDTH), BF16),
            jax.ShapeDtypeStruct((DA_HEADS, nk, LANES, tk), BF16),
            jax.ShapeDtypeStruct((r, GQ_WIDTH), BF16),
            jax.ShapeDtypeStruct((GQ_KV_HEADS, nk, LANES, tk), BF16),
        ),
        grid=(nk,),
        in_specs=[
            blk(DA_WIDTH, OFF_QA), blk(DA_WIDTH, OFF_KA), blk(GQ_WIDTH, OFF_QB), blk(GQ_KV_WIDTH, OFF_KB),
            pl.BlockSpec((3, tk, LANES), lambda i: (0, i, 0)),
            pl.BlockSpec((3, tk, LANES), lambda i: (0, i, 0)),
            _const_spec((1, LANES)),
            _const_spec((1, LANES)),
        ],
        out_specs=(out(DA_WIDTH), outt(DA_HEADS), out(GQ_WIDTH), outt(GQ_KV_HEADS)),
        compiler_params=_cparams("parallel"),
        name="prep_qk",
    )(proj, proj, proj, proj, tab_da, tab_gq, qn_w.reshape(1, LANES), kn_w.reshape(1, LANES))


def _rope_tables(n_lat, n_ctx, dim):
    rows = n_lat // GRID_W
    row = jnp.broadcast_to(jnp.arange(rows, dtype=F32)[:, None], (rows, GRID_W)).reshape(-1)
    col = jnp.broadcast_to(jnp.arange(GRID_W, dtype=F32)[None, :], (rows, GRID_W)).reshape(-1)
    axis_dim = dim // 2
    inv_freq = ROPE_THETA ** (-jnp.arange(0, axis_dim, 2, dtype=F32) / axis_dim)
    ang_r = row[:, None] * inv_freq[None, :]
    ang_c = col[:, None] * inv_freq[None, :]
    ang = jnp.concatenate([ang_r, ang_r, ang_c, ang_c], axis=-1)
    cos, sin = jnp.cos(ang), jnp.sin(ang)
    first_half = (jnp.arange(dim) % (dim // 2)) < (dim // 4)
    sin_a = jnp.where(first_half[None, :], -sin, 0.0)
    sin_b = jnp.where(first_half[None, :], 0.0, sin)
    tab = jnp.stack([cos, sin_a, sin_b])
    ident = jnp.stack([jnp.ones((n_ctx, dim), F32), jnp.zeros((n_ctx, dim), F32), jnp.zeros((n_ctx, dim), F32)])
    tab = jnp.concatenate([tab, ident], axis=1)
    return jnp.tile(tab, (1, 1, LANES // dim))


def _flash_step(s_buf, vx, m_scr, a_scr):
    s = s_buf[...]
    m_old = m_scr[...]
    m = jnp.maximum(m_old, jnp.max(s, axis=-1, keepdims=True))
    p = jnp.exp2((s - m).astype(BF16))
    m_scr[...] = m
    a_scr[...] = jnp.exp2(m_old - m) * a_scr[...] + jnp.dot(p, vx, preferred_element_type=F32)


def _flash_loop(qs, kt_ref, v_ref, vx_scr, bufs, stats, tk, n_chunks):
    @pl.when(pl.program_id(1) == 0)
    def _():
        vx_scr[:, :LANES] = v_ref[...]
        vx_scr[:, LANES:] = jnp.ones((vx_scr.shape[0], LANES), vx_scr.dtype)

    for m_scr, a_scr in stats:
        m_scr[...] = jnp.full_like(m_scr, NEG_BIG)
        a_scr[...] = jnp.zeros_like(a_scr)

    def scores(c, slot):
        kt = kt_ref[c]
        for q, buf in zip(qs, bufs):
            buf[slot][...] = jnp.dot(q, kt, preferred_element_type=F32)

    def update(c, slot):
        vx = vx_scr[pl.ds(pl.multiple_of(c * tk, tk), tk), :]
        for buf, st in zip(bufs, stats):
            _flash_step(buf[slot], vx, *st)

    scores(0, 0)
    pairs = (n_chunks - 1) // 2

    def body(j, carry):
        c = 2 * j
        scores(c + 1, 1)
        update(c, 0)
        scores(c + 2, 0)
        update(c + 1, 1)
        return carry

    lax.fori_loop(0, pairs, body, 0)
    c = 2 * pairs
    if n_chunks % 2 == 1:
        update(c, 0)
    else:
        scores(c + 1, 1)
        update(c, 0)
        update(c + 1, 1)


def _softmax_once(s, v):
    p = jnp.exp2(s - jnp.max(s, axis=-1, keepdims=True))
    return jnp.dot(p.astype(BF16), v, preferred_element_type=F32) / jnp.sum(p, axis=-1, keepdims=True)


def _da_split(q):
    lane = lax.broadcasted_iota(jnp.int32, (1, LANES), 1)
    zero = jnp.zeros_like(q)
    return jnp.where(lane < DA_HEAD_DIM, q, zero), jnp.where(lane < DA_HEAD_DIM, zero, q)


def _da_finish(o1, o2, lam_ref, w_ref, o_ref, lam_init):
    lp = lam_ref[...]
    lam = (jnp.exp(jnp.sum(lp[0:1] * lp[1:2], axis=-1, keepdims=True))
           - jnp.exp(jnp.sum(lp[2:3] * lp[3:4], axis=-1, keepdims=True)) + lam_init)
    o = _rms(o1 - lam * o2, w_ref[...], DA_SUBLN_EPS) * (1.0 - lam_init)
    o_ref[...] = o.astype(o_ref.dtype)


def _normalised(a_scr):
    return a_scr[:, :LANES] / a_scr[:, LANES:]


def _da_kernel(lam_ref, w_ref, q_ref, kt_ref, v_ref, o_ref, vx, s1a, s1b, m1, a1, s2a, s2b, m2, a2,
               *, tk, n_chunks, lam_init):
    q1, q2 = _da_split(q_ref[...])
    _flash_loop((q1, q2), kt_ref, v_ref, vx, ((s1a, s1b), (s2a, s2b)), ((m1, a1), (m2, a2)), tk, n_chunks)
    _da_finish(_normalised(a1), _normalised(a2), lam_ref, w_ref, o_ref, lam_init)


def _da_ctx_kernel(lam_ref, w_ref, q_ref, kt_ref, v_ref, latent_out_ref, o_ref, *, off, n_ctx, lam_init):
    del latent_out_ref
    q1, q2 = _da_split(q_ref[...])
    kt = kt_ref[:, off:off + n_ctx]
    v = v_ref[...]
    o1 = _softmax_once(jnp.dot(q1, kt, preferred_element_type=F32), v)
    o2 = _softmax_once(jnp.dot(q2, kt, preferred_element_type=F32), v)
    _da_finish(o1, o2, lam_ref, w_ref, o_ref, lam_init)


def _gq_kernel(q_ref, kt_ref, v_ref, o_ref, vx, sa, sb, m1, a1, *, tk, n_chunks):
    _flash_loop((q_ref[...],), kt_ref, v_ref, vx, ((sa, sb),), ((m1, a1),), tk, n_chunks)
    o_ref[...] = _normalised(a1).astype(o_ref.dtype)


def _gq_ctx_kernel(q_ref, kt_ref, v_ref, latent_out_ref, o_ref, *, off, n_ctx):
    del latent_out_ref
    kt = kt_ref[:, off:off + n_ctx]
    o_ref[...] = _softmax_once(jnp.dot(q_ref[...], kt, preferred_element_type=F32), v_ref[...]).astype(o_ref.dtype)


def _attention(params, q, kt, proj, v_off, kv_group, n_lat, main_kernel, ctx_kernel, n_maps, name):
    r, width = q.shape
    n_heads = width // LANES
    n_ctx = r - n_lat
    _, n_chunks, _, tk = kt.shape
    tq = _tile(n_lat, 512, 16)
    ctx_blk = n_lat // n_ctx
    off = n_lat - (n_chunks - 1) * tk
    param_specs = [_const_spec(p.shape) for p in params]
    stat = pltpu.VMEM((tq, 1), F32)
    scores = pltpu.VMEM((tq, tk), F32)
    scratch = [pltpu.VMEM((r, 2 * LANES), BF16)] + n_maps * [scores, scores, stat, pltpu.VMEM((tq, 2 * LANES), F32)]
    out = pl.pallas_call(
        functools.partial(main_kernel, tk=tk, n_chunks=n_chunks),
        out_shape=jax.ShapeDtypeStruct((r, width), BF16),
        grid=(n_heads, n_lat // tq),
        in_specs=param_specs + [
            pl.BlockSpec((tq, LANES), lambda h, i: (i, h)),
            pl.BlockSpec((None, n_chunks, LANES, tk), lambda h, i: (h // kv_group, 0, 0, 0)),
            pl.BlockSpec((r, LANES), lambda h, i: (0, v_off // LANES + h // kv_group)),
        ],
        out_specs=pl.BlockSpec((tq, LANES), lambda h, i: (i, h)),
        scratch_shapes=scratch,
        compiler_params=_cparams("parallel", "arbitrary"),
        name=name,
    )(*params, q, kt, proj)
    return pl.pallas_call(
        functools.partial(ctx_kernel, off=off, n_ctx=n_ctx),
        out_shape=jax.ShapeDtypeStruct((r, width), BF16),
        grid=(n_heads,),
        in_specs=param_specs + [
            pl.BlockSpec((n_ctx, LANES), lambda h: (ctx_blk, h)),
            pl.BlockSpec((None, None, LANES, tk), lambda h: (h // kv_group, n_chunks - 1, 0, 0)),
            pl.BlockSpec((n_ctx, LANES), lambda h: (ctx_blk, v_off // LANES + h // kv_group)),
            pl.BlockSpec(memory_space=pl.ANY),
        ],
        out_specs=pl.BlockSpec((n_ctx, LANES), lambda h: (ctx_blk, h)),
        input_output_aliases={len(params) + 3: 0},
        compiler_params=_cparams("parallel"),
        name=name + "_ctx",
    )(*params, q, kt, proj, out)


def _diff_attention(lam_params, subln_w, qa, kat, proj, n_lat, lam_init):
    return _attention((lam_params, subln_w.reshape(1, LANES)), qa, kat, proj, OFF_VA, 1, n_lat,
                      functools.partial(_da_kernel, lam_init=lam_init),
                      functools.partial(_da_ctx_kernel, lam_init=lam_init), 2, "diff_attention")


def _gq_attention(qb, kbt, proj, n_lat):
    return _attention((), qb, kbt, proj, OFF_VB, GQ_GROUP, n_lat, _gq_kernel, _gq_ctx_kernel, 1, "gq_attention")


def _s5_matrices(a_re, a_im, log_dt, b_re, b_im, c_re, c_im, d_skip):
    t = S5_CHUNK
    g, p, h = S5_GROUPS, S5_STATE, S5_GROUP
    a_re = jnp.minimum(a_re.astype(F32), S5_MAX_REAL)
    a_im = a_im.astype(F32)
    dt = jnp.exp(log_dt.astype(F32))[..., None]
    mag = jnp.exp(a_re * dt)
    lr = mag * jnp.cos(a_im * dt)
    li = mag * jnp.sin(a_im * dt)
    den = jnp.square(a_re) + jnp.square(a_im)
    nr = lr - 1.0
    cr = (nr * a_re + li * a_im) / den
    ci = (li * a_re - nr * a_im) / den
    bbr = cr[..., None] * b_re - ci[..., None] * b_im
    bbi = cr[..., None] * b_im + ci[..., None] * b_re
    j = jnp.arange(t + 1, dtype=F32)[:, None, None, None]
    pmag = jnp.exp(j * (a_re * dt)[None])
    pr = pmag * jnp.cos(j * (a_im * dt)[None])
    pi = pmag * jnp.sin(j * (a_im * dt)[None])
    clr = c_re[None] * pr[:, :, :, None, :] - c_im[None] * pi[:, :, :, None, :]
    cli = c_re[None] * pi[:, :, :, None, :] + c_im[None] * pr[:, :, :, None, :]
    kern = (jnp.einsum('jdghp,dgpk->jdghk', clr[:t], bbr) - jnp.einsum('jdghp,dgpk->jdghk', cli[:t], bbi))
    eye = jnp.eye(h, dtype=F32)[None] * d_skip.astype(F32).reshape(g, h)[:, :, None]
    k0 = kern[0, 0] + kern[0, 1] + eye
    kcat = jnp.concatenate([kern[1:, 1][::-1], k0[None], kern[1:, 0]], axis=0)
    idx = (jnp.arange(t)[None, :] - jnp.arange(t)[:, None]) + (t - 1)
    m = kcat[idx]
    m = m.transpose(2, 0, 4, 1, 3).reshape(g, t * h, t * h)

    def to_state(d, pw):
        wr = pw[0][:, :, :, None] * bbr[d][None] - pw[1][:, :, :, None] * bbi[d][None]
        wi = pw[0][:, :, :, None] * bbi[d][None] + pw[1][:, :, :, None] * bbr[d][None]
        f = lambda w: w.transpose(1, 0, 3, 2).reshape(g, t * h, p)
        return f(wr), f(wi)

    wf = to_state(0, (pr[:t, 0][::-1], pi[:t, 0][::-1]))
    wb = to_state(1, (pr[:t, 1], pi[:t, 1]))
    mcat = jnp.concatenate([m, wf[0], wf[1], wb[0], wb[1]], axis=-1)

    def from_state(d, cl_r, cl_i):
        f = lambda c: c.transpose(1, 3, 0, 2).reshape(g, p, t * h)
        return f(cl_r), -f(cl_i)

    nf = from_state(0, clr[1:, 0], cli[1:, 0])
    nb = from_state(1, clr[1:, 1][::-1], cli[1:, 1][::-1])
    ncat = jnp.concatenate([nf[0], nf[1], nb[0], nb[1]], axis=1)

    def coef(d):
        r_, i_ = pr[t, d], pi[t, d]
        return [jnp.concatenate([r_, r_], -1), jnp.concatenate([-i_, i_], -1), jnp.concatenate([i_, -i_], -1)]

    coefs = jnp.stack(coef(0) + coef(1))
    return mcat.astype(BF16), ncat.astype(BF16), coefs


def _s5_lane_perm():
    i = np.arange(S5_BLOCK_GROUPS * S5_CW)
    tok, grp, ch = i // LANES, (i // S5_GROUP) % S5_BLOCK_GROUPS, i % S5_GROUP
    perm = np.zeros((i.size, i.size), np.float32)
    perm[i, grp * S5_CW + tok * S5_GROUP + ch] = 1.0
    return perm


def _s5_a_kernel(*refs):
    u_refs = refs[:S5_CHUNK]
    p_ref, m_ref, y_ref, s_ref = refs[S5_CHUNK:]
    u_nat = jnp.concatenate([u[...] for u in u_refs], axis=1)
    ug = jnp.dot(u_nat, p_ref[...], preferred_element_type=F32).astype(BF16)
    for gl in range(S5_BLOCK_GROUPS):
        r = jnp.dot(ug[:, gl * S5_CW:(gl + 1) * S5_CW], m_ref[gl], preferred_element_type=F32)
        y_ref[gl] = r[:, :S5_CW]
        s_ref[gl] = r[:, S5_CW:]


def _s5_b_kernel(s_ref, coef_ref, x_ref, sw_scr, *, n_chunks, n_lat_chunks):
    half = 2 * S5_STATE
    n_ctx_chunks = n_chunks - n_lat_chunks
    fwd_sl, bwd_sl = slice(0, half), slice(half, 2 * half)

    def swap(c, carry):
        sw_scr[c, :, fwd_sl] = pltpu.roll(s_ref[c, :, fwd_sl], S5_STATE, 1)
        sw_scr[c, :, bwd_sl] = pltpu.roll(s_ref[c, :, bwd_sl], S5_STATE, 1)
        return carry

    lax.fori_loop(0, n_chunks, swap, 0, unroll=S5_SCAN_UNROLL)

    def step(sl, a, b, bt):
        def f(c, carry):
            x, xt = carry
            x_ref[c, :, sl] = x.astype(x_ref.dtype)
            return a * x + b * xt + s_ref[c, :, sl], a * xt + bt * x + sw_scr[c, :, sl]
        return f

    fwd = step(fwd_sl, coef_ref[0], coef_ref[1], coef_ref[2])
    bwd = step(bwd_sl, coef_ref[3], coef_ref[4], coef_ref[5])
    zero = jnp.zeros_like(coef_ref[0])
    loop = functools.partial(lax.fori_loop, unroll=S5_SCAN_UNROLL)
    carry = loop(0, n_ctx_chunks, lambda i, cr: fwd(n_lat_chunks + i, cr), (zero, zero))
    loop(0, n_lat_chunks, fwd, carry)
    carry = loop(0, n_ctx_chunks, lambda i, cr: bwd(n_chunks - 1 - i, cr), (zero, zero))
    loop(0, n_lat_chunks, lambda i, cr: bwd(n_lat_chunks - 1 - i, cr), carry)


def _s5_c_kernel(x_ref, n_ref, yi_ref, q_ref, y_ref):
    ys = [(yi_ref[gl] + jnp.dot(x_ref[gl], n_ref[gl], preferred_element_type=F32)).astype(BF16)
          for gl in range(S5_BLOCK_GROUPS)]
    y_ref[...] = jnp.dot(jnp.concatenate(ys, axis=1), q_ref[...], preferred_element_type=F32).astype(y_ref.dtype)


def _s5_mixer(u, mcat, ncat, coefs, n_lat):
    r = u.shape[0]
    g, t, gb = S5_GROUPS, S5_CHUNK, S5_BLOCK_GROUPS
    nc = r // t
    sw = 4 * S5_STATE
    nblk = g // gb
    perm = _s5_lane_perm()
    p_in = jnp.asarray(perm, BF16)
    p_out = jnp.asarray(perm.T, BF16)
    u_rows = u.reshape(nc, t * S5_WIDTH)
    grp = lambda a, b: pl.BlockSpec((gb, a, b), lambda i: (i, 0, 0))
    perm_spec = pl.BlockSpec(perm.shape, lambda i: (0, 0), pipeline_mode=pl.Buffered(1))
    tok_specs = [pl.BlockSpec((nc, LANES), lambda i, s=s: (0, s * nblk + i)) for s in range(t)]
    y_intra, s = pl.pallas_call(
        _s5_a_kernel,
        out_shape=(jax.ShapeDtypeStruct((g, nc, S5_CW), F32), jax.ShapeDtypeStruct((g, nc, sw), F32)),
        grid=(nblk,),
        in_specs=tok_specs + [perm_spec, grp(S5_CW, S5_CW + sw)],
        out_specs=(grp(nc, S5_CW), grp(nc, sw)),
        compiler_params=_cparams("parallel"),
        name="s5_chunk_local",
    )(*([u_rows] * t), p_in, mcat)

    gs = 16
    st = s.transpose(1, 0, 2)
    xt = pl.pallas_call(
        functools.partial(_s5_b_kernel, n_chunks=nc, n_lat_chunks=n_lat // t),
        out_shape=jax.ShapeDtypeStruct((nc, g, sw), BF16),
        grid=(g // gs,),
        in_specs=[pl.BlockSpec((nc, gs, sw), lambda i: (0, i, 0)),
                  pl.BlockSpec((6, gs, 2 * S5_STATE), lambda i: (0, i, 0))],
        out_specs=pl.BlockSpec((nc, gs, sw), lambda i: (0, i, 0)),
        scratch_shapes=[pltpu.VMEM((nc, gs, sw), F32)],
        compiler_params=_cparams("parallel"),
        name="s5_chunk_scan",
    )(st, coefs)

    x_in = xt.transpose(1, 0, 2)
    y_blocks = pl.pallas_call(
        _s5_c_kernel,
        out_shape=jax.ShapeDtypeStruct((nblk, nc, gb * S5_CW), BF16),
        grid=(nblk,),
        in_specs=[grp(nc, sw), grp(sw, S5_CW), grp(nc, S5_CW), perm_spec],
        out_specs=pl.BlockSpec((None, nc, gb * S5_CW), lambda i: (i, 0, 0)),
        compiler_params=_cparams("parallel"),
        name="s5_state_out",
    )(x_in, ncat, y_intra, p_out)
    return y_blocks.reshape(nblk, nc, t, LANES).transpose(1, 2, 0, 3).reshape(r, S5_WIDTH)


def _merge_kernel(oa_ref, ob_ref, y_ref, wpa_ref, wpb_ref, wga_ref, wgg_ref, g0_ref, g1_ref, g2_ref,
                  o_ref, gy_scr):
    @pl.when(pl.program_id(1) == 0)
    def _():
        gy_scr[...] = _gelu_tanh(y_ref[...].astype(F32)).astype(BF16)

    dot = lambda a, b: jnp.dot(a, b, preferred_element_type=F32)
    pa = dot(oa_ref[...], wpa_ref[...])
    pb = dot(ob_ref[...], wpb_ref[...])
    gy = gy_scr[...]
    pc = dot(gy, wga_ref[...]) * _sigmoid(dot(gy, wgg_ref[...]))
    sg = lambda ref: _sigmoid(ref[...].astype(F32))
    o_ref[...] = (sg(g0_ref) * pa + sg(g1_ref) * pb + sg(g2_ref) * pc).astype(o_ref.dtype)


def _merge(o_a, o_b, y_s5, proj, w_pa, w_pb, w_glu, layer, d):
    r = o_a.shape[0]
    tm = _tile(r, 1056, 16)
    tn = _tile(math.gcd(d, OFF_GATE), 512, LANES)
    row = lambda w: pl.BlockSpec((tm, w), lambda i, j: (i, 0))
    col = lambda k, off: pl.BlockSpec((None, k, tn), lambda i, j: (layer, 0, off // tn + j))
    gate = lambda b: pl.BlockSpec((tm, tn), lambda i, j: (i, (OFF_GATE + b * d) // tn + j))
    return pl.pallas_call(
        _merge_kernel,
        out_shape=jax.ShapeDtypeStruct((r, d), BF16),
        grid=(r // tm, d // tn),
        in_specs=[row(DA_WIDTH), row(GQ_WIDTH), row(S5_WIDTH),
                  col(DA_WIDTH, 0), col(GQ_WIDTH, 0), col(S5_WIDTH, 0), col(S5_WIDTH, d),
                  gate(0), gate(1), gate(2)],
        out_specs=pl.BlockSpec((tm, tn), lambda i, j: (i, j)),
        scratch_shapes=[pltpu.VMEM((tm, S5_WIDTH), BF16)],
        compiler_params=_cparams("parallel", "arbitrary"),
        name="branch_merge",
    )(o_a, o_b, y_s5, w_pa, w_pb, w_glu, w_glu, proj, proj, proj)


def _out_ln_kernel(m_ref, w_ref, x_ref, gm_ref, lg_ref, lb_ref, o_ref, *, tm, n_lat, alpha):
    y = jnp.dot(m_ref[...], w_ref[...], preferred_element_type=F32)
    gate = _row_select(gm_ref, pl.program_id(0) * tm, tm, n_lat)
    o_ref[...] = _layer_norm(alpha * x_ref[...] + gate * y, lg_ref[...], lb_ref[...])


def _out_ln(merged, w_o, layer, xs, mod, ln_g, ln_b, n_lat, alpha):
    r, d = xs.shape
    tm = _tile(r, 528, 16)
    return pl.pallas_call(
        functools.partial(_out_ln_kernel, tm=tm, n_lat=n_lat, alpha=alpha),
        out_shape=jax.ShapeDtypeStruct((r, d), F32),
        grid=(r // tm,),
        in_specs=[
            pl.BlockSpec((tm, d), lambda i: (i, 0)),
            pl.BlockSpec((None, d, d), lambda i: (layer, 0, 0)),
            pl.BlockSpec((tm, d), lambda i: (i, 0)),
            pl.BlockSpec((8, d), lambda i: (0, 2)),
            _const_spec((1, d)), _const_spec((1, d)),
        ],
        out_specs=pl.BlockSpec((tm, d), lambda i: (i, 0)),
        compiler_params=_cparams("parallel"),
        name="out_proj_ln",
    )(merged, w_o, xs, mod, ln_g.reshape(1, d), ln_b.reshape(1, d))


def _ffn_kernel(x_ref, sh_ref, sc_ref, gf_ref, wg_ref, wu_ref, wd_ref, lg_ref, lb_ref, o_ref,
                h_scr, acc_scr, *, tm, n_lat, alpha):
    i = pl.program_id(0)
    f = pl.program_id(1)

    @pl.when(f == 0)
    def _():
        sh = _row_select(sh_ref, i * tm, tm, n_lat)
        sc = _row_select(sc_ref, i * tm, tm, n_lat)
        h_scr[...] = (x_ref[...] * (1.0 + sc) + sh).astype(BF16)
        acc_scr[...] = jnp.zeros_like(acc_scr)

    h = h_scr[...]
    gate = jnp.dot(h, wg_ref[...], preferred_element_type=F32)
    up = jnp.dot(h, wu_ref[...], preferred_element_type=F32)
    act = (gate * _sigmoid(gate) * up).astype(BF16)
    acc_scr[...] += jnp.dot(act, wd_ref[...], preferred_element_type=F32)

    @pl.when(f == pl.num_programs(1) - 1)
    def _():
        g = _row_select(gf_ref, i * tm, tm, n_lat)
        o_ref[...] = _layer_norm(alpha * x_ref[...] + g * acc_scr[...], lg_ref[...], lb_ref[...])


def _ffn(xs, mod, w_gate, w_up, w_down, layer, ln_g, ln_b, n_lat, alpha):
    r, d = xs.shape
    f = w_gate.shape[2]
    tm = _tile(r, 528, 16)
    tf = _tile(f, 512, LANES)
    modv = lambda c: pl.BlockSpec((8, d), lambda i, j: (0, c))
    return pl.pallas_call(
        functools.partial(_ffn_kernel, tm=tm, n_lat=n_lat, alpha=alpha),
        out_shape=jax.ShapeDtypeStruct((r, d), F32),
        grid=(r // tm, f // tf),
        in_specs=[
            pl.BlockSpec((tm, d), lambda i, j: (i, 0)),
            modv(3), modv(4), modv(5),
            pl.BlockSpec((None, d, tf), lambda i, j: (layer, 0, j)),
            pl.BlockSpec((None, d, tf), lambda i, j: (layer, 0, j)),
            pl.BlockSpec((None, tf, d), lambda i, j: (layer, j, 0)),
            _const_spec((1, d)), _const_spec((1, d)),
        ],
        out_specs=pl.BlockSpec((tm, d), lambda i, j: (i, 0)),
        scratch_shapes=[pltpu.VMEM((tm, d), BF16), pltpu.VMEM((tm, d), F32)],
        compiler_params=_cparams("parallel", "arbitrary"),
        name="swiglu_ln",
    )(xs, mod, mod, mod, w_gate, w_up, w_down, ln_g.reshape(1, d), ln_b.reshape(1, d))


def kernel(x, c, ctx, c_ctx, w_ada, b_ada, w_in, da_lam_q1, da_lam_k1, da_lam_q2, da_lam_k2, da_subln_w, w_pa, gq_qnorm_w, gq_knorm_w, w_pb, s5_a_re, s5_a_im, s5_log_dt, s5_b_re, s5_b_im, s5_c_re, s5_c_im, s5_d, w_glu, w_o, ln_mix_g, ln_mix_b, w_ffn_gate, w_ffn_up, w_ffn_down, ln_ffn_g, ln_ffn_b):
    batch, n_lat, d = x.shape
    n_ctx = ctx.shape[1]
    depth = w_in.shape[0]
    r = n_lat + n_ctx
    assert batch == 1 and n_lat % GRID_W == 0 and n_ctx % S5_CHUNK == 0 and n_lat % S5_CHUNK == 0
    assert n_lat % n_ctx == 0
    alpha = (2 * depth) ** 0.25
    tk = _tile(r, 768, MXU_DIM)
    assert n_ctx <= tk and (n_lat - (r // tk - 1) * tk) % LANES == 0

    xs = jnp.concatenate([x[0], ctx[0]], axis=0).astype(F32)
    c_rows = jnp.zeros((8, d), F32).at[0].set(c[0]).at[1].set(c_ctx)
    mods = _ada_mod(c_rows, w_ada, b_ada)
    tab_da = _rope_tables(n_lat, n_ctx, DA_HEAD_DIM)
    tab_gq = _rope_tables(n_lat, n_ctx, GQ_HEAD_DIM)

    w_in, w_pa, w_pb, w_glu, w_o, w_ffn_gate, w_ffn_up, w_ffn_down = (
        w.astype(BF16) for w in (w_in, w_pa, w_pb, w_glu, w_o, w_ffn_gate, w_ffn_up, w_ffn_down))

    for l in range(depth):
        lam_init = 0.8 - 0.6 * math.exp(-0.3 * l)
        mod = mods[l]
        proj = _in_proj(xs, mod, w_in, l, n_lat)
        qa, kat, qb, kbt = _prep_qk(proj, tab_da, tab_gq, gq_qnorm_w[l], gq_knorm_w[l], tk)
        lam_params = jnp.stack([da_lam_q1[l], da_lam_k1[l], da_lam_q2[l], da_lam_k2[l]]).astype(F32)
        o_a = _diff_attention(lam_params, da_subln_w[l], qa, kat, proj, n_lat, lam_init)
        o_b = _gq_attention(qb, kbt, proj, n_lat)
        mcat, ncat, coefs = _s5_matrices(s5_a_re[l], s5_a_im[l], s5_log_dt[l], s5_b_re[l], s5_b_im[l],
                                         s5_c_re[l], s5_c_im[l], s5_d[l])
        y_s5 = _s5_mixer(proj[:, OFF_U:OFF_U + S5_WIDTH], mcat, ncat, coefs, n_lat)
        merged = _merge(o_a, o_b, y_s5, proj, w_pa, w_pb, w_glu, l, d)
        xs = _out_ln(merged, w_o, l, xs, mod, ln_mix_g[l], ln_mix_b[l], n_lat, alpha)
        xs = _ffn(xs, mod, w_ffn_gate, w_ffn_up, w_ffn_down, l, ln_ffn_g[l], ln_ffn_b[l], n_lat, alpha)
    return xs[:n_lat][None].astype(x.dtype)
```

```python
import functools
import math

import jax
import jax.numpy as jnp
import numpy as np
from jax import lax
from jax.experimental import pallas as pl
from jax.experimental.pallas import tpu as pltpu

F32 = jnp.float32
BF16 = jnp.bfloat16

GRID_W = 64
DA_HEADS = 8
DA_HEAD_DIM = 64
DA_WIDTH = DA_HEADS * 2 * DA_HEAD_DIM
GQ_HEADS = 8
GQ_KV_HEADS = 2
GQ_GROUP = GQ_HEADS // GQ_KV_HEADS
GQ_HEAD_DIM = 128
GQ_WIDTH = GQ_HEADS * GQ_HEAD_DIM
GQ_KV_WIDTH = GQ_KV_HEADS * GQ_HEAD_DIM
S5_WIDTH = 1024
S5_GROUP = 16
S5_GROUPS = S5_WIDTH // S5_GROUP
S5_STATE = 64
S5_MAX_REAL = -1e-4
N_BRANCH = 3
ROPE_THETA = 10000.0
LN_EPS = 1e-6
RMS_EPS = 1e-6
DA_SUBLN_EPS = 1e-5

OFF_QA = 0
OFF_KA = OFF_QA + DA_WIDTH
OFF_VA = OFF_KA + DA_WIDTH
OFF_QB = OFF_VA + DA_WIDTH
OFF_KB = OFF_QB + GQ_WIDTH
OFF_VB = OFF_KB + GQ_KV_WIDTH
OFF_U = OFF_VB + GQ_KV_WIDTH
OFF_GATE = OFF_U + S5_WIDTH

LANES = 128
MXU_DIM = 256
V7X_VMEM_BYTES = 64 * 1024 * 1024
VMEM_LIMIT = V7X_VMEM_BYTES - 8 * 1024 * 1024

S5_CHUNK = 16
S5_CW = S5_CHUNK * S5_GROUP
S5_BLOCK_GROUPS = LANES // S5_GROUP
S5_SCAN_UNROLL = 8

LOG2E = math.log2(math.e)
NEG_BIG = -1e30


def _cparams(*sem):
    return pltpu.CompilerParams(dimension_semantics=sem, vmem_limit_bytes=VMEM_LIMIT)


def _tile(n, target, mult):
    best = None
    for t in range(mult, min(n, target) + 1, mult):
        if n % t == 0:
            best = t
    assert best is not None, (n, target, mult)
    return best


def _const_spec(block_shape):
    return pl.BlockSpec(block_shape, lambda *_: (0,) * len(block_shape))


def _sigmoid(x):
    return 1.0 / (1.0 + jnp.exp(-x))


def _gelu_tanh(x):
    return 0.5 * x * (1.0 + jnp.tanh(math.sqrt(2.0 / math.pi) * (x + 0.044715 * (x * x * x))))


def _row_select(vec_ref, row0, tm, n_lat):
    rows = row0 + lax.broadcasted_iota(jnp.int32, (tm, 1), 0)
    return jnp.where(rows < n_lat, vec_ref[0:1, :], vec_ref[1:2, :])


def _layer_norm(z, g, b):
    mu = jnp.mean(z, axis=-1, keepdims=True)
    zc = z - mu
    var = jnp.mean(zc * zc, axis=-1, keepdims=True)
    return zc * lax.rsqrt(var + LN_EPS) * g + b


def _mod_kernel(c_ref, w_ref, b_ref, o_ref):
    a = c_ref[...]
    a = a * _sigmoid(a)
    o_ref[0] = jnp.dot(a.astype(BF16), w_ref[0].astype(BF16), preferred_element_type=F32) + b_ref[0]


def _ada_mod(c_rows, w_ada, b_ada):
    depth, d, n6 = w_ada.shape
    tn = _tile(n6, 1024, LANES)
    return pl.pallas_call(
        _mod_kernel,
        out_shape=jax.ShapeDtypeStruct((depth, 8, n6), F32),
        grid=(depth, n6 // tn),
        in_specs=[
            pl.BlockSpec((8, d), lambda l, j: (0, 0)),
            pl.BlockSpec((1, d, tn), lambda l, j: (l, 0, j)),
            pl.BlockSpec((1, 1, tn), lambda l, j: (l, 0, j)),
        ],
        out_specs=pl.BlockSpec((1, 8, tn), lambda l, j: (l, 0, j)),
        compiler_params=_cparams("parallel", "parallel"),
        name="ada_mod",
    )(c_rows, w_ada, b_ada.reshape(depth, 1, n6))


def _in_proj_kernel(x_ref, sh_ref, sc_ref, w_ref, o_ref, h_scr, *, tm, n_lat):
    i = pl.program_id(0)

    @pl.when(pl.program_id(1) == 0)
    def _():
        sh = _row_select(sh_ref, i * tm, tm, n_lat)
        sc = _row_select(sc_ref, i * tm, tm, n_lat)
        h_scr[...] = (x_ref[...] * (1.0 + sc) + sh).astype(BF16)

    o_ref[...] = jnp.dot(h_scr[...], w_ref[...].astype(BF16), preferred_element_type=F32).astype(o_ref.dtype)


def _in_proj(xs, mod, w_in, layer, n_lat):
    r, d = xs.shape
    n = w_in.shape[2]
    tm = _tile(r, 1056, 16)
    tn = _tile(n, 512, LANES)
    return pl.pallas_call(
        functools.partial(_in_proj_kernel, tm=tm, n_lat=n_lat),
        out_shape=jax.ShapeDtypeStruct((r, n), BF16),
        grid=(r // tm, n // tn),
        in_specs=[
            pl.BlockSpec((tm, d), lambda i, j: (i, 0)),
            pl.BlockSpec((8, d), lambda i, j: (0, 0)),
            pl.BlockSpec((8, d), lambda i, j: (0, 1)),
            pl.BlockSpec((None, d, tn), lambda i, j: (layer, 0, j)),
        ],
        out_specs=pl.BlockSpec((tm, tn), lambda i, j: (i, j)),
        scratch_shapes=[pltpu.VMEM((tm, d), BF16)],
        compiler_params=_cparams("parallel", "arbitrary"),
        name="in_proj",
    )(xs, mod, mod, w_in)


def _rope(z, cos, sin_a, sin_b, quarter):
    return z * cos + pltpu.roll(z, LANES - quarter, 1) * sin_a + pltpu.roll(z, quarter, 1) * sin_b


def _rms(z, w, eps):
    return z * lax.rsqrt(jnp.mean(z * z, axis=-1, keepdims=True) + eps) * w


def _prep_kernel(qa_ref, ka_ref, qb_ref, kb_ref, tda_ref, tgq_ref, qn_ref, kn_ref,
                 qa_o, kat_o, qb_o, kbt_o):
    cda, sda_a, sda_b = tda_ref[0], tda_ref[1], tda_ref[2]
    cgq, sgq_a, sgq_b = tgq_ref[0], tgq_ref[1], tgq_ref[2]
    da_scale = DA_HEAD_DIM ** -0.5 * LOG2E
    gq_scale = GQ_HEAD_DIM ** -0.5 * LOG2E
    for h in range(DA_HEADS):
        sl = slice(h * LANES, (h + 1) * LANES)
        z = qa_ref[:, sl].astype(F32)
        qa_o[:, sl] = (_rope(z, cda, sda_a, sda_b, DA_HEAD_DIM // 4) * da_scale).astype(BF16)
        z = ka_ref[:, sl].astype(F32)
        kat_o[h, 0] = _rope(z, cda, sda_a, sda_b, DA_HEAD_DIM // 4).T.astype(BF16)
    for h in range(GQ_HEADS):
        sl = slice(h * LANES, (h + 1) * LANES)
        z = _rms(qb_ref[:, sl].astype(F32), qn_ref[...], RMS_EPS)
        qb_o[:, sl] = (_rope(z, cgq, sgq_a, sgq_b, GQ_HEAD_DIM // 4) * gq_scale).astype(BF16)
    for h in range(GQ_KV_HEADS):
        sl = slice(h * LANES, (h + 1) * LANES)
        z = _rms(kb_ref[:, sl].astype(F32), kn_ref[...], RMS_EPS)
        kbt_o[h, 0] = _rope(z, cgq, sgq_a, sgq_b, GQ_HEAD_DIM // 4).T.astype(BF16)


def _prep_qk(proj, tab_da, tab_gq, qn_w, kn_w, tk):
    r = proj.shape[0]
    nk = r // tk
    blk = lambda w, off: pl.BlockSpec((tk, w), lambda i: (i, off // w))
    out = lambda w: pl.BlockSpec((tk, w), lambda i: (i, 0))
    outt = lambda heads: pl.BlockSpec((heads, 1, LANES, tk), lambda i: (0, i, 0, 0))
    return pl.pallas_call(
        _prep_kernel,
        out_shape=(
            jax.ShapeDtypeStruct((r, DA_WIDTH), BF16),
            jax.ShapeDtypeStruct((DA_HEADS, nk, LANES, tk), BF16),
            jax.ShapeDtypeStruct((r, GQ_WIDTH), BF16),
            jax.ShapeDtypeStruct((GQ_KV_HEADS, nk, LANES, tk), BF16),
        ),
        grid=(nk,),
        in_specs=[
            blk(DA_WIDTH, OFF_QA), blk(DA_WIDTH, OFF_KA), blk(GQ_WIDTH, OFF_QB), blk(GQ_KV_WIDTH, OFF_KB),
            pl.BlockSpec((3, tk, LANES), lambda i: (0, i, 0)),
            pl.BlockSpec((3, tk, LANES), lambda i: (0, i, 0)),
            _const_spec((1, LANES)),
            _const_spec((1, LANES)),
        ],
        out_specs=(out(DA_WIDTH), outt(DA_HEADS), out(GQ_WIDTH), outt(GQ_KV_HEADS)),
        compiler_params=_cparams("parallel"),
        name="prep_qk",
    )(proj, proj, proj, proj, tab_da, tab_gq, qn_w.reshape(1, LANES), kn_w.reshape(1, LANES))


def _rope_tables(n_lat, n_ctx, dim):
    rows = n_lat // GRID_W
    row = jnp.broadcast_to(jnp.arange(rows, dtype=F32)[:, None], (rows, GRID_W)).reshape(-1)
    col = jnp.broadcast_to(jnp.arange(GRID_W, dtype=F32)[None, :], (rows, GRID_W)).reshape(-1)
    axis_dim = dim // 2
    inv_freq = ROPE_THETA ** (-jnp.arange(0, axis_dim, 2, dtype=F32) / axis_dim)
    ang_r = row[:, None] * inv_freq[None, :]
    ang_c = col[:, None] * inv_freq[None, :]
    ang = jnp.concatenate([ang_r, ang_r, ang_c, ang_c], axis=-1)
    cos, sin = jnp.cos(ang), jnp.sin(ang)
    first_half = (jnp.arange(dim) % (dim // 2)) < (dim // 4)
    sin_a = jnp.where(first_half[None, :], -sin, 0.0)
    sin_b = jnp.where(first_half[None, :], 0.0, sin)
    tab = jnp.stack([cos, sin_a, sin_b])
    ident = jnp.stack([jnp.ones((n_ctx, dim), F32), jnp.zeros((n_ctx, dim), F32), jnp.zeros((n_ctx, dim), F32)])
    tab = jnp.concatenate([tab, ident], axis=1)
    return jnp.tile(tab, (1, 1, LANES // dim))


def _flash_step(s_buf, vx, m_scr, a_scr):
    s = s_buf[...]
    m_old = m_scr[...]
    m = jnp.maximum(m_old, jnp.max(s, axis=-1, keepdims=True))
    p = jnp.exp2((s - m).astype(BF16))
    m_scr[...] = m
    a_scr[...] = jnp.exp2(m_old - m) * a_scr[...] + jnp.dot(p, vx, preferred_element_type=F32)


def _flash_loop(qs, kt_ref, v_ref, vx_scr, bufs, stats, tk, n_chunks):
    @pl.when(pl.program_id(1) == 0)
    def _():
        vx_scr[:, :LANES] = v_ref[...]
        vx_scr[:, LANES:] = jnp.ones((vx_scr.shape[0], LANES), vx_scr.dtype)

    for m_scr, a_scr in stats:
        m_scr[...] = jnp.full_like(m_scr, NEG_BIG)
        a_scr[...] = jnp.zeros_like(a_scr)

    def scores(c, slot):
        kt = kt_ref[c]
        for q, buf in zip(qs, bufs):
            buf[slot][...] = jnp.dot(q, kt, preferred_element_type=F32)

    def update(c, slot):
        vx = vx_scr[c * tk:(c + 1) * tk, :]
        for buf, st in zip(bufs, stats):
            _flash_step(buf[slot], vx, *st)

    scores(0, 0)
    for c in range(n_chunks):
        if c + 1 < n_chunks:
            scores(c + 1, (c + 1) % 2)
        update(c, c % 2)


def _softmax_once(s, v):
    p = jnp.exp2(s - jnp.max(s, axis=-1, keepdims=True))
    return jnp.dot(p.astype(BF16), v, preferred_element_type=F32) / jnp.sum(p, axis=-1, keepdims=True)


def _da_split(q):
    lane = lax.broadcasted_iota(jnp.int32, (1, LANES), 1)
    zero = jnp.zeros_like(q)
    return jnp.where(lane < DA_HEAD_DIM, q, zero), jnp.where(lane < DA_HEAD_DIM, zero, q)


def _da_finish(o1, o2, lam_ref, w_ref, o_ref, lam_init):
    lp = lam_ref[...]
    lam = (jnp.exp(jnp.sum(lp[0:1] * lp[1:2], axis=-1, keepdims=True))
           - jnp.exp(jnp.sum(lp[2:3] * lp[3:4], axis=-1, keepdims=True)) + lam_init)
    o = _rms(o1 - lam * o2, w_ref[...], DA_SUBLN_EPS) * (1.0 - lam_init)
    o_ref[...] = o.astype(o_ref.dtype)


def _normalised(a_scr):
    return a_scr[:, :LANES] / a_scr[:, LANES:]


def _da_kernel(lam_ref, w_ref, q_ref, kt_ref, v_ref, o_ref, vx, s1a, s1b, m1, a1, s2a, s2b, m2, a2,
               *, tk, n_chunks, lam_init):
    q1, q2 = _da_split(q_ref[...])
    _flash_loop((q1, q2), kt_ref, v_ref, vx, ((s1a, s1b), (s2a, s2b)), ((m1, a1), (m2, a2)), tk, n_chunks)
    _da_finish(_normalised(a1), _normalised(a2), lam_ref, w_ref, o_ref, lam_init)


def _da_ctx_kernel(lam_ref, w_ref, q_ref, kt_ref, v_ref, latent_out_ref, o_ref, *, off, n_ctx, lam_init):
    del latent_out_ref
    q1, q2 = _da_split(q_ref[...])
    kt = kt_ref[:, off:off + n_ctx]
    v = v_ref[...]
    o1 = _softmax_once(jnp.dot(q1, kt, preferred_element_type=F32), v)
    o2 = _softmax_once(jnp.dot(q2, kt, preferred_element_type=F32), v)
    _da_finish(o1, o2, lam_ref, w_ref, o_ref, lam_init)


def _gq_kernel(q_ref, kt_ref, v_ref, o_ref, vx, sa, sb, m1, a1, *, tk, n_chunks):
    _flash_loop((q_ref[...],), kt_ref, v_ref, vx, ((sa, sb),), ((m1, a1),), tk, n_chunks)
    o_ref[...] = _normalised(a1).astype(o_ref.dtype)


def _gq_ctx_kernel(q_ref, kt_ref, v_ref, latent_out_ref, o_ref, *, off, n_ctx):
    del latent_out_ref
    kt = kt_ref[:, off:off + n_ctx]
    o_ref[...] = _softmax_once(jnp.dot(q_ref[...], kt, preferred_element_type=F32), v_ref[...]).astype(o_ref.dtype)


def _attention(params, q, kt, proj, v_off, kv_group, n_lat, main_kernel, ctx_kernel, n_maps, name):
    r, width = q.shape
    n_heads = width // LANES
    n_ctx = r - n_lat
    _, n_chunks, _, tk = kt.shape
    tq = _tile(n_lat, 512, 16)
    ctx_blk = n_lat // n_ctx
    off = n_lat - (n_chunks - 1) * tk
    param_specs = [_const_spec(p.shape) for p in params]
    stat = pltpu.VMEM((tq, 1), F32)
    scores = pltpu.VMEM((tq, tk), F32)
    scratch = [pltpu.VMEM((r, 2 * LANES), BF16)] + n_maps * [scores, scores, stat, pltpu.VMEM((tq, 2 * LANES), F32)]
    out = pl.pallas_call(
        functools.partial(main_kernel, tk=tk, n_chunks=n_chunks),
        out_shape=jax.ShapeDtypeStruct((r, width), BF16),
        grid=(n_heads, n_lat // tq),
        in_specs=param_specs + [
            pl.BlockSpec((tq, LANES), lambda h, i: (i, h)),
            pl.BlockSpec((None, n_chunks, LANES, tk), lambda h, i: (h // kv_group, 0, 0, 0)),
            pl.BlockSpec((r, LANES), lambda h, i: (0, v_off // LANES + h // kv_group)),
        ],
        out_specs=pl.BlockSpec((tq, LANES), lambda h, i: (i, h)),
        scratch_shapes=scratch,
        compiler_params=_cparams("parallel", "arbitrary"),
        name=name,
    )(*params, q, kt, proj)
    return pl.pallas_call(
        functools.partial(ctx_kernel, off=off, n_ctx=n_ctx),
        out_shape=jax.ShapeDtypeStruct((r, width), BF16),
        grid=(n_heads,),
        in_specs=param_specs + [
            pl.BlockSpec((n_ctx, LANES), lambda h: (ctx_blk, h)),
            pl.BlockSpec((None, None, LANES, tk), lambda h: (h // kv_group, n_chunks - 1, 0, 0)),
            pl.BlockSpec((n_ctx, LANES), lambda h: (ctx_blk, v_off // LANES + h // kv_group)),
            pl.BlockSpec(memory_space=pl.ANY),
        ],
        out_specs=pl.BlockSpec((n_ctx, LANES), lambda h: (ctx_blk, h)),
        input_output_aliases={len(params) + 3: 0},
        compiler_params=_cparams("parallel"),
        name=name + "_ctx",
    )(*params, q, kt, proj, out)


def _diff_attention(lam_params, subln_w, qa, kat, proj, n_lat, lam_init):
    return _attention((lam_params, subln_w.reshape(1, LANES)), qa, kat, proj, OFF_VA, 1, n_lat,
                      functools.partial(_da_kernel, lam_init=lam_init),
                      functools.partial(_da_ctx_kernel, lam_init=lam_init), 2, "diff_attention")


def _gq_attention(qb, kbt, proj, n_lat):
    return _attention((), qb, kbt, proj, OFF_VB, GQ_GROUP, n_lat, _gq_kernel, _gq_ctx_kernel, 1, "gq_attention")


def _s5_matrices(a_re, a_im, log_dt, b_re, b_im, c_re, c_im, d_skip):
    t = S5_CHUNK
    g, p, h = S5_GROUPS, S5_STATE, S5_GROUP
    a_re = jnp.minimum(a_re.astype(F32), S5_MAX_REAL)
    a_im = a_im.astype(F32)
    dt = jnp.exp(log_dt.astype(F32))[..., None]
    mag = jnp.exp(a_re * dt)
    lr = mag * jnp.cos(a_im * dt)
    li = mag * jnp.sin(a_im * dt)
    den = jnp.square(a_re) + jnp.square(a_im)
    nr = lr - 1.0
    cr = (nr * a_re + li * a_im) / den
    ci = (li * a_re - nr * a_im) / den
    bbr = cr[..., None] * b_re - ci[..., None] * b_im
    bbi = cr[..., None] * b_im + ci[..., None] * b_re
    j = jnp.arange(t + 1, dtype=F32)[:, None, None, None]
    pmag = jnp.exp(j * (a_re * dt)[None])
    pr = pmag * jnp.cos(j * (a_im * dt)[None])
    pi = pmag * jnp.sin(j * (a_im * dt)[None])
    clr = c_re[None] * pr[:, :, :, None, :] - c_im[None] * pi[:, :, :, None, :]
    cli = c_re[None] * pi[:, :, :, None, :] + c_im[None] * pr[:, :, :, None, :]
    kern = (jnp.einsum('jdghp,dgpk->jdghk', clr[:t], bbr) - jnp.einsum('jdghp,dgpk->jdghk', cli[:t], bbi))
    eye = jnp.eye(h, dtype=F32)[None] * d_skip.astype(F32).reshape(g, h)[:, :, None]
    k0 = kern[0, 0] + kern[0, 1] + eye
    kcat = jnp.concatenate([kern[1:, 1][::-1], k0[None], kern[1:, 0]], axis=0)
    idx = (jnp.arange(t)[None, :] - jnp.arange(t)[:, None]) + (t - 1)
    m = kcat[idx]
    m = m.transpose(2, 0, 4, 1, 3).reshape(g, t * h, t * h)

    def to_state(d, pw):
        wr = pw[0][:, :, :, None] * bbr[d][None] - pw[1][:, :, :, None] * bbi[d][None]
        wi = pw[0][:, :, :, None] * bbi[d][None] + pw[1][:, :, :, None] * bbr[d][None]
        f = lambda w: w.transpose(1, 0, 3, 2).reshape(g, t * h, p)
        return f(wr), f(wi)

    wf = to_state(0, (pr[:t, 0][::-1], pi[:t, 0][::-1]))
    wb = to_state(1, (pr[:t, 1], pi[:t, 1]))
    mcat = jnp.concatenate([m, wf[0], wf[1], wb[0], wb[1]], axis=-1)

    def from_state(d, cl_r, cl_i):
        f = lambda c: c.transpose(1, 3, 0, 2).reshape(g, p, t * h)
        return f(cl_r), -f(cl_i)

    nf = from_state(0, clr[1:, 0], cli[1:, 0])
    nb = from_state(1, clr[1:, 1][::-1], cli[1:, 1][::-1])
    ncat = jnp.concatenate([nf[0], nf[1], nb[0], nb[1]], axis=1)

    def coef(d):
        r_, i_ = pr[t, d], pi[t, d]
        return [jnp.concatenate([r_, r_], -1), jnp.concatenate([-i_, i_], -1), jnp.concatenate([i_, -i_], -1)]

    coefs = jnp.stack(coef(0) + coef(1))
    return mcat.astype(BF16), ncat.astype(BF16), coefs


def _s5_lane_perm():
    i = np.arange(S5_BLOCK_GROUPS * S5_CW)
    tok, grp, ch = i // LANES, (i // S5_GROUP) % S5_BLOCK_GROUPS, i % S5_GROUP
    perm = np.zeros((i.size, i.size), np.float32)
    perm[i, grp * S5_CW + tok * S5_GROUP + ch] = 1.0
    return perm


def _s5_a_kernel(*refs):
    u_refs = refs[:S5_CHUNK]
    p_ref, m_ref, y_ref, s_ref = refs[S5_CHUNK:]
    u_nat = jnp.concatenate([u[...] for u in u_refs], axis=1)
    ug = jnp.dot(u_nat, p_ref[...], preferred_element_type=F32).astype(BF16)
    for gl in range(S5_BLOCK_GROUPS):
        r = jnp.dot(ug[:, gl * S5_CW:(gl + 1) * S5_CW], m_ref[gl], preferred_element_type=F32)
        y_ref[gl] = r[:, :S5_CW]
        s_ref[gl] = r[:, S5_CW:]


def _s5_b_kernel(s_ref, coef_ref, x_ref, sw_scr, *, n_chunks, n_lat_chunks):
    half = 2 * S5_STATE
    n_ctx_chunks = n_chunks - n_lat_chunks
    fwd_sl, bwd_sl = slice(0, half), slice(half, 2 * half)

    def swap(c, carry):
        sw_scr[c, :, fwd_sl] = pltpu.roll(s_ref[c, :, fwd_sl], S5_STATE, 1)
        sw_scr[c, :, bwd_sl] = pltpu.roll(s_ref[c, :, bwd_sl], S5_STATE, 1)
        return carry

    lax.fori_loop(0, n_chunks, swap, 0, unroll=S5_SCAN_UNROLL)

    def step(sl, a, b, bt):
        def f(c, carry):
            x, xt = carry
            x_ref[c, :, sl] = x.astype(x_ref.dtype)
            return a * x + b * xt + s_ref[c, :, sl], a * xt + bt * x + sw_scr[c, :, sl]
        return f

    fwd = step(fwd_sl, coef_ref[0], coef_ref[1], coef_ref[2])
    bwd = step(bwd_sl, coef_ref[3], coef_ref[4], coef_ref[5])
    zero = jnp.zeros_like(coef_ref[0])
    loop = functools.partial(lax.fori_loop, unroll=S5_SCAN_UNROLL)
    carry = loop(0, n_ctx_chunks, lambda i, cr: fwd(n_lat_chunks + i, cr), (zero, zero))
    loop(0, n_lat_chunks, fwd, carry)
    carry = loop(0, n_ctx_chunks, lambda i, cr: bwd(n_chunks - 1 - i, cr), (zero, zero))
    loop(0, n_lat_chunks, lambda i, cr: bwd(n_lat_chunks - 1 - i, cr), carry)


def _s5_c_kernel(x_ref, n_ref, yi_ref, q_ref, y_ref):
    ys = [(yi_ref[gl] + jnp.dot(x_ref[gl], n_ref[gl], preferred_element_type=F32)).astype(BF16)
          for gl in range(S5_BLOCK_GROUPS)]
    y_ref[...] = jnp.dot(jnp.concatenate(ys, axis=1), q_ref[...], preferred_element_type=F32).astype(y_ref.dtype)


def _s5_mixer(u, mcat, ncat, coefs, layer, n_lat):
    r = u.shape[0]
    g, t, gb = S5_GROUPS, S5_CHUNK, S5_BLOCK_GROUPS
    nc = r // t
    sw = 4 * S5_STATE
    nblk = g // gb
    perm = _s5_lane_perm()
    p_in = jnp.asarray(perm, BF16)
    p_out = jnp.asarray(perm.T, BF16)
    u_rows = u.reshape(nc, t * S5_WIDTH)
    grp = lambda a, b: pl.BlockSpec((gb, a, b), lambda i: (i, 0, 0))
    wgrp = lambda a, b: pl.BlockSpec((None, gb, a, b), lambda i: (layer, i, 0, 0))
    perm_spec = pl.BlockSpec(perm.shape, lambda i: (0, 0), pipeline_mode=pl.Buffered(1))
    tok_specs = [pl.BlockSpec((nc, LANES), lambda i, s=s: (0, s * nblk + i)) for s in range(t)]
    y_intra, s = pl.pallas_call(
        _s5_a_kernel,
        out_shape=(jax.ShapeDtypeStruct((g, nc, S5_CW), F32), jax.ShapeDtypeStruct((g, nc, sw), F32)),
        grid=(nblk,),
        in_specs=tok_specs + [perm_spec, wgrp(S5_CW, S5_CW + sw)],
        out_specs=(grp(nc, S5_CW), grp(nc, sw)),
        compiler_params=_cparams("parallel"),
        name="s5_chunk_local",
    )(*([u_rows] * t), p_in, mcat)

    gs = 16
    st = s.transpose(1, 0, 2)
    xt = pl.pallas_call(
        functools.partial(_s5_b_kernel, n_chunks=nc, n_lat_chunks=n_lat // t),
        out_shape=jax.ShapeDtypeStruct((nc, g, sw), BF16),
        grid=(g // gs,),
        in_specs=[pl.BlockSpec((nc, gs, sw), lambda i: (0, i, 0)),
                  pl.BlockSpec((None, 6, gs, 2 * S5_STATE), lambda i: (layer, 0, i, 0))],
        out_specs=pl.BlockSpec((nc, gs, sw), lambda i: (0, i, 0)),
        scratch_shapes=[pltpu.VMEM((nc, gs, sw), F32)],
        compiler_params=_cparams("parallel"),
        name="s5_chunk_scan",
    )(st, coefs)

    x_in = xt.transpose(1, 0, 2)
    y_blocks = pl.pallas_call(
        _s5_c_kernel,
        out_shape=jax.ShapeDtypeStruct((nblk, nc, gb * S5_CW), BF16),
        grid=(nblk,),
        in_specs=[grp(nc, sw), wgrp(sw, S5_CW), grp(nc, S5_CW), perm_spec],
        out_specs=pl.BlockSpec((None, nc, gb * S5_CW), lambda i: (i, 0, 0)),
        compiler_params=_cparams("parallel"),
        name="s5_state_out",
    )(x_in, ncat, y_intra, p_out)
    return y_blocks.reshape(nblk, nc, t, LANES).transpose(1, 2, 0, 3).reshape(r, S5_WIDTH)


def _merge_kernel(oa_ref, ob_ref, y_ref, wpa_ref, wpb_ref, wga_ref, wgg_ref, g0_ref, g1_ref, g2_ref,
                  o_ref, gy_scr):
    @pl.when(pl.program_id(1) == 0)
    def _():
        gy_scr[...] = _gelu_tanh(y_ref[...].astype(F32)).astype(BF16)

    dot = lambda a, b: jnp.dot(a, b, preferred_element_type=F32)
    pa = dot(oa_ref[...], wpa_ref[...])
    pb = dot(ob_ref[...], wpb_ref[...])
    gy = gy_scr[...]
    pc = dot(gy, wga_ref[...]) * _sigmoid(dot(gy, wgg_ref[...]))
    sg = lambda ref: _sigmoid(ref[...].astype(F32))
    o_ref[...] = (sg(g0_ref) * pa + sg(g1_ref) * pb + sg(g2_ref) * pc).astype(o_ref.dtype)


def _merge(o_a, o_b, y_s5, proj, w_pa, w_pb, w_glu, layer, d):
    r = o_a.shape[0]
    tm = _tile(r, 1056, 16)
    tn = _tile(math.gcd(d, OFF_GATE), 512, LANES)
    row = lambda w: pl.BlockSpec((tm, w), lambda i, j: (i, 0))
    col = lambda k, off: pl.BlockSpec((None, k, tn), lambda i, j: (layer, 0, off // tn + j))
    gate = lambda b: pl.BlockSpec((tm, tn), lambda i, j: (i, (OFF_GATE + b * d) // tn + j))
    return pl.pallas_call(
        _merge_kernel,
        out_shape=jax.ShapeDtypeStruct((r, d), BF16),
        grid=(r // tm, d // tn),
        in_specs=[row(DA_WIDTH), row(GQ_WIDTH), row(S5_WIDTH),
                  col(DA_WIDTH, 0), col(GQ_WIDTH, 0), col(S5_WIDTH, 0), col(S5_WIDTH, d),
                  gate(0), gate(1), gate(2)],
        out_specs=pl.BlockSpec((tm, tn), lambda i, j: (i, j)),
        scratch_shapes=[pltpu.VMEM((tm, S5_WIDTH), BF16)],
        compiler_params=_cparams("parallel", "arbitrary"),
        name="branch_merge",
    )(o_a, o_b, y_s5, w_pa, w_pb, w_glu, w_glu, proj, proj, proj)


def _out_ln_kernel(m_ref, w_ref, x_ref, gm_ref, lg_ref, lb_ref, o_ref, *, tm, n_lat, alpha):
    y = jnp.dot(m_ref[...], w_ref[...], preferred_element_type=F32)
    gate = _row_select(gm_ref, pl.program_id(0) * tm, tm, n_lat)
    o_ref[...] = _layer_norm(alpha * x_ref[...] + gate * y, lg_ref[...], lb_ref[...])


def _out_ln(merged, w_o, layer, xs, mod, ln_g, ln_b, n_lat, alpha):
    r, d = xs.shape
    tm = _tile(r, 528, 16)
    return pl.pallas_call(
        functools.partial(_out_ln_kernel, tm=tm, n_lat=n_lat, alpha=alpha),
        out_shape=jax.ShapeDtypeStruct((r, d), F32),
        grid=(r // tm,),
        in_specs=[
            pl.BlockSpec((tm, d), lambda i: (i, 0)),
            pl.BlockSpec((None, d, d), lambda i: (layer, 0, 0)),
            pl.BlockSpec((tm, d), lambda i: (i, 0)),
            pl.BlockSpec((8, d), lambda i: (0, 2)),
            _const_spec((1, d)), _const_spec((1, d)),
        ],
        out_specs=pl.BlockSpec((tm, d), lambda i: (i, 0)),
        compiler_params=_cparams("parallel"),
        name="out_proj_ln",
    )(merged, w_o, xs, mod, ln_g.reshape(1, d), ln_b.reshape(1, d))


def _ffn_kernel(x_ref, sh_ref, sc_ref, gf_ref, wg_ref, wu_ref, wd_ref, lg_ref, lb_ref, o_ref,
                h_scr, *, tm, n_lat, alpha):
    i = pl.program_id(0)
    f = pl.program_id(1)

    @pl.when(f == 0)
    def _():
        sh = _row_select(sh_ref, i * tm, tm, n_lat)
        sc = _row_select(sc_ref, i * tm, tm, n_lat)
        h_scr[...] = (x_ref[...] * (1.0 + sc) + sh).astype(BF16)
        o_ref[...] = jnp.zeros_like(o_ref)

    h = h_scr[...]
    gate = jnp.dot(h, wg_ref[...].astype(BF16), preferred_element_type=F32)
    up = jnp.dot(h, wu_ref[...].astype(BF16), preferred_element_type=F32)
    act = (gate * _sigmoid(gate) * up).astype(BF16)
    o_ref[...] += jnp.dot(act, wd_ref[...].astype(BF16), preferred_element_type=F32)

    @pl.when(f == pl.num_programs(1) - 1)
    def _():
        g = _row_select(gf_ref, i * tm, tm, n_lat)
        o_ref[...] = _layer_norm(alpha * x_ref[...] + g * o_ref[...], lg_ref[...], lb_ref[...])


def _ffn(xs, r, mod, w_gate, w_up, w_down, layer, ln_g, ln_b, n_lat, alpha):
    d = xs.shape[1]
    f = w_gate.shape[2]
    tm = _tile(r, 1056, 16)
    tf = _tile(f, MXU_DIM, LANES)
    modv = lambda c: pl.BlockSpec((8, d), lambda i, j: (0, c))
    return pl.pallas_call(
        functools.partial(_ffn_kernel, tm=tm, n_lat=n_lat, alpha=alpha),
        out_shape=jax.ShapeDtypeStruct((r, d), F32),
        grid=(r // tm, f // tf),
        in_specs=[
            pl.BlockSpec((tm, d), lambda i, j: (i, 0), pipeline_mode=pl.Buffered(1)),
            modv(3), modv(4), modv(5),
            pl.BlockSpec((None, d, tf), lambda i, j: (layer, 0, j)),
            pl.BlockSpec((None, d, tf), lambda i, j: (layer, 0, j)),
            pl.BlockSpec((None, tf, d), lambda i, j: (layer, j, 0)),
            _const_spec((1, d)), _const_spec((1, d)),
        ],
        out_specs=pl.BlockSpec((tm, d), lambda i, j: (i, 0)),
        scratch_shapes=[pltpu.VMEM((tm, d), BF16)],
        compiler_params=_cparams("parallel", "arbitrary"),
        name="swiglu_ln",
    )(xs, mod, mod, mod, w_gate, w_up, w_down, ln_g.reshape(1, d), ln_b.reshape(1, d))


def kernel(x, c, ctx, c_ctx, w_ada, b_ada, w_in, da_lam_q1, da_lam_k1, da_lam_q2, da_lam_k2, da_subln_w, w_pa, gq_qnorm_w, gq_knorm_w, w_pb, s5_a_re, s5_a_im, s5_log_dt, s5_b_re, s5_b_im, s5_c_re, s5_c_im, s5_d, w_glu, w_o, ln_mix_g, ln_mix_b, w_ffn_gate, w_ffn_up, w_ffn_down, ln_ffn_g, ln_ffn_b):
    batch, n_lat, d = x.shape
    n_ctx = ctx.shape[1]
    depth = w_in.shape[0]
    r = n_lat + n_ctx
    assert batch == 1 and n_lat % GRID_W == 0 and n_ctx % S5_CHUNK == 0 and n_lat % S5_CHUNK == 0
    assert n_lat % n_ctx == 0
    alpha = (2 * depth) ** 0.25
    tk = _tile(r, 768, MXU_DIM)
    assert n_ctx <= tk and (n_lat - (r // tk - 1) * tk) % LANES == 0

    xs = jnp.concatenate([x[0], ctx[0]], axis=0).astype(F32)
    c_rows = jnp.zeros((8, d), F32).at[0].set(c[0]).at[1].set(c_ctx)
    mods = _ada_mod(c_rows, w_ada, b_ada)
    tab_da = _rope_tables(n_lat, n_ctx, DA_HEAD_DIM)
    tab_gq = _rope_tables(n_lat, n_ctx, GQ_HEAD_DIM)

    w_pa, w_pb, w_glu, w_o = (w.astype(BF16) for w in (w_pa, w_pb, w_glu, w_o))
    mcat, ncat, coefs = jax.vmap(_s5_matrices)(s5_a_re, s5_a_im, s5_log_dt, s5_b_re, s5_b_im, s5_c_re, s5_c_im, s5_d)

    for l in range(depth):
        lam_init = 0.8 - 0.6 * math.exp(-0.3 * l)
        mod = mods[l]
        proj = _in_proj(xs, mod, w_in, l, n_lat)
        qa, kat, qb, kbt = _prep_qk(proj, tab_da, tab_gq, gq_qnorm_w[l], gq_knorm_w[l], tk)
        lam_params = jnp.stack([da_lam_q1[l], da_lam_k1[l], da_lam_q2[l], da_lam_k2[l]]).astype(F32)
        o_a = _diff_attention(lam_params, da_subln_w[l], qa, kat, proj, n_lat, lam_init)
        o_b = _gq_attention(qb, kbt, proj, n_lat)
        y_s5 = _s5_mixer(proj[:, OFF_U:OFF_U + S5_WIDTH], mcat, ncat, coefs, l, n_lat)
        merged = _merge(o_a, o_b, y_s5, proj, w_pa, w_pb, w_glu, l, d)
        xs = _out_ln(merged, w_o, l, xs, mod, ln_mix_g[l], ln_mix_b[l], n_lat, alpha)
        rows_out = n_lat if l == depth - 1 else r
        xs = _ffn(xs, rows_out, mod, w_ffn_gate, w_ffn_up, w_ffn_down, l, ln_ffn_g[l], ln_ffn_b[l], n_lat, alpha)
    return xs[None].astype(x.dtype)
```

```python
import functools
import math

import jax
import jax.numpy as jnp
import numpy as np
from jax import lax
from jax.experimental import pallas as pl
from jax.experimental.pallas import tpu as pltpu

F32 = jnp.float32
BF16 = jnp.bfloat16

GRID_W = 64
DA_HEADS = 8
DA_HEAD_DIM = 64
DA_WIDTH = DA_HEADS * 2 * DA_HEAD_DIM
GQ_HEADS = 8
GQ_KV_HEADS = 2
GQ_GROUP = GQ_HEADS // GQ_KV_HEADS
GQ_HEAD_DIM = 128
GQ_WIDTH = GQ_HEADS * GQ_HEAD_DIM
GQ_KV_WIDTH = GQ_KV_HEADS * GQ_HEAD_DIM
S5_WIDTH = 1024
S5_GROUP = 16
S5_GROUPS = S5_WIDTH // S5_GROUP
S5_STATE = 64
S5_MAX_REAL = -1e-4
N_BRANCH = 3
ROPE_THETA = 10000.0
LN_EPS = 1e-6
RMS_EPS = 1e-6
DA_SUBLN_EPS = 1e-5

OFF_QA = 0
OFF_KA = OFF_QA + DA_WIDTH
OFF_VA = OFF_KA + DA_WIDTH
OFF_QB = OFF_VA + DA_WIDTH
OFF_KB = OFF_QB + GQ_WIDTH
OFF_VB = OFF_KB + GQ_KV_WIDTH
OFF_U = OFF_VB + GQ_KV_WIDTH
OFF_GATE = OFF_U + S5_WIDTH

LANES = 128
MXU_DIM = 256
V7X_VMEM_BYTES = 64 * 1024 * 1024
VMEM_LIMIT = V7X_VMEM_BYTES - 8 * 1024 * 1024

S5_CHUNK = 16
S5_CW = S5_CHUNK * S5_GROUP
S5_BLOCK_GROUPS = LANES // S5_GROUP
S5_SCAN_UNROLL = 8

LOG2E = math.log2(math.e)
NEG_BIG = -1e30


def _cparams(*sem):
    return pltpu.CompilerParams(dimension_semantics=sem, vmem_limit_bytes=VMEM_LIMIT)


def _tile(n, target, mult):
    best = None
    for t in range(mult, min(n, target) + 1, mult):
        if n % t == 0:
            best = t
    assert best is not None, (n, target, mult)
    return best


def _layer_spec(block_shape, layer, col=0):
    return pl.BlockSpec((None,) + tuple(block_shape), lambda *_: (layer, 0, col))


def _sigmoid(x):
    return 1.0 / (1.0 + jnp.exp(-x))


def _gelu_tanh(x):
    return 0.5 * x * (1.0 + jnp.tanh(math.sqrt(2.0 / math.pi) * (x + 0.044715 * (x * x * x))))


def _row_select(vec_ref, row0, tm, n_lat):
    rows = row0 + lax.broadcasted_iota(jnp.int32, (tm, 1), 0)
    return jnp.where(rows < n_lat, vec_ref[0:1, :], vec_ref[1:2, :])


def _layer_norm(z, g, b):
    mu = jnp.mean(z, axis=-1, keepdims=True)
    zc = z - mu
    var = jnp.mean(zc * zc, axis=-1, keepdims=True)
    return zc * lax.rsqrt(var + LN_EPS) * g + b


def _mod_kernel(c_ref, w_ref, b_ref, o_ref):
    a = c_ref[...]
    a = a * _sigmoid(a)
    o_ref[0] = jnp.dot(a.astype(BF16), w_ref[0].astype(BF16), preferred_element_type=F32) + b_ref[0]


def _ada_mod(c_rows, w_ada, b_ada):
    depth, d, n6 = w_ada.shape
    tn = _tile(n6, 1024, LANES)
    return pl.pallas_call(
        _mod_kernel,
        out_shape=jax.ShapeDtypeStruct((depth, 8, n6), F32),
        grid=(depth, n6 // tn),
        in_specs=[
            pl.BlockSpec((8, d), lambda l, j: (0, 0)),
            pl.BlockSpec((1, d, tn), lambda l, j: (l, 0, j)),
            pl.BlockSpec((1, 1, tn), lambda l, j: (l, 0, j)),
        ],
        out_specs=pl.BlockSpec((1, 8, tn), lambda l, j: (l, 0, j)),
        compiler_params=_cparams("parallel", "parallel"),
        name="ada_mod",
    )(c_rows, w_ada, b_ada.reshape(depth, 1, n6))


def _in_proj_kernel(x_ref, sh_ref, sc_ref, w_ref, o_ref, h_scr, *, tm, n_lat):
    i = pl.program_id(0)

    @pl.when(pl.program_id(1) == 0)
    def _():
        sh = _row_select(sh_ref, i * tm, tm, n_lat)
        sc = _row_select(sc_ref, i * tm, tm, n_lat)
        h_scr[...] = (x_ref[...] * (1.0 + sc) + sh).astype(BF16)

    o_ref[...] = jnp.dot(h_scr[...], w_ref[...].astype(BF16), preferred_element_type=F32).astype(o_ref.dtype)


def _in_proj(xs, mods, w_in, layer, n_lat):
    r, d = xs.shape
    n = w_in.shape[2]
    tm = _tile(r, 1056, 16)
    tn = _tile(n, 512, LANES)
    return pl.pallas_call(
        functools.partial(_in_proj_kernel, tm=tm, n_lat=n_lat),
        out_shape=jax.ShapeDtypeStruct((r, n), BF16),
        grid=(r // tm, n // tn),
        in_specs=[
            pl.BlockSpec((tm, d), lambda i, j: (i, 0)),
            _layer_spec((8, d), layer, 0),
            _layer_spec((8, d), layer, 1),
            pl.BlockSpec((None, d, tn), lambda i, j: (layer, 0, j)),
        ],
        out_specs=pl.BlockSpec((tm, tn), lambda i, j: (i, j)),
        scratch_shapes=[pltpu.VMEM((tm, d), BF16)],
        compiler_params=_cparams("parallel", "arbitrary"),
        name="in_proj",
    )(xs, mods, mods, w_in)


def _rope(z, cos, sin_a, sin_b, quarter):
    return z * cos + pltpu.roll(z, LANES - quarter, 1) * sin_a + pltpu.roll(z, quarter, 1) * sin_b


def _rms(z, w, eps):
    return z * lax.rsqrt(jnp.mean(z * z, axis=-1, keepdims=True) + eps) * w


def _prep_kernel(qa_ref, ka_ref, qb_ref, kb_ref, tda_ref, tgq_ref, qn_ref, kn_ref,
                 qa_o, kat_o, qb_o, kbt_o):
    cda, sda_a, sda_b = tda_ref[0], tda_ref[1], tda_ref[2]
    cgq, sgq_a, sgq_b = tgq_ref[0], tgq_ref[1], tgq_ref[2]
    da_scale = DA_HEAD_DIM ** -0.5 * LOG2E
    gq_scale = GQ_HEAD_DIM ** -0.5 * LOG2E
    for h in range(DA_HEADS):
        sl = slice(h * LANES, (h + 1) * LANES)
        z = qa_ref[:, sl].astype(F32)
        qa_o[:, sl] = (_rope(z, cda, sda_a, sda_b, DA_HEAD_DIM // 4) * da_scale).astype(BF16)
        z = ka_ref[:, sl].astype(F32)
        kat_o[h, 0] = _rope(z, cda, sda_a, sda_b, DA_HEAD_DIM // 4).T.astype(BF16)
    for h in range(GQ_HEADS):
        sl = slice(h * LANES, (h + 1) * LANES)
        z = _rms(qb_ref[:, sl].astype(F32), qn_ref[...], RMS_EPS)
        qb_o[:, sl] = (_rope(z, cgq, sgq_a, sgq_b, GQ_HEAD_DIM // 4) * gq_scale).astype(BF16)
    for h in range(GQ_KV_HEADS):
        sl = slice(h * LANES, (h + 1) * LANES)
        z = _rms(kb_ref[:, sl].astype(F32), kn_ref[...], RMS_EPS)
        kbt_o[h, 0] = _rope(z, cgq, sgq_a, sgq_b, GQ_HEAD_DIM // 4).T.astype(BF16)


def _prep_qk(proj, tab_da, tab_gq, qn_w, kn_w, layer, tk):
    r = proj.shape[0]
    nk = r // tk
    blk = lambda w, off: pl.BlockSpec((tk, w), lambda i: (i, off // w))
    out = lambda w: pl.BlockSpec((tk, w), lambda i: (i, 0))
    outt = lambda heads: pl.BlockSpec((heads, 1, LANES, tk), lambda i: (0, i, 0, 0))
    return pl.pallas_call(
        _prep_kernel,
        out_shape=(
            jax.ShapeDtypeStruct((r, DA_WIDTH), BF16),
            jax.ShapeDtypeStruct((DA_HEADS, nk, LANES, tk), BF16),
            jax.ShapeDtypeStruct((r, GQ_WIDTH), BF16),
            jax.ShapeDtypeStruct((GQ_KV_HEADS, nk, LANES, tk), BF16),
        ),
        grid=(nk,),
        in_specs=[
            blk(DA_WIDTH, OFF_QA), blk(DA_WIDTH, OFF_KA), blk(GQ_WIDTH, OFF_QB), blk(GQ_KV_WIDTH, OFF_KB),
            pl.BlockSpec((3, tk, LANES), lambda i: (0, i, 0)),
            pl.BlockSpec((3, tk, LANES), lambda i: (0, i, 0)),
            _layer_spec((1, LANES), layer),
            _layer_spec((1, LANES), layer),
        ],
        out_specs=(out(DA_WIDTH), outt(DA_HEADS), out(GQ_WIDTH), outt(GQ_KV_HEADS)),
        compiler_params=_cparams("parallel"),
        name="prep_qk",
    )(proj, proj, proj, proj, tab_da, tab_gq, qn_w, kn_w)


def _rope_tables(n_lat, n_ctx, dim):
    rows = n_lat // GRID_W
    row = jnp.broadcast_to(jnp.arange(rows, dtype=F32)[:, None], (rows, GRID_W)).reshape(-1)
    col = jnp.broadcast_to(jnp.arange(GRID_W, dtype=F32)[None, :], (rows, GRID_W)).reshape(-1)
    axis_dim = dim // 2
    inv_freq = ROPE_THETA ** (-jnp.arange(0, axis_dim, 2, dtype=F32) / axis_dim)
    ang_r = row[:, None] * inv_freq[None, :]
    ang_c = col[:, None] * inv_freq[None, :]
    ang = jnp.concatenate([ang_r, ang_r, ang_c, ang_c], axis=-1)
    cos, sin = jnp.cos(ang), jnp.sin(ang)
    first_half = (jnp.arange(dim) % (dim // 2)) < (dim // 4)
    sin_a = jnp.where(first_half[None, :], -sin, 0.0)
    sin_b = jnp.where(first_half[None, :], 0.0, sin)
    tab = jnp.stack([cos, sin_a, sin_b])
    ident = jnp.stack([jnp.ones((n_ctx, dim), F32), jnp.zeros((n_ctx, dim), F32), jnp.zeros((n_ctx, dim), F32)])
    tab = jnp.concatenate([tab, ident], axis=1)
    return jnp.tile(tab, (1, 1, LANES // dim))


def _flash_loop(qs, kt_ref, v_ref, vx_scr, tk, n_chunks):
    @pl.when(pl.program_id(1) == 0)
    def _():
        vx_scr[:, :LANES] = v_ref[...]
        vx_scr[:, LANES:] = jnp.ones((vx_scr.shape[0], LANES), vx_scr.dtype)

    tq = qs[0].shape[0]
    scores = lambda c: [jnp.dot(q, kt_ref[c], preferred_element_type=F32) for q in qs]
    m = [jnp.full((tq, 1), NEG_BIG, F32) for _ in qs]
    acc = [jnp.zeros((tq, 2 * LANES), F32) for _ in qs]
    s_next = scores(0)
    for c in range(n_chunks):
        s_cur = s_next
        if c + 1 < n_chunks:
            s_next = scores(c + 1)
        vx = vx_scr[c * tk:(c + 1) * tk, :]
        for i, s in enumerate(s_cur):
            m_new = jnp.maximum(m[i], jnp.max(s, axis=-1, keepdims=True))
            p = jnp.exp2((s - m_new).astype(BF16))
            acc[i] = jnp.exp2(m[i] - m_new) * acc[i] + jnp.dot(p, vx, preferred_element_type=F32)
            m[i] = m_new
    return [a[:, :LANES] / a[:, LANES:] for a in acc]


def _softmax_once(s, v):
    p = jnp.exp2(s - jnp.max(s, axis=-1, keepdims=True))
    return jnp.dot(p.astype(BF16), v, preferred_element_type=F32) / jnp.sum(p, axis=-1, keepdims=True)


def _da_split(q):
    lane = lax.broadcasted_iota(jnp.int32, (1, LANES), 1)
    zero = jnp.zeros_like(q)
    return jnp.where(lane < DA_HEAD_DIM, q, zero), jnp.where(lane < DA_HEAD_DIM, zero, q)


def _da_finish(o1, o2, lam_ref, w_ref, o_ref, lam_init):
    lp = lam_ref[...]
    lam = (jnp.exp(jnp.sum(lp[0:1] * lp[1:2], axis=-1, keepdims=True))
           - jnp.exp(jnp.sum(lp[2:3] * lp[3:4], axis=-1, keepdims=True)) + lam_init)
    o = _rms(o1 - lam * o2, w_ref[...], DA_SUBLN_EPS) * (1.0 - lam_init)
    o_ref[...] = o.astype(o_ref.dtype)


def _da_kernel(lam_ref, w_ref, q_ref, kt_ref, v_ref, o_ref, vx, *, tk, n_chunks, lam_init):
    o1, o2 = _flash_loop(_da_split(q_ref[...]), kt_ref, v_ref, vx, tk, n_chunks)
    _da_finish(o1, o2, lam_ref, w_ref, o_ref, lam_init)


def _da_ctx_kernel(lam_ref, w_ref, q_ref, kt_ref, v_ref, latent_out_ref, o_ref, *, off, n_ctx, lam_init):
    del latent_out_ref
    q1, q2 = _da_split(q_ref[...])
    kt = kt_ref[:, off:off + n_ctx]
    v = v_ref[...]
    o1 = _softmax_once(jnp.dot(q1, kt, preferred_element_type=F32), v)
    o2 = _softmax_once(jnp.dot(q2, kt, preferred_element_type=F32), v)
    _da_finish(o1, o2, lam_ref, w_ref, o_ref, lam_init)


def _gq_kernel(q_ref, kt_ref, v_ref, o_ref, vx, *, tk, n_chunks):
    (o,) = _flash_loop((q_ref[...],), kt_ref, v_ref, vx, tk, n_chunks)
    o_ref[...] = o.astype(o_ref.dtype)


def _gq_ctx_kernel(q_ref, kt_ref, v_ref, latent_out_ref, o_ref, *, off, n_ctx):
    del latent_out_ref
    kt = kt_ref[:, off:off + n_ctx]
    o_ref[...] = _softmax_once(jnp.dot(q_ref[...], kt, preferred_element_type=F32), v_ref[...]).astype(o_ref.dtype)


def _attention(params, layer, q, kt, proj, v_off, kv_group, n_lat, main_kernel, ctx_kernel, name):
    r, width = q.shape
    n_heads = width // LANES
    n_ctx = r - n_lat
    _, n_chunks, _, tk = kt.shape
    tq = _tile(n_lat, 512, 16)
    ctx_blk = n_lat // n_ctx
    off = n_lat - (n_chunks - 1) * tk
    param_specs = [_layer_spec(p.shape[1:], layer) for p in params]
    out = pl.pallas_call(
        functools.partial(main_kernel, tk=tk, n_chunks=n_chunks),
        out_shape=jax.ShapeDtypeStruct((r, width), BF16),
        grid=(n_heads, n_lat // tq),
        in_specs=param_specs + [
            pl.BlockSpec((tq, LANES), lambda h, i: (i, h)),
            pl.BlockSpec((None, n_chunks, LANES, tk), lambda h, i: (h // kv_group, 0, 0, 0)),
            pl.BlockSpec((r, LANES), lambda h, i: (0, v_off // LANES + h // kv_group)),
        ],
        out_specs=pl.BlockSpec((tq, LANES), lambda h, i: (i, h)),
        scratch_shapes=[pltpu.VMEM((r, 2 * LANES), BF16)],
        compiler_params=_cparams("parallel", "arbitrary"),
        name=name,
    )(*params, q, kt, proj)
    return pl.pallas_call(
        functools.partial(ctx_kernel, off=off, n_ctx=n_ctx),
        out_shape=jax.ShapeDtypeStruct((r, width), BF16),
        grid=(n_heads,),
        in_specs=param_specs + [
            pl.BlockSpec((n_ctx, LANES), lambda h: (ctx_blk, h)),
            pl.BlockSpec((None, None, LANES, tk), lambda h: (h // kv_group, n_chunks - 1, 0, 0)),
            pl.BlockSpec((n_ctx, LANES), lambda h: (ctx_blk, v_off // LANES + h // kv_group)),
            pl.BlockSpec(memory_space=pl.ANY),
        ],
        out_specs=pl.BlockSpec((n_ctx, LANES), lambda h: (ctx_blk, h)),
        input_output_aliases={len(params) + 3: 0},
        compiler_params=_cparams("parallel"),
        name=name + "_ctx",
    )(*params, q, kt, proj, out)


def _diff_attention(lam_params, subln_w, layer, qa, kat, proj, n_lat, lam_init):
    return _attention((lam_params, subln_w), layer, qa, kat, proj, OFF_VA, 1, n_lat,
                      functools.partial(_da_kernel, lam_init=lam_init),
                      functools.partial(_da_ctx_kernel, lam_init=lam_init), "diff_attention")


def _gq_attention(qb, kbt, proj, n_lat):
    return _attention((), 0, qb, kbt, proj, OFF_VB, GQ_GROUP, n_lat, _gq_kernel, _gq_ctx_kernel, "gq_attention")


def _s5_matrices(a_re, a_im, log_dt, b_re, b_im, c_re, c_im, d_skip):
    t = S5_CHUNK
    g, p, h = S5_GROUPS, S5_STATE, S5_GROUP
    a_re = jnp.minimum(a_re.astype(F32), S5_MAX_REAL)
    a_im = a_im.astype(F32)
    dt = jnp.exp(log_dt.astype(F32))[..., None]
    mag = jnp.exp(a_re * dt)
    lr = mag * jnp.cos(a_im * dt)
    li = mag * jnp.sin(a_im * dt)
    den = jnp.square(a_re) + jnp.square(a_im)
    nr = lr - 1.0
    cr = (nr * a_re + li * a_im) / den
    ci = (li * a_re - nr * a_im) / den
    bbr = cr[..., None] * b_re - ci[..., None] * b_im
    bbi = cr[..., None] * b_im + ci[..., None] * b_re
    j = jnp.arange(t + 1, dtype=F32)
    pmag = jnp.exp((a_re * dt)[..., None] * j)
    pr = pmag * jnp.cos((a_im * dt)[..., None] * j)
    pi = pmag * jnp.sin((a_im * dt)[..., None] * j)
    ct_re = c_re.astype(F32).transpose(0, 1, 3, 2)[:, :, :, None, :]
    ct_im = c_im.astype(F32).transpose(0, 1, 3, 2)[:, :, :, None, :]
    clr = ct_re * pr[..., None] - ct_im * pi[..., None]
    cli = ct_re * pi[..., None] + ct_im * pr[..., None]
    kern = (jnp.einsum('dgpjh,dgpk->dgkjh', clr[:, :, :, :t], bbr)
            - jnp.einsum('dgpjh,dgpk->dgkjh', cli[:, :, :, :t], bbi))
    eye = jnp.eye(h, dtype=F32)[None] * d_skip.astype(F32).reshape(g, h)[:, None, :]
    k0 = kern[0][:, :, 0] + kern[1][:, :, 0] + eye
    kk = jnp.concatenate([kern[1][:, :, :0:-1], k0[:, :, None], kern[0][:, :, 1:]], axis=2)
    kk = kk.reshape(g, h, (2 * t - 1) * h)
    m = jnp.stack([kk[:, :, (t - 1 - s) * h:(2 * t - 1 - s) * h] for s in range(t)], axis=1)
    m = m.reshape(g, t * h, t * h)

    def to_state(d, pw_r, pw_i):
        pw_r = pw_r.transpose(0, 2, 1)[:, :, None, :]
        pw_i = pw_i.transpose(0, 2, 1)[:, :, None, :]
        br = bbr[d].transpose(0, 2, 1)[:, None, :, :]
        bi = bbi[d].transpose(0, 2, 1)[:, None, :, :]
        return (pw_r * br - pw_i * bi).reshape(g, t * h, p), (pw_r * bi + pw_i * br).reshape(g, t * h, p)

    wf = to_state(0, pr[0, :, :, :t][..., ::-1], pi[0, :, :, :t][..., ::-1])
    wb = to_state(1, pr[1, :, :, :t], pi[1, :, :, :t])
    w = jnp.concatenate([wf[0], wf[1], wb[0], wb[1]], axis=-1)

    flat = lambda c: c.reshape(g, p, t * h)
    n = jnp.concatenate([flat(clr[0][:, :, 1:]), -flat(cli[0][:, :, 1:]),
                         flat(clr[1][:, :, :0:-1]), -flat(cli[1][:, :, :0:-1])],
                        axis=1)

    def coef(d):
        r_, i_ = pr[d, :, :, t], pi[d, :, :, t]
        return [jnp.concatenate([r_, r_], -1), jnp.concatenate([-i_, i_], -1), jnp.concatenate([i_, -i_], -1)]

    coefs = jnp.stack(coef(0) + coef(1))
    return m.astype(BF16), w.astype(BF16), n.astype(BF16), coefs


def _s5_lane_perm():
    i = np.arange(S5_BLOCK_GROUPS * S5_CW)
    tok, grp, ch = i // LANES, (i // S5_GROUP) % S5_BLOCK_GROUPS, i % S5_GROUP
    perm = np.zeros((i.size, i.size), np.float32)
    perm[i, grp * S5_CW + tok * S5_GROUP + ch] = 1.0
    return perm


def _s5_a_kernel(*refs):
    u_refs = refs[:S5_CHUNK]
    p_ref, m_ref, w_ref, y_ref, s_ref = refs[S5_CHUNK:]
    u_nat = jnp.concatenate([u[...] for u in u_refs], axis=1)
    ug = jnp.dot(u_nat, p_ref[...], preferred_element_type=F32).astype(BF16)
    for gl in range(S5_BLOCK_GROUPS):
        u_g = ug[:, gl * S5_CW:(gl + 1) * S5_CW]
        y_ref[gl] = jnp.dot(u_g, m_ref[gl], preferred_element_type=F32)
        s_ref[gl] = jnp.dot(u_g, w_ref[gl], preferred_element_type=F32)


def _s5_b_kernel(s_ref, coef_ref, x_ref, sw_scr, *, n_chunks, n_lat_chunks):
    half = 2 * S5_STATE
    n_ctx_chunks = n_chunks - n_lat_chunks
    fwd_sl, bwd_sl = slice(0, half), slice(half, 2 * half)

    def swap(c, carry):
        sw_scr[c, :, fwd_sl] = pltpu.roll(s_ref[c, :, fwd_sl], S5_STATE, 1)
        sw_scr[c, :, bwd_sl] = pltpu.roll(s_ref[c, :, bwd_sl], S5_STATE, 1)
        return carry

    lax.fori_loop(0, n_chunks, swap, 0, unroll=S5_SCAN_UNROLL)

    def step(sl, a, b, bt):
        def f(c, carry):
            x, xt = carry
            x_ref[c, :, sl] = x.astype(x_ref.dtype)
            return a * x + b * xt + s_ref[c, :, sl], a * xt + bt * x + sw_scr[c, :, sl]
        return f

    fwd = step(fwd_sl, coef_ref[0], coef_ref[1], coef_ref[2])
    bwd = step(bwd_sl, coef_ref[3], coef_ref[4], coef_ref[5])
    zero = jnp.zeros_like(coef_ref[0])
    loop = functools.partial(lax.fori_loop, unroll=S5_SCAN_UNROLL)
    carry = loop(0, n_ctx_chunks, lambda i, cr: fwd(n_lat_chunks + i, cr), (zero, zero))
    loop(0, n_lat_chunks, fwd, carry)
    carry = loop(0, n_ctx_chunks, lambda i, cr: bwd(n_chunks - 1 - i, cr), (zero, zero))
    loop(0, n_lat_chunks, lambda i, cr: bwd(n_lat_chunks - 1 - i, cr), carry)


def _s5_c_kernel(x_ref, n_ref, yi_ref, q_ref, y_ref):
    ys = [(yi_ref[gl] + jnp.dot(x_ref[gl], n_ref[gl], preferred_element_type=F32)).astype(BF16)
          for gl in range(S5_BLOCK_GROUPS)]
    y_ref[...] = jnp.dot(jnp.concatenate(ys, axis=1), q_ref[...], preferred_element_type=F32).astype(y_ref.dtype)


def _s5_mixer(u, m, w, ncat, coefs, layer, n_lat):
    r = u.shape[0]
    g, t, gb = S5_GROUPS, S5_CHUNK, S5_BLOCK_GROUPS
    nc = r // t
    sw = 4 * S5_STATE
    nblk = g // gb
    perm = _s5_lane_perm()
    p_in = jnp.asarray(perm, BF16)
    p_out = jnp.asarray(perm.T, BF16)
    u_rows = u.reshape(nc, t * S5_WIDTH)
    grp = lambda a, b: pl.BlockSpec((gb, a, b), lambda i: (i, 0, 0))
    wgrp = lambda a, b: pl.BlockSpec((None, gb, a, b), lambda i: (layer, i, 0, 0))
    perm_spec = pl.BlockSpec(perm.shape, lambda i: (0, 0), pipeline_mode=pl.Buffered(1))
    tok_specs = [pl.BlockSpec((nc, LANES), lambda i, s=s: (0, s * nblk + i)) for s in range(t)]
    y_intra, s = pl.pallas_call(
        _s5_a_kernel,
        out_shape=(jax.ShapeDtypeStruct((g, nc, S5_CW), F32), jax.ShapeDtypeStruct((g, nc, sw), F32)),
        grid=(nblk,),
        in_specs=tok_specs + [perm_spec, wgrp(S5_CW, S5_CW), wgrp(S5_CW, sw)],
        out_specs=(grp(nc, S5_CW), grp(nc, sw)),
        compiler_params=_cparams("parallel"),
        name="s5_chunk_local",
    )(*([u_rows] * t), p_in, m, w)

    gs = 16
    st = s.transpose(1, 0, 2)
    xt = pl.pallas_call(
        functools.partial(_s5_b_kernel, n_chunks=nc, n_lat_chunks=n_lat // t),
        out_shape=jax.ShapeDtypeStruct((nc, g, sw), BF16),
        grid=(g // gs,),
        in_specs=[pl.BlockSpec((nc, gs, sw), lambda i: (0, i, 0)),
                  pl.BlockSpec((None, 6, gs, 2 * S5_STATE), lambda i: (layer, 0, i, 0))],
        out_specs=pl.BlockSpec((nc, gs, sw), lambda i: (0, i, 0)),
        scratch_shapes=[pltpu.VMEM((nc, gs, sw), F32)],
        compiler_params=_cparams("parallel"),
        name="s5_chunk_scan",
    )(st, coefs)

    x_in = xt.transpose(1, 0, 2)
    y_blocks = pl.pallas_call(
        _s5_c_kernel,
        out_shape=jax.ShapeDtypeStruct((nblk, nc, gb * S5_CW), BF16),
        grid=(nblk,),
        in_specs=[grp(nc, sw), wgrp(sw, S5_CW), grp(nc, S5_CW), perm_spec],
        out_specs=pl.BlockSpec((None, nc, gb * S5_CW), lambda i: (i, 0, 0)),
        compiler_params=_cparams("parallel"),
        name="s5_state_out",
    )(x_in, ncat, y_intra, p_out)
    return y_blocks.reshape(nblk, nc, t, LANES).transpose(1, 2, 0, 3).reshape(r, S5_WIDTH)


def _merge_kernel(oa_ref, ob_ref, y_ref, wpa_ref, wpb_ref, wga_ref, wgg_ref, g0_ref, g1_ref, g2_ref,
                  o_ref, gy_scr):
    @pl.when(pl.program_id(1) == 0)
    def _():
        gy_scr[...] = _gelu_tanh(y_ref[...].astype(F32)).astype(BF16)

    dot = lambda a, b: jnp.dot(a, b, preferred_element_type=F32)
    pa = dot(oa_ref[...], wpa_ref[...])
    pb = dot(ob_ref[...], wpb_ref[...])
    gy = gy_scr[...]
    pc = dot(gy, wga_ref[...]) * _sigmoid(dot(gy, wgg_ref[...]))
    sg = lambda ref: _sigmoid(ref[...].astype(F32))
    o_ref[...] = (sg(g0_ref) * pa + sg(g1_ref) * pb + sg(g2_ref) * pc).astype(o_ref.dtype)


def _merge(o_a, o_b, y_s5, proj, w_pa, w_pb, w_glu, layer, d):
    r = o_a.shape[0]
    tm = _tile(r, 1056, 16)
    tn = _tile(math.gcd(d, OFF_GATE), 512, LANES)
    row = lambda w: pl.BlockSpec((tm, w), lambda i, j: (i, 0))
    col = lambda k, off: pl.BlockSpec((None, k, tn), lambda i, j: (layer, 0, off // tn + j))
    gate = lambda b: pl.BlockSpec((tm, tn), lambda i, j: (i, (OFF_GATE + b * d) // tn + j))
    return pl.pallas_call(
        _merge_kernel,
        out_shape=jax.ShapeDtypeStruct((r, d), BF16),
        grid=(r // tm, d // tn),
        in_specs=[row(DA_WIDTH), row(GQ_WIDTH), row(S5_WIDTH),
                  col(DA_WIDTH, 0), col(GQ_WIDTH, 0), col(S5_WIDTH, 0), col(S5_WIDTH, d),
                  gate(0), gate(1), gate(2)],
        out_specs=pl.BlockSpec((tm, tn), lambda i, j: (i, j)),
        scratch_shapes=[pltpu.VMEM((tm, S5_WIDTH), BF16)],
        compiler_params=_cparams("parallel", "arbitrary"),
        name="branch_merge",
    )(o_a, o_b, y_s5, w_pa, w_pb, w_glu, w_glu, proj, proj, proj)


def _out_ln_kernel(m_ref, w_ref, x_ref, gm_ref, lg_ref, lb_ref, o_ref, *, tm, n_lat, alpha):
    y = jnp.dot(m_ref[...], w_ref[...], preferred_element_type=F32)
    gate = _row_select(gm_ref, pl.program_id(0) * tm, tm, n_lat)
    o_ref[...] = _layer_norm(alpha * x_ref[...] + gate * y, lg_ref[...], lb_ref[...])


def _out_ln(merged, w_o, layer, xs, mods, ln_g, ln_b, n_lat, alpha):
    r, d = xs.shape
    tm = _tile(r, 528, 16)
    return pl.pallas_call(
        functools.partial(_out_ln_kernel, tm=tm, n_lat=n_lat, alpha=alpha),
        out_shape=jax.ShapeDtypeStruct((r, d), F32),
        grid=(r // tm,),
        in_specs=[
            pl.BlockSpec((tm, d), lambda i: (i, 0)),
            pl.BlockSpec((None, d, d), lambda i: (layer, 0, 0)),
            pl.BlockSpec((tm, d), lambda i: (i, 0)),
            _layer_spec((8, d), layer, 2),
            _layer_spec((1, d), layer), _layer_spec((1, d), layer),
        ],
        out_specs=pl.BlockSpec((tm, d), lambda i: (i, 0)),
        compiler_params=_cparams("parallel"),
        name="out_proj_ln",
    )(merged, w_o, xs, mods, ln_g, ln_b)


def _ffn_kernel(x_ref, sh_ref, sc_ref, gf_ref, wg_ref, wu_ref, wd_ref, lg_ref, lb_ref, o_ref,
                h_scr, *, tm, n_lat, alpha):
    i = pl.program_id(0)
    f = pl.program_id(1)

    @pl.when(f == 0)
    def _():
        sh = _row_select(sh_ref, i * tm, tm, n_lat)
        sc = _row_select(sc_ref, i * tm, tm, n_lat)
        h_scr[...] = (x_ref[...] * (1.0 + sc) + sh).astype(BF16)
        o_ref[...] = jnp.zeros_like(o_ref)

    h = h_scr[...]
    gate = jnp.dot(h, wg_ref[...].astype(BF16), preferred_element_type=F32)
    up = jnp.dot(h, wu_ref[...].astype(BF16), preferred_element_type=F32)
    act = (gate * _sigmoid(gate) * up).astype(BF16)
    o_ref[...] += jnp.dot(act, wd_ref[...].astype(BF16), preferred_element_type=F32)

    @pl.when(f == pl.num_programs(1) - 1)
    def _():
        g = _row_select(gf_ref, i * tm, tm, n_lat)
        o_ref[...] = _layer_norm(alpha * x_ref[...] + g * o_ref[...], lg_ref[...], lb_ref[...])


def _ffn(xs, r, mods, w_gate, w_up, w_down, layer, ln_g, ln_b, n_lat, alpha):
    d = xs.shape[1]
    f = w_gate.shape[2]
    tm = _tile(r, 1056, 16)
    tf = _tile(f, MXU_DIM, LANES)
    modv = lambda c: _layer_spec((8, d), layer, c)
    return pl.pallas_call(
        functools.partial(_ffn_kernel, tm=tm, n_lat=n_lat, alpha=alpha),
        out_shape=jax.ShapeDtypeStruct((r, d), F32),
        grid=(r // tm, f // tf),
        in_specs=[
            pl.BlockSpec((tm, d), lambda i, j: (i, 0), pipeline_mode=pl.Buffered(1)),
            modv(3), modv(4), modv(5),
            pl.BlockSpec((None, d, tf), lambda i, j: (layer, 0, j)),
            pl.BlockSpec((None, d, tf), lambda i, j: (layer, 0, j)),
            pl.BlockSpec((None, tf, d), lambda i, j: (layer, j, 0)),
            _layer_spec((1, d), layer), _layer_spec((1, d), layer),
        ],
        out_specs=pl.BlockSpec((tm, d), lambda i, j: (i, 0)),
        scratch_shapes=[pltpu.VMEM((tm, d), BF16)],
        compiler_params=_cparams("parallel", "arbitrary"),
        name="swiglu_ln",
    )(xs, mods, mods, mods, w_gate, w_up, w_down, ln_g, ln_b)


def kernel(x, c, ctx, c_ctx, w_ada, b_ada, w_in, da_lam_q1, da_lam_k1, da_lam_q2, da_lam_k2, da_subln_w, w_pa, gq_qnorm_w, gq_knorm_w, w_pb, s5_a_re, s5_a_im, s5_log_dt, s5_b_re, s5_b_im, s5_c_re, s5_c_im, s5_d, w_glu, w_o, ln_mix_g, ln_mix_b, w_ffn_gate, w_ffn_up, w_ffn_down, ln_ffn_g, ln_ffn_b):
    batch, n_lat, d = x.shape
    n_ctx = ctx.shape[1]
    depth = w_in.shape[0]
    r = n_lat + n_ctx
    assert batch == 1 and n_lat % GRID_W == 0 and n_ctx % S5_CHUNK == 0 and n_lat % S5_CHUNK == 0
    assert n_lat % n_ctx == 0
    alpha = (2 * depth) ** 0.25
    tk = _tile(r, 768, MXU_DIM)
    assert n_ctx <= tk and (n_lat - (r // tk - 1) * tk) % LANES == 0

    xs = jnp.concatenate([x[0], ctx[0]], axis=0).astype(F32)
    c_rows = jnp.zeros((8, d), F32).at[0].set(c[0]).at[1].set(c_ctx)
    mods = _ada_mod(c_rows, w_ada, b_ada)
    tab_da = _rope_tables(n_lat, n_ctx, DA_HEAD_DIM)
    tab_gq = _rope_tables(n_lat, n_ctx, GQ_HEAD_DIM)

    w_pa, w_pb, w_glu, w_o = (w.astype(BF16) for w in (w_pa, w_pb, w_glu, w_o))
    s5_m, s5_w, s5_n, s5_coefs = jax.vmap(_s5_matrices)(
        s5_a_re, s5_a_im, s5_log_dt, s5_b_re, s5_b_im, s5_c_re, s5_c_im, s5_d)

    row = lambda v: v.astype(F32)[:, None, :]
    lam_params = jnp.stack([da_lam_q1, da_lam_k1, da_lam_q2, da_lam_k2], axis=1).astype(F32)
    subln_w, qn_w, kn_w = row(da_subln_w), row(gq_qnorm_w), row(gq_knorm_w)
    ln_mix_g, ln_mix_b, ln_ffn_g, ln_ffn_b = row(ln_mix_g), row(ln_mix_b), row(ln_ffn_g), row(ln_ffn_b)

    for l in range(depth):
        lam_init = 0.8 - 0.6 * math.exp(-0.3 * l)
        proj = _in_proj(xs, mods, w_in, l, n_lat)
        qa, kat, qb, kbt = _prep_qk(proj, tab_da, tab_gq, qn_w, kn_w, l, tk)
        o_a = _diff_attention(lam_params, subln_w, l, qa, kat, proj, n_lat, lam_init)
        o_b = _gq_attention(qb, kbt, proj, n_lat)
        y_s5 = _s5_mixer(proj[:, OFF_U:OFF_U + S5_WIDTH], s5_m, s5_w, s5_n, s5_coefs, l, n_lat)
        merged = _merge(o_a, o_b, y_s5, proj, w_pa, w_pb, w_glu, l, d)
        xs = _out_ln(merged, w_o, l, xs, mods, ln_mix_g, ln_mix_b, n_lat, alpha)
        rows_out = n_lat if l == depth - 1 else r
        xs = _ffn(xs, rows_out, mods, w_ffn_gate, w_ffn_up, w_ffn_down, l, ln_ffn_g, ln_ffn_b, n_lat, alpha)
    return xs[None].astype(x.dtype)
```

```python
import functools
import math

import jax
import jax.numpy as jnp
import numpy as np
from jax import lax
from jax.experimental import pallas as pl
from jax.experimental.pallas import tpu as pltpu

F32 = jnp.float32
BF16 = jnp.bfloat16

GRID_W = 64
DA_HEADS = 8
DA_HEAD_DIM = 64
DA_WIDTH = DA_HEADS * 2 * DA_HEAD_DIM
GQ_HEADS = 8
GQ_KV_HEADS = 2
GQ_GROUP = GQ_HEADS // GQ_KV_HEADS
GQ_HEAD_DIM = 128
GQ_WIDTH = GQ_HEADS * GQ_HEAD_DIM
GQ_KV_WIDTH = GQ_KV_HEADS * GQ_HEAD_DIM
S5_WIDTH = 1024
S5_GROUP = 16
S5_GROUPS = S5_WIDTH // S5_GROUP
S5_STATE = 64
S5_MAX_REAL = -1e-4
N_BRANCH = 3
ROPE_THETA = 10000.0
LN_EPS = 1e-6
RMS_EPS = 1e-6
DA_SUBLN_EPS = 1e-5

OFF_QA = 0
OFF_KA = OFF_QA + DA_WIDTH
OFF_VA = OFF_KA + DA_WIDTH
OFF_QB = OFF_VA + DA_WIDTH
OFF_KB = OFF_QB + GQ_WIDTH
OFF_VB = OFF_KB + GQ_KV_WIDTH
OFF_U = OFF_VB + GQ_KV_WIDTH
OFF_GATE = OFF_U + S5_WIDTH

LANES = 128
MXU_DIM = 256
V7X_VMEM_BYTES = 64 * 1024 * 1024
VMEM_LIMIT = V7X_VMEM_BYTES - 8 * 1024 * 1024

S5_CHUNK = 16
S5_CW = S5_CHUNK * S5_GROUP
S5_BLOCK_GROUPS = LANES // S5_GROUP
S5_SCAN_UNROLL = 8

LOG2E = math.log2(math.e)
NEG_BIG = -1e30


def _cparams(*sem):
    return pltpu.CompilerParams(dimension_semantics=sem, vmem_limit_bytes=VMEM_LIMIT)


def _tile(n, target, mult):
    best = None
    for t in range(mult, min(n, target) + 1, mult):
        if n % t == 0:
            best = t
    assert best is not None, (n, target, mult)
    return best


def _layer_spec(block_shape, layer, col=0):
    return pl.BlockSpec((None,) + tuple(block_shape), lambda *_: (layer, 0, col))


def _sigmoid(x):
    return 1.0 / (1.0 + jnp.exp(-x))


def _gelu_tanh(x):
    return 0.5 * x * (1.0 + jnp.tanh(math.sqrt(2.0 / math.pi) * (x + 0.044715 * (x * x * x))))


def _row_select(vec_ref, row0, tm, n_lat):
    rows = row0 + lax.broadcasted_iota(jnp.int32, (tm, 1), 0)
    return jnp.where(rows < n_lat, vec_ref[0:1, :], vec_ref[1:2, :])


def _layer_norm(z, g, b):
    mu = jnp.mean(z, axis=-1, keepdims=True)
    zc = z - mu
    var = jnp.mean(zc * zc, axis=-1, keepdims=True)
    return zc * lax.rsqrt(var + LN_EPS) * g + b


def _mod_kernel(c_ref, w_ref, b_ref, o_ref):
    a = c_ref[...]
    a = a * _sigmoid(a)
    o_ref[0] = jnp.dot(a.astype(BF16), w_ref[0].astype(BF16), preferred_element_type=F32) + b_ref[0]


def _ada_mod(c_rows, w_ada, b_ada):
    depth, d, n6 = w_ada.shape
    tn = _tile(n6, 1024, LANES)
    return pl.pallas_call(
        _mod_kernel,
        out_shape=jax.ShapeDtypeStruct((depth, 8, n6), F32),
        grid=(depth, n6 // tn),
        in_specs=[
            pl.BlockSpec((8, d), lambda l, j: (0, 0)),
            pl.BlockSpec((1, d, tn), lambda l, j: (l, 0, j)),
            pl.BlockSpec((1, 1, tn), lambda l, j: (l, 0, j)),
        ],
        out_specs=pl.BlockSpec((1, 8, tn), lambda l, j: (l, 0, j)),
        compiler_params=_cparams("parallel", "parallel"),
        name="ada_mod",
    )(c_rows, w_ada, b_ada.reshape(depth, 1, n6))


def _in_proj_kernel(x_ref, sh_ref, sc_ref, w_ref, o_ref, h_scr, *, tm, n_lat):
    i = pl.program_id(0)

    @pl.when(pl.program_id(1) == 0)
    def _():
        sh = _row_select(sh_ref, i * tm, tm, n_lat)
        sc = _row_select(sc_ref, i * tm, tm, n_lat)
        h_scr[...] = (x_ref[...] * (1.0 + sc) + sh).astype(BF16)

    o_ref[...] = jnp.dot(h_scr[...], w_ref[...].astype(BF16), preferred_element_type=F32).astype(o_ref.dtype)


def _in_proj(xs, mods, w_in, layer, n_lat):
    r, d = xs.shape
    n = w_in.shape[2]
    tm = _tile(r, 2112, 16)
    tn = _tile(n, 512, LANES)
    return pl.pallas_call(
        functools.partial(_in_proj_kernel, tm=tm, n_lat=n_lat),
        out_shape=jax.ShapeDtypeStruct((r, n), BF16),
        grid=(r // tm, n // tn),
        in_specs=[
            pl.BlockSpec((tm, d), lambda i, j: (i, 0), pipeline_mode=pl.Buffered(1)),
            _layer_spec((8, d), layer, 0),
            _layer_spec((8, d), layer, 1),
            pl.BlockSpec((None, d, tn), lambda i, j: (layer, 0, j)),
        ],
        out_specs=pl.BlockSpec((tm, tn), lambda i, j: (i, j)),
        scratch_shapes=[pltpu.VMEM((tm, d), BF16)],
        compiler_params=_cparams("parallel", "arbitrary"),
        name="in_proj",
    )(xs, mods, mods, w_in)


def _rope(z, cos, sin_a, sin_b, quarter):
    return z * cos + pltpu.roll(z, LANES - quarter, 1) * sin_a + pltpu.roll(z, quarter, 1) * sin_b


def _rms(z, w, eps):
    return z * lax.rsqrt(jnp.mean(z * z, axis=-1, keepdims=True) + eps) * w


def _prep_kernel(qa_ref, ka_ref, qb_ref, kb_ref, tda_ref, tgq_ref, qn_ref, kn_ref,
                 qa_o, kat_o, qb_o, kbt_o):
    cda, sda_a, sda_b = tda_ref[0], tda_ref[1], tda_ref[2]
    cgq, sgq_a, sgq_b = tgq_ref[0], tgq_ref[1], tgq_ref[2]
    da_scale = DA_HEAD_DIM ** -0.5 * LOG2E
    gq_scale = GQ_HEAD_DIM ** -0.5 * LOG2E
    for h in range(DA_HEADS):
        sl = slice(h * LANES, (h + 1) * LANES)
        z = qa_ref[:, sl].astype(F32)
        qa_o[:, sl] = (_rope(z, cda, sda_a, sda_b, DA_HEAD_DIM // 4) * da_scale).astype(BF16)
        z = ka_ref[:, sl].astype(F32)
        kat_o[h, 0] = _rope(z, cda, sda_a, sda_b, DA_HEAD_DIM // 4).T.astype(BF16)
    for h in range(GQ_HEADS):
        sl = slice(h * LANES, (h + 1) * LANES)
        z = _rms(qb_ref[:, sl].astype(F32), qn_ref[...], RMS_EPS)
        qb_o[:, sl] = (_rope(z, cgq, sgq_a, sgq_b, GQ_HEAD_DIM // 4) * gq_scale).astype(BF16)
    for h in range(GQ_KV_HEADS):
        sl = slice(h * LANES, (h + 1) * LANES)
        z = _rms(kb_ref[:, sl].astype(F32), kn_ref[...], RMS_EPS)
        kbt_o[h, 0] = _rope(z, cgq, sgq_a, sgq_b, GQ_HEAD_DIM // 4).T.astype(BF16)


def _prep_qk(proj, tab_da, tab_gq, qn_w, kn_w, layer, tk):
    r = proj.shape[0]
    nk = r // tk
    blk = lambda w, off: pl.BlockSpec((tk, w), lambda i: (i, off // w))
    out = lambda w: pl.BlockSpec((tk, w), lambda i: (i, 0))
    outt = lambda heads: pl.BlockSpec((heads, 1, LANES, tk), lambda i: (0, i, 0, 0))
    return pl.pallas_call(
        _prep_kernel,
        out_shape=(
            jax.ShapeDtypeStruct((r, DA_WIDTH), BF16),
            jax.ShapeDtypeStruct((DA_HEADS, nk, LANES, tk), BF16),
            jax.ShapeDtypeStruct((r, GQ_WIDTH), BF16),
            jax.ShapeDtypeStruct((GQ_KV_HEADS, nk, LANES, tk), BF16),
        ),
        grid=(nk,),
        in_specs=[
            blk(DA_WIDTH, OFF_QA), blk(DA_WIDTH, OFF_KA), blk(GQ_WIDTH, OFF_QB), blk(GQ_KV_WIDTH, OFF_KB),
            pl.BlockSpec((3, tk, LANES), lambda i: (0, i, 0)),
            pl.BlockSpec((3, tk, LANES), lambda i: (0, i, 0)),
            _layer_spec((1, LANES), layer),
            _layer_spec((1, LANES), layer),
        ],
        out_specs=(out(DA_WIDTH), outt(DA_HEADS), out(GQ_WIDTH), outt(GQ_KV_HEADS)),
        compiler_params=_cparams("parallel"),
        name="prep_qk",
    )(proj, proj, proj, proj, tab_da, tab_gq, qn_w, kn_w)


def _rope_tables(n_lat, n_ctx, dim):
    rows = n_lat // GRID_W
    row = np.repeat(np.arange(rows, dtype=np.float64), GRID_W)
    col = np.tile(np.arange(GRID_W, dtype=np.float64), rows)
    axis_dim = dim // 2
    inv_freq = ROPE_THETA ** (-np.arange(0, axis_dim, 2, dtype=np.float64) / axis_dim)
    ang_r = row[:, None] * inv_freq[None, :]
    ang_c = col[:, None] * inv_freq[None, :]
    ang = np.concatenate([ang_r, ang_r, ang_c, ang_c], axis=-1)
    cos, sin = np.cos(ang), np.sin(ang)
    first_half = (np.arange(dim) % (dim // 2)) < (dim // 4)
    sin_a = np.where(first_half[None, :], -sin, 0.0)
    sin_b = np.where(first_half[None, :], 0.0, sin)
    tab = np.stack([cos, sin_a, sin_b])
    ident = np.stack([np.ones((n_ctx, dim)), np.zeros((n_ctx, dim)), np.zeros((n_ctx, dim))])
    tab = np.concatenate([tab, ident], axis=1)
    return jnp.asarray(np.tile(tab, (1, 1, LANES // dim)), F32)


def _flash_loop(qs, kt_ref, v_ref, vx_scr, tk, n_chunks):
    @pl.when(pl.program_id(1) == 0)
    def _():
        vx_scr[:, :LANES] = v_ref[...]
        vx_scr[:, LANES:] = jnp.ones((vx_scr.shape[0], LANES), vx_scr.dtype)

    tq = qs[0].shape[0]
    scores = lambda c: [jnp.dot(q, kt_ref[c], preferred_element_type=F32) for q in qs]
    m = [jnp.full((tq, 1), NEG_BIG, F32) for _ in qs]
    acc = [jnp.zeros((tq, 2 * LANES), F32) for _ in qs]
    s_next = scores(0)
    for c in range(n_chunks):
        s_cur = s_next
        if c + 1 < n_chunks:
            s_next = scores(c + 1)
        vx = vx_scr[c * tk:(c + 1) * tk, :]
        for i, s in enumerate(s_cur):
            m_new = jnp.maximum(m[i], jnp.max(s, axis=-1, keepdims=True))
            p = jnp.exp2((s - m_new).astype(BF16))
            acc[i] = jnp.exp2(m[i] - m_new) * acc[i] + jnp.dot(p, vx, preferred_element_type=F32)
            m[i] = m_new
    return [a[:, :LANES] / a[:, LANES:] for a in acc]


def _softmax_once(s, v):
    p = jnp.exp2(s - jnp.max(s, axis=-1, keepdims=True))
    return jnp.dot(p.astype(BF16), v, preferred_element_type=F32) / jnp.sum(p, axis=-1, keepdims=True)


def _da_split(q):
    lane = lax.broadcasted_iota(jnp.int32, (1, LANES), 1)
    zero = jnp.zeros_like(q)
    return jnp.where(lane < DA_HEAD_DIM, q, zero), jnp.where(lane < DA_HEAD_DIM, zero, q)


def _da_finish(o1, o2, lam_ref, w_ref, o_ref, lam_init):
    lp = lam_ref[...]
    lam = (jnp.exp(jnp.sum(lp[0:1] * lp[1:2], axis=-1, keepdims=True))
           - jnp.exp(jnp.sum(lp[2:3] * lp[3:4], axis=-1, keepdims=True)) + lam_init)
    o = _rms(o1 - lam * o2, w_ref[...], DA_SUBLN_EPS) * (1.0 - lam_init)
    o_ref[...] = o.astype(o_ref.dtype)


def _da_kernel(lam_ref, w_ref, q_ref, kt_ref, v_ref, o_ref, vx, *, tk, n_chunks, lam_init):
    o1, o2 = _flash_loop(_da_split(q_ref[...]), kt_ref, v_ref, vx, tk, n_chunks)
    _da_finish(o1, o2, lam_ref, w_ref, o_ref, lam_init)


def _da_ctx_kernel(lam_ref, w_ref, q_ref, kt_ref, v_ref, latent_out_ref, o_ref, *, off, n_ctx, lam_init):
    del latent_out_ref
    q1, q2 = _da_split(q_ref[...])
    kt = kt_ref[:, off:off + n_ctx]
    v = v_ref[...]
    o1 = _softmax_once(jnp.dot(q1, kt, preferred_element_type=F32), v)
    o2 = _softmax_once(jnp.dot(q2, kt, preferred_element_type=F32), v)
    _da_finish(o1, o2, lam_ref, w_ref, o_ref, lam_init)


def _gq_kernel(q_ref, kt_ref, v_ref, o_ref, vx, *, tk, n_chunks):
    (o,) = _flash_loop((q_ref[...],), kt_ref, v_ref, vx, tk, n_chunks)
    o_ref[...] = o.astype(o_ref.dtype)


def _gq_ctx_kernel(q_ref, kt_ref, v_ref, latent_out_ref, o_ref, *, off, n_ctx):
    del latent_out_ref
    kt = kt_ref[:, off:off + n_ctx]
    o_ref[...] = _softmax_once(jnp.dot(q_ref[...], kt, preferred_element_type=F32), v_ref[...]).astype(o_ref.dtype)


def _attention(params, layer, q, kt, proj, v_off, kv_group, n_lat, main_kernel, ctx_kernel, name):
    r, width = q.shape
    n_heads = width // LANES
    n_ctx = r - n_lat
    _, n_chunks, _, tk = kt.shape
    tq = _tile(n_lat, 512, 16)
    ctx_blk = n_lat // n_ctx
    off = n_lat - (n_chunks - 1) * tk
    param_specs = [_layer_spec(p.shape[1:], layer) for p in params]
    out = pl.pallas_call(
        functools.partial(main_kernel, tk=tk, n_chunks=n_chunks),
        out_shape=jax.ShapeDtypeStruct((r, width), BF16),
        grid=(n_heads, n_lat // tq),
        in_specs=param_specs + [
            pl.BlockSpec((tq, LANES), lambda h, i: (i, h)),
            pl.BlockSpec((None, n_chunks, LANES, tk), lambda h, i: (h // kv_group, 0, 0, 0)),
            pl.BlockSpec((r, LANES), lambda h, i: (0, v_off // LANES + h // kv_group)),
        ],
        out_specs=pl.BlockSpec((tq, LANES), lambda h, i: (i, h)),
        scratch_shapes=[pltpu.VMEM((r, 2 * LANES), BF16)],
        compiler_params=_cparams("parallel", "arbitrary"),
        name=name,
    )(*params, q, kt, proj)
    return pl.pallas_call(
        functools.partial(ctx_kernel, off=off, n_ctx=n_ctx),
        out_shape=jax.ShapeDtypeStruct((r, width), BF16),
        grid=(n_heads,),
        in_specs=param_specs + [
            pl.BlockSpec((n_ctx, LANES), lambda h: (ctx_blk, h)),
            pl.BlockSpec((None, None, LANES, tk), lambda h: (h // kv_group, n_chunks - 1, 0, 0)),
            pl.BlockSpec((n_ctx, LANES), lambda h: (ctx_blk, v_off // LANES + h // kv_group)),
            pl.BlockSpec(memory_space=pl.ANY),
        ],
        out_specs=pl.BlockSpec((n_ctx, LANES), lambda h: (ctx_blk, h)),
        input_output_aliases={len(params) + 3: 0},
        compiler_params=_cparams("parallel"),
        name=name + "_ctx",
    )(*params, q, kt, proj, out)


def _diff_attention(lam_params, subln_w, layer, qa, kat, proj, n_lat, lam_init):
    return _attention((lam_params, subln_w), layer, qa, kat, proj, OFF_VA, 1, n_lat,
                      functools.partial(_da_kernel, lam_init=lam_init),
                      functools.partial(_da_ctx_kernel, lam_init=lam_init), "diff_attention")


def _gq_attention(qb, kbt, proj, n_lat):
    return _attention((), 0, qb, kbt, proj, OFF_VB, GQ_GROUP, n_lat, _gq_kernel, _gq_ctx_kernel, "gq_attention")


def _s5_matrices(a_re, a_im, log_dt, b_re, b_im, c_re, c_im, d_skip):
    t = S5_CHUNK
    g, p, h = S5_GROUPS, S5_STATE, S5_GROUP
    a_re = jnp.minimum(a_re.astype(F32), S5_MAX_REAL)
    a_im = a_im.astype(F32)
    dt = jnp.exp(log_dt.astype(F32))[..., None]
    mag = jnp.exp(a_re * dt)
    lr = mag * jnp.cos(a_im * dt)
    li = mag * jnp.sin(a_im * dt)
    den = jnp.square(a_re) + jnp.square(a_im)
    nr = lr - 1.0
    cr = (nr * a_re + li * a_im) / den
    ci = (li * a_re - nr * a_im) / den
    bbr = cr[..., None] * b_re - ci[..., None] * b_im
    bbi = cr[..., None] * b_im + ci[..., None] * b_re
    j = jnp.arange(t + 1, dtype=F32)
    pmag = jnp.exp((a_re * dt)[..., None] * j)
    pr = pmag * jnp.cos((a_im * dt)[..., None] * j)
    pi = pmag * jnp.sin((a_im * dt)[..., None] * j)
    per_step = lambda x: jnp.repeat(x, h, axis=-1)
    per_chan = lambda x: jnp.tile(x, (1,) * (x.ndim - 1) + (t,))
    pr_j, pi_j = per_step(pr[..., :t]), per_step(pi[..., :t])
    bb_r, bb_i = per_chan(bbr), per_chan(bbi)
    y_re = pr_j * bb_r - pi_j * bb_i
    y_im = pr_j * bb_i + pi_j * bb_r
    dot = functools.partial(jnp.einsum, 'dghp,dgpn->dghn', precision=lax.Precision.HIGHEST)
    kern = dot(c_re.astype(F32), y_re) - dot(c_im.astype(F32), y_im)
    kern = kern.reshape(2, g, h, t, h).transpose(0, 1, 4, 3, 2)
    eye =jnp.eye(h, dtype=F32)[None] * d_skip.astype(F32).reshape(g, h)[:, None, :]
    k0 = kern[0][:, :, 0] + kern[1][:, :, 0] + eye
    kk = jnp.concatenate([kern[1][:, :, :0:-1], k0[:, :, None], kern[0][:, :, 1:]], axis=2)
    kk = kk.reshape(g, h, (2 * t - 1) * h)
    m = jnp.stack([kk[:, :, (t - 1 - s) * h:(2 * t - 1 - s) * h] for s in range(t)], axis=1)
    m = m.reshape(g, t * h, t * h)

    def to_state(y, flip):
        y = y.reshape(g, p, t, h)
        y = y[:, :, ::-1] if flip else y
        return y.reshape(g, p, t * h).transpose(0, 2, 1)

    w = jnp.concatenate([to_state(y_re[0], True), to_state(y_im[0], True),
                         to_state(y_re[1], False), to_state(y_im[1], False)], axis=-1)

    ct_re = per_chan(c_re.astype(F32).transpose(0, 1, 3, 2))
    ct_im = per_chan(c_im.astype(F32).transpose(0, 1, 3, 2))
    pf_r, pf_i = per_step(pr[0][..., 1:]), per_step(pi[0][..., 1:])
    pb_r, pb_i = per_step(pr[1][..., :0:-1]), per_step(pi[1][..., :0:-1])
    n = jnp.concatenate([ct_re[0] * pf_r - ct_im[0] * pf_i, -(ct_re[0] * pf_i + ct_im[0] * pf_r),
                         ct_re[1] * pb_r - ct_im[1] * pb_i, -(ct_re[1] * pb_i + ct_im[1] * pb_r)],
                        axis=1)

    def coef(d):
        r_, i_ = pr[d, :, :, t], pi[d, :, :, t]
        return [jnp.concatenate([r_, r_], -1), jnp.concatenate([-i_, i_], -1), jnp.concatenate([i_, -i_], -1)]

    coefs = jnp.stack(coef(0) + coef(1))
    return m.astype(BF16), w.astype(BF16), n.astype(BF16), coefs


def _s5_lane_perm():
    i = np.arange(S5_BLOCK_GROUPS * S5_CW)
    tok, grp, ch = i // LANES, (i // S5_GROUP) % S5_BLOCK_GROUPS, i % S5_GROUP
    perm = np.zeros((i.size, i.size), np.float32)
    perm[i, grp * S5_CW + tok * S5_GROUP + ch] = 1.0
    return perm


def _s5_a_kernel(*refs):
    u_refs = refs[:S5_CHUNK]
    p_ref, m_ref, w_ref, y_ref, s_ref = refs[S5_CHUNK:]
    u_nat = jnp.concatenate([u[...] for u in u_refs], axis=1)
    ug = jnp.dot(u_nat, p_ref[...], preferred_element_type=F32).astype(BF16)
    for gl in range(S5_BLOCK_GROUPS):
        u_g = ug[:, gl * S5_CW:(gl + 1) * S5_CW]
        y_ref[gl] = jnp.dot(u_g, m_ref[gl], preferred_element_type=F32)
        s_ref[gl] = jnp.dot(u_g, w_ref[gl], preferred_element_type=F32)


def _s5_b_kernel(s_ref, coef_ref, x_ref, sw_scr, *, n_chunks, n_lat_chunks):
    half = 2 * S5_STATE
    n_ctx_chunks = n_chunks - n_lat_chunks
    fwd_sl, bwd_sl = slice(0, half), slice(half, 2 * half)

    def swap(c, carry):
        sw_scr[c, :, fwd_sl] = pltpu.roll(s_ref[c, :, fwd_sl], S5_STATE, 1)
        sw_scr[c, :, bwd_sl] = pltpu.roll(s_ref[c, :, bwd_sl], S5_STATE, 1)
        return carry

    lax.fori_loop(0, n_chunks, swap, 0, unroll=S5_SCAN_UNROLL)

    def step(sl, a, b, bt):
        def f(c, carry):
            x, xt = carry
            x_ref[c, :, sl] = x.astype(x_ref.dtype)
            return a * x + b * xt + s_ref[c, :, sl], a * xt + bt * x + sw_scr[c, :, sl]
        return f

    fwd = step(fwd_sl, coef_ref[0], coef_ref[1], coef_ref[2])
    bwd = step(bwd_sl, coef_ref[3], coef_ref[4], coef_ref[5])
    zero = jnp.zeros_like(coef_ref[0])
    loop = functools.partial(lax.fori_loop, unroll=S5_SCAN_UNROLL)
    carry = loop(0, n_ctx_chunks, lambda i, cr: fwd(n_lat_chunks + i, cr), (zero, zero))
    loop(0, n_lat_chunks, fwd, carry)
    carry = loop(0, n_ctx_chunks, lambda i, cr: bwd(n_chunks - 1 - i, cr), (zero, zero))
    loop(0, n_lat_chunks, lambda i, cr: bwd(n_lat_chunks - 1 - i, cr), carry)


def _s5_c_kernel(x_ref, n_ref, yi_ref, q_ref, y_ref):
    ys = [(yi_ref[gl] + jnp.dot(x_ref[gl], n_ref[gl], preferred_element_type=F32)).astype(BF16)
          for gl in range(S5_BLOCK_GROUPS)]
    y_ref[...] = jnp.dot(jnp.concatenate(ys, axis=1), q_ref[...], preferred_element_type=F32).astype(y_ref.dtype)


def _s5_mixer(u, m, w, ncat, coefs, layer, n_lat):
    r = u.shape[0]
    g, t, gb = S5_GROUPS, S5_CHUNK, S5_BLOCK_GROUPS
    nc = r // t
    sw = 4 * S5_STATE
    nblk = g // gb
    perm = _s5_lane_perm()
    p_in = jnp.asarray(perm, BF16)
    p_out = jnp.asarray(perm.T, BF16)
    u_rows = u.reshape(nc, t * S5_WIDTH)
    grp = lambda a, b: pl.BlockSpec((gb, a, b), lambda i: (i, 0, 0))
    wgrp = lambda a, b: pl.BlockSpec((None, gb, a, b), lambda i: (layer, i, 0, 0))
    perm_spec = pl.BlockSpec(perm.shape, lambda i: (0, 0), pipeline_mode=pl.Buffered(1))
    tok_specs = [pl.BlockSpec((nc, LANES), lambda i, s=s: (0, s * nblk + i)) for s in range(t)]
    y_intra, s = pl.pallas_call(
        _s5_a_kernel,
        out_shape=(jax.ShapeDtypeStruct((g, nc, S5_CW), F32), jax.ShapeDtypeStruct((g, nc, sw), F32)),
        grid=(nblk,),
        in_specs=tok_specs + [perm_spec, wgrp(S5_CW, S5_CW), wgrp(S5_CW, sw)],
        out_specs=(grp(nc, S5_CW), grp(nc, sw)),
        compiler_params=_cparams("parallel"),
        name="s5_chunk_local",
    )(*([u_rows] * t), p_in, m, w)

    gs = 16
    st = s.transpose(1, 0, 2)
    xt = pl.pallas_call(
        functools.partial(_s5_b_kernel, n_chunks=nc, n_lat_chunks=n_lat // t),
        out_shape=jax.ShapeDtypeStruct((nc, g, sw), BF16),
        grid=(g // gs,),
        in_specs=[pl.BlockSpec((nc, gs, sw), lambda i: (0, i, 0)),
                  pl.BlockSpec((None, 6, gs, 2 * S5_STATE), lambda i: (layer, 0, i, 0))],
        out_specs=pl.BlockSpec((nc, gs, sw), lambda i: (0, i, 0)),
        scratch_shapes=[pltpu.VMEM((nc, gs, sw), F32)],
        compiler_params=_cparams("parallel"),
        name="s5_chunk_scan",
    )(st, coefs)

    x_in = xt.transpose(1, 0, 2)
    y_blocks = pl.pallas_call(
        _s5_c_kernel,
        out_shape=jax.ShapeDtypeStruct((nblk, nc, gb * S5_CW), BF16),
        grid=(nblk,),
        in_specs=[grp(nc, sw), wgrp(sw, S5_CW), grp(nc, S5_CW), perm_spec],
        out_specs=pl.BlockSpec((None, nc, gb * S5_CW), lambda i: (i, 0, 0)),
        compiler_params=_cparams("parallel"),
        name="s5_state_out",
    )(x_in, ncat, y_intra, p_out)
    return y_blocks.reshape(nblk, nc, t, LANES).transpose(1, 2, 0, 3).reshape(r, S5_WIDTH)


def _merge_kernel(oa_ref, ob_ref, y_ref, wpa_ref, wpb_ref, wga_ref, wgg_ref, g0_ref, g1_ref, g2_ref,
                  o_ref, gy_scr):
    @pl.when(pl.program_id(1) == 0)
    def _():
        gy_scr[...] = _gelu_tanh(y_ref[...].astype(F32)).astype(BF16)

    dot = lambda a, b: jnp.dot(a, b, preferred_element_type=F32)
    pa = dot(oa_ref[...], wpa_ref[...])
    pb = dot(ob_ref[...], wpb_ref[...])
    gy = gy_scr[...]
    pc = dot(gy, wga_ref[...]) * _sigmoid(dot(gy, wgg_ref[...]))
    sg = lambda ref: _sigmoid(ref[...].astype(F32))
    o_ref[...] = (sg(g0_ref) * pa + sg(g1_ref) * pb + sg(g2_ref) * pc).astype(o_ref.dtype)


def _merge(o_a, o_b, y_s5, proj, w_pa, w_pb, w_glu, layer, d):
    r = o_a.shape[0]
    tm = _tile(r, 1056, 16)
    tn = _tile(math.gcd(d, OFF_GATE), 512, LANES)
    row = lambda w: pl.BlockSpec((tm, w), lambda i, j: (i, 0))
    col = lambda k, off: pl.BlockSpec((None, k, tn), lambda i, j: (layer, 0, off // tn + j))
    gate = lambda b: pl.BlockSpec((tm, tn), lambda i, j: (i, (OFF_GATE + b * d) // tn + j))
    return pl.pallas_call(
        _merge_kernel,
        out_shape=jax.ShapeDtypeStruct((r, d), BF16),
        grid=(r // tm, d // tn),
        in_specs=[row(DA_WIDTH), row(GQ_WIDTH), row(S5_WIDTH),
                  col(DA_WIDTH, 0), col(GQ_WIDTH, 0), col(S5_WIDTH, 0), col(S5_WIDTH, d),
                  gate(0), gate(1), gate(2)],
        out_specs=pl.BlockSpec((tm, tn), lambda i, j: (i, j)),
        scratch_shapes=[pltpu.VMEM((tm, S5_WIDTH), BF16)],
        compiler_params=_cparams("parallel", "arbitrary"),
        name="branch_merge",
    )(o_a, o_b, y_s5, w_pa, w_pb, w_glu, w_glu, proj, proj, proj)


def _out_ln_kernel(m_ref, w_ref, x_ref, gm_ref, lg_ref, lb_ref, o_ref, *, tm, n_lat, alpha):
    y = jnp.dot(m_ref[...], w_ref[...], preferred_element_type=F32)
    gate = _row_select(gm_ref, pl.program_id(0) * tm, tm, n_lat)
    o_ref[...] = _layer_norm(alpha * x_ref[...] + gate * y, lg_ref[...], lb_ref[...])


def _out_ln(merged, w_o, layer, xs, mods, ln_g, ln_b, n_lat, alpha):
    r, d = xs.shape
    tm = _tile(r, 528, 16)
    return pl.pallas_call(
        functools.partial(_out_ln_kernel, tm=tm, n_lat=n_lat, alpha=alpha),
        out_shape=jax.ShapeDtypeStruct((r, d), F32),
        grid=(r // tm,),
        in_specs=[
            pl.BlockSpec((tm, d), lambda i: (i, 0)),
            pl.BlockSpec((None, d, d), lambda i: (layer, 0, 0)),
            pl.BlockSpec((tm, d), lambda i: (i, 0)),
            _layer_spec((8, d), layer, 2),
            _layer_spec((1, d), layer), _layer_spec((1, d), layer),
        ],
        out_specs=pl.BlockSpec((tm, d), lambda i: (i, 0)),
        compiler_params=_cparams("parallel"),
        name="out_proj_ln",
    )(merged, w_o, xs, mods, ln_g, ln_b)


def _ffn_kernel(x_ref, sh_ref, sc_ref, gf_ref, wg_ref, wu_ref, wd_ref, lg_ref, lb_ref, o_ref,
                h_scr, *, tm, n_lat, alpha):
    i = pl.program_id(0)
    f = pl.program_id(1)

    @pl.when(f == 0)
    def _():
        sh = _row_select(sh_ref, i * tm, tm, n_lat)
        sc = _row_select(sc_ref, i * tm, tm, n_lat)
        h_scr[...] = (x_ref[...] * (1.0 + sc) + sh).astype(BF16)
        o_ref[...] = jnp.zeros_like(o_ref)

    h = h_scr[...]
    gate = jnp.dot(h, wg_ref[...].astype(BF16), preferred_element_type=F32)
    up = jnp.dot(h, wu_ref[...].astype(BF16), preferred_element_type=F32)
    act = (gate * _sigmoid(gate) * up).astype(BF16)
    o_ref[...] += jnp.dot(act, wd_ref[...].astype(BF16), preferred_element_type=F32)

    @pl.when(f == pl.num_programs(1) - 1)
    def _():
        g = _row_select(gf_ref, i * tm, tm, n_lat)
        o_ref[...] = _layer_norm(alpha * x_ref[...] + g * o_ref[...], lg_ref[...], lb_ref[...])


def _ffn(xs, r, mods, w_gate, w_up, w_down, layer, ln_g, ln_b, n_lat, alpha):
    d = xs.shape[1]
    f = w_gate.shape[2]
    tm = _tile(r, 1056, 16)
    tf = _tile(f, MXU_DIM, LANES)
    modv = lambda c: _layer_spec((8, d), layer, c)
    return pl.pallas_call(
        functools.partial(_ffn_kernel, tm=tm, n_lat=n_lat, alpha=alpha),
        out_shape=jax.ShapeDtypeStruct((r, d), F32),
        grid=(r // tm, f // tf),
        in_specs=[
            pl.BlockSpec((tm, d), lambda i, j: (i, 0), pipeline_mode=pl.Buffered(1)),
            modv(3), modv(4), modv(5),
            pl.BlockSpec((None, d, tf), lambda i, j: (layer, 0, j)),
            pl.BlockSpec((None, d, tf), lambda i, j: (layer, 0, j)),
            pl.BlockSpec((None, tf, d), lambda i, j: (layer, j, 0)),
            _layer_spec((1, d), layer), _layer_spec((1, d), layer),
        ],
        out_specs=pl.BlockSpec((tm, d), lambda i, j: (i, 0)),
        scratch_shapes=[pltpu.VMEM((tm, d), BF16)],
        compiler_params=_cparams("parallel", "arbitrary"),
        name="swiglu_ln",
    )(xs, mods, mods, mods, w_gate, w_up, w_down, ln_g, ln_b)


def kernel(x, c, ctx, c_ctx, w_ada, b_ada, w_in, da_lam_q1, da_lam_k1, da_lam_q2, da_lam_k2, da_subln_w, w_pa, gq_qnorm_w, gq_knorm_w, w_pb, s5_a_re, s5_a_im, s5_log_dt, s5_b_re, s5_b_im, s5_c_re, s5_c_im, s5_d, w_glu, w_o, ln_mix_g, ln_mix_b, w_ffn_gate, w_ffn_up, w_ffn_down, ln_ffn_g, ln_ffn_b):
    batch, n_lat, d = x.shape
    n_ctx = ctx.shape[1]
    depth = w_in.shape[0]
    r = n_lat + n_ctx
    assert batch == 1 and n_lat % GRID_W == 0 and n_ctx % S5_CHUNK == 0 and n_lat % S5_CHUNK == 0
    assert n_lat % n_ctx == 0
    alpha = (2 * depth) ** 0.25
    tk = _tile(r, 768, MXU_DIM)
    assert n_ctx <= tk and (n_lat - (r // tk - 1) * tk) % LANES == 0

    xs = jnp.concatenate([x[0], ctx[0]], axis=0).astype(F32)
    c_rows = jnp.zeros((8, d), F32).at[0].set(c[0]).at[1].set(c_ctx)
    mods = _ada_mod(c_rows, w_ada, b_ada)
    tab_da = _rope_tables(n_lat, n_ctx, DA_HEAD_DIM)
    tab_gq = _rope_tables(n_lat, n_ctx, GQ_HEAD_DIM)

    w_pa, w_pb, w_glu, w_o = (w.astype(BF16) for w in (w_pa, w_pb, w_glu, w_o))
    s5_m, s5_w, s5_n, s5_coefs = jax.vmap(_s5_matrices)(
        s5_a_re, s5_a_im, s5_log_dt, s5_b_re, s5_b_im, s5_c_re, s5_c_im, s5_d)

    row = lambda v: v.astype(F32)[:, None, :]
    lam_params = jnp.stack([da_lam_q1, da_lam_k1, da_lam_q2, da_lam_k2], axis=1).astype(F32)
    subln_w, qn_w, kn_w = row(da_subln_w), row(gq_qnorm_w), row(gq_knorm_w)
    ln_mix_g, ln_mix_b, ln_ffn_g, ln_ffn_b = row(ln_mix_g), row(ln_mix_b), row(ln_ffn_g), row(ln_ffn_b)

    for l in range(depth):
        lam_init = 0.8 - 0.6 * math.exp(-0.3 * l)
        proj = _in_proj(xs, mods, w_in, l, n_lat)
        qa, kat, qb, kbt = _prep_qk(proj, tab_da, tab_gq, qn_w, kn_w, l, tk)
        o_a = _diff_attention(lam_params, subln_w, l, qa, kat, proj, n_lat, lam_init)
        o_b = _gq_attention(qb, kbt, proj, n_lat)
        y_s5 = _s5_mixer(proj[:, OFF_U:OFF_U + S5_WIDTH], s5_m, s5_w, s5_n, s5_coefs, l, n_lat)
        merged = _merge(o_a, o_b, y_s5, proj, w_pa, w_pb, w_glu, l, d)
        xs = _out_ln(merged, w_o, l, xs, mods, ln_mix_g, ln_mix_b, n_lat, alpha)
        rows_out = n_lat if l == depth - 1 else r
        xs = _ffn(xs, rows_out, mods, w_ffn_gate, w_ffn_up, w_ffn_down, l, ln_ffn_g, ln_ffn_b, n_lat, alpha)
    return xs[None].astype(x.dtype)
```

```python
import functools
import math

import jax
import jax.numpy as jnp
import numpy as np
from jax import lax
from jax.experimental import pallas as pl
from jax.experimental.pallas import tpu as pltpu

F32 = jnp.float32
BF16 = jnp.bfloat16

GRID_W = 64
DA_HEADS = 8
DA_HEAD_DIM = 64
DA_WIDTH = DA_HEADS * 2 * DA_HEAD_DIM
GQ_HEADS = 8
GQ_KV_HEADS = 2
GQ_GROUP = GQ_HEADS // GQ_KV_HEADS
GQ_HEAD_DIM = 128
GQ_WIDTH = GQ_HEADS * GQ_HEAD_DIM
GQ_KV_WIDTH = GQ_KV_HEADS * GQ_HEAD_DIM
S5_WIDTH = 1024
S5_GROUP = 16
S5_GROUPS = S5_WIDTH // S5_GROUP
S5_STATE = 64
S5_MAX_REAL = -1e-4
N_BRANCH = 3
ROPE_THETA = 10000.0
LN_EPS = 1e-6
RMS_EPS = 1e-6
DA_SUBLN_EPS = 1e-5

OFF_QA = 0
OFF_KA = OFF_QA + DA_WIDTH
OFF_VA = OFF_KA + DA_WIDTH
OFF_QB = OFF_VA + DA_WIDTH
OFF_KB = OFF_QB + GQ_WIDTH
OFF_VB = OFF_KB + GQ_KV_WIDTH
OFF_U = OFF_VB + GQ_KV_WIDTH
OFF_GATE = OFF_U + S5_WIDTH

LANES = 128
MXU_DIM = 256
V7X_VMEM_BYTES = 64 * 1024 * 1024
VMEM_LIMIT = V7X_VMEM_BYTES - 8 * 1024 * 1024

S5_CHUNK = 16
S5_CW = S5_CHUNK * S5_GROUP
S5_BLOCK_GROUPS = LANES // S5_GROUP
S5_SCAN_UNROLL = 8

LOG2E = math.log2(math.e)
NEG_BIG = -1e30


def _cparams(*sem):
    return pltpu.CompilerParams(dimension_semantics=sem, vmem_limit_bytes=VMEM_LIMIT)


def _tile(n, target, mult):
    best = None
    for t in range(mult, min(n, target) + 1, mult):
        if n % t == 0:
            best = t
    assert best is not None, (n, target, mult)
    return best


def _row_halves(tm):
    if tm % 32 == 0:
        return [slice(0, tm // 2), slice(tm // 2, tm)]
    return [slice(0, tm)]


def _layer_spec(block_shape, layer, col=0):
    return pl.BlockSpec((None,) + tuple(block_shape), lambda *_: (layer, 0, col))


def _sigmoid(x):
    return 1.0 / (1.0 + jnp.exp(-x))


def _gelu_tanh(x):
    return 0.5 * x * (1.0 + jnp.tanh(math.sqrt(2.0 / math.pi) * (x + 0.044715 * (x * x * x))))


def _row_select(vec_ref, row0, tm, n_lat):
    rows = row0 + lax.broadcasted_iota(jnp.int32, (tm, 1), 0)
    return jnp.where(rows < n_lat, vec_ref[0:1, :], vec_ref[1:2, :])


def _layer_norm(z, g, b):
    mu = jnp.mean(z, axis=-1, keepdims=True)
    zc = z - mu
    var = jnp.mean(zc * zc, axis=-1, keepdims=True)
    return zc * lax.rsqrt(var + LN_EPS) * g + b


def _mod_kernel(c_ref, w_ref, b_ref, o_ref):
    a = c_ref[...]
    a = a * _sigmoid(a)
    o_ref[0] = jnp.dot(a.astype(BF16), w_ref[0].astype(BF16), preferred_element_type=F32) + b_ref[0]


def _ada_mod(c_rows, w_ada, b_ada):
    depth, d, n6 = w_ada.shape
    tn = _tile(n6, 1024, LANES)
    return pl.pallas_call(
        _mod_kernel,
        out_shape=jax.ShapeDtypeStruct((depth, 8, n6), F32),
        grid=(depth, n6 // tn),
        in_specs=[
            pl.BlockSpec((8, d), lambda l, j: (0, 0)),
            pl.BlockSpec((1, d, tn), lambda l, j: (l, 0, j)),
            pl.BlockSpec((1, 1, tn), lambda l, j: (l, 0, j)),
        ],
        out_specs=pl.BlockSpec((1, 8, tn), lambda l, j: (l, 0, j)),
        compiler_params=_cparams("parallel", "parallel"),
        name="ada_mod",
    )(c_rows, w_ada, b_ada.reshape(depth, 1, n6))


def _in_proj_kernel(x_ref, sh_ref, sc_ref, w_ref, o_ref, h_scr, *, tm, n_lat):
    i = pl.program_id(0)

    @pl.when(pl.program_id(1) == 0)
    def _():
        sh = _row_select(sh_ref, i * tm, tm, n_lat)
        sc = _row_select(sc_ref, i * tm, tm, n_lat)
        h_scr[...] = (x_ref[...] * (1.0 + sc) + sh).astype(BF16)

    o_ref[...] = jnp.dot(h_scr[...], w_ref[...].astype(BF16), preferred_element_type=F32).astype(o_ref.dtype)


def _in_proj(xs, mods, w_in, layer, n_lat):
    r, d = xs.shape
    n = w_in.shape[2]
    tm = _tile(r, 2112, 16)
    tn = _tile(n, 512, LANES)
    return pl.pallas_call(
        functools.partial(_in_proj_kernel, tm=tm, n_lat=n_lat),
        out_shape=jax.ShapeDtypeStruct((r, n), BF16),
        grid=(r // tm, n // tn),
        in_specs=[
            pl.BlockSpec((tm, d), lambda i, j: (i, 0), pipeline_mode=pl.Buffered(1)),
            _layer_spec((8, d), layer, 0),
            _layer_spec((8, d), layer, 1),
            pl.BlockSpec((None, d, tn), lambda i, j: (layer, 0, j)),
        ],
        out_specs=pl.BlockSpec((tm, tn), lambda i, j: (i, j)),
        scratch_shapes=[pltpu.VMEM((tm, d), BF16)],
        compiler_params=_cparams("parallel", "arbitrary"),
        name="in_proj",
    )(xs, mods, mods, w_in)


def _rope(z, cos, sin_a, sin_b, quarter):
    return z * cos + pltpu.roll(z, LANES - quarter, 1) * sin_a + pltpu.roll(z, quarter, 1) * sin_b


def _rms(z, w, eps):
    return z * lax.rsqrt(jnp.mean(z * z, axis=-1, keepdims=True) + eps) * w


def _prep_kernel(qa_ref, ka_ref, qb_ref, kb_ref, tda_ref, tgq_ref, qn_ref, kn_ref,
                 qa_o, kat_o, qb_o, kbt_o):
    cda, sda_a, sda_b = tda_ref[0], tda_ref[1], tda_ref[2]
    cgq, sgq_a, sgq_b = tgq_ref[0], tgq_ref[1], tgq_ref[2]
    da_scale = DA_HEAD_DIM ** -0.5 * LOG2E
    gq_scale = GQ_HEAD_DIM ** -0.5 * LOG2E
    for h in range(DA_HEADS):
        sl = slice(h * LANES, (h + 1) * LANES)
        z = qa_ref[:, sl].astype(F32)
        qa_o[:, sl] = (_rope(z, cda, sda_a, sda_b, DA_HEAD_DIM // 4) * da_scale).astype(BF16)
        z = ka_ref[:, sl].astype(F32)
        kat_o[h, 0] = _rope(z, cda, sda_a, sda_b, DA_HEAD_DIM // 4).T.astype(BF16)
    for h in range(GQ_HEADS):
        sl = slice(h * LANES, (h + 1) * LANES)
        z = _rms(qb_ref[:, sl].astype(F32), qn_ref[...], RMS_EPS)
        qb_o[:, sl] = (_rope(z, cgq, sgq_a, sgq_b, GQ_HEAD_DIM // 4) * gq_scale).astype(BF16)
    for h in range(GQ_KV_HEADS):
        sl = slice(h * LANES, (h + 1) * LANES)
        z = _rms(kb_ref[:, sl].astype(F32), kn_ref[...], RMS_EPS)
        kbt_o[h, 0] = _rope(z, cgq, sgq_a, sgq_b, GQ_HEAD_DIM // 4).T.astype(BF16)


def _prep_qk(proj, tab_da, tab_gq, qn_w, kn_w, layer, tk):
    r = proj.shape[0]
    nk = r // tk
    blk = lambda w, off: pl.BlockSpec((tk, w), lambda i: (i, off // w))
    out = lambda w: pl.BlockSpec((tk, w), lambda i: (i, 0))
    outt = lambda heads: pl.BlockSpec((heads, 1, LANES, tk), lambda i: (0, i, 0, 0))
    return pl.pallas_call(
        _prep_kernel,
        out_shape=(
            jax.ShapeDtypeStruct((r, DA_WIDTH), BF16),
            jax.ShapeDtypeStruct((DA_HEADS, nk, LANES, tk), BF16),
            jax.ShapeDtypeStruct((r, GQ_WIDTH), BF16),
            jax.ShapeDtypeStruct((GQ_KV_HEADS, nk, LANES, tk), BF16),
        ),
        grid=(nk,),
        in_specs=[
            blk(DA_WIDTH, OFF_QA), blk(DA_WIDTH, OFF_KA), blk(GQ_WIDTH, OFF_QB), blk(GQ_KV_WIDTH, OFF_KB),
            pl.BlockSpec((3, tk, LANES), lambda i: (0, i, 0)),
            pl.BlockSpec((3, tk, LANES), lambda i: (0, i, 0)),
            _layer_spec((1, LANES), layer),
            _layer_spec((1, LANES), layer),
        ],
        out_specs=(out(DA_WIDTH), outt(DA_HEADS), out(GQ_WIDTH), outt(GQ_KV_HEADS)),
        compiler_params=_cparams("parallel"),
        name="prep_qk",
    )(proj, proj, proj, proj, tab_da, tab_gq, qn_w, kn_w)


def _rope_tables(n_lat, n_ctx, dim):
    rows = n_lat // GRID_W
    row = np.repeat(np.arange(rows, dtype=np.float64), GRID_W)
    col = np.tile(np.arange(GRID_W, dtype=np.float64), rows)
    axis_dim = dim // 2
    inv_freq = ROPE_THETA ** (-np.arange(0, axis_dim, 2, dtype=np.float64) / axis_dim)
    ang_r = row[:, None] * inv_freq[None, :]
    ang_c = col[:, None] * inv_freq[None, :]
    ang = np.concatenate([ang_r, ang_r, ang_c, ang_c], axis=-1)
    cos, sin = np.cos(ang), np.sin(ang)
    first_half = (np.arange(dim) % (dim // 2)) < (dim // 4)
    sin_a = np.where(first_half[None, :], -sin, 0.0)
    sin_b = np.where(first_half[None, :], 0.0, sin)
    tab = np.stack([cos, sin_a, sin_b])
    ident = np.stack([np.ones((n_ctx, dim)), np.zeros((n_ctx, dim)), np.zeros((n_ctx, dim))])
    tab = np.concatenate([tab, ident], axis=1)
    return jnp.asarray(np.tile(tab, (1, 1, LANES // dim)), F32)


def _flash_loop(qs, kt_ref, v_ref, vx_scr, tk, n_chunks):
    @pl.when(pl.program_id(1) == 0)
    def _():
        vx_scr[:, :LANES] = v_ref[...]
        vx_scr[:, LANES:] = jnp.ones((vx_scr.shape[0], LANES), vx_scr.dtype)

    tq = qs[0].shape[0]
    scores = lambda c: [jnp.dot(q, kt_ref[c], preferred_element_type=F32) for q in qs]
    m = [jnp.full((tq, 1), NEG_BIG, F32) for _ in qs]
    acc = [jnp.zeros((tq, 2 * LANES), F32) for _ in qs]
    s_next = scores(0)
    for c in range(n_chunks):
        s_cur = s_next
        if c + 1 < n_chunks:
            s_next = scores(c + 1)
        vx = vx_scr[c * tk:(c + 1) * tk, :]
        for i, s in enumerate(s_cur):
            m_new = jnp.maximum(m[i], jnp.max(s, axis=-1, keepdims=True))
            p = jnp.exp2((s - m_new).astype(BF16))
            acc[i] = jnp.exp2(m[i] - m_new) * acc[i] + jnp.dot(p, vx, preferred_element_type=F32)
            m[i] = m_new
    return [a[:, :LANES] / a[:, LANES:] for a in acc]


def _softmax_once(s, v):
    p = jnp.exp2(s - jnp.max(s, axis=-1, keepdims=True))
    return jnp.dot(p.astype(BF16), v, preferred_element_type=F32) / jnp.sum(p, axis=-1, keepdims=True)


def _da_split(q):
    lane = lax.broadcasted_iota(jnp.int32, (1, LANES), 1)
    zero = jnp.zeros_like(q)
    return jnp.where(lane < DA_HEAD_DIM, q, zero), jnp.where(lane < DA_HEAD_DIM, zero, q)


def _da_finish(o1, o2, lam_ref, w_ref, o_ref, lam_init):
    lp = lam_ref[...]
    lam = (jnp.exp(jnp.sum(lp[0:1] * lp[1:2], axis=-1, keepdims=True))
           - jnp.exp(jnp.sum(lp[2:3] * lp[3:4], axis=-1, keepdims=True)) + lam_init)
    o = _rms(o1 - lam * o2, w_ref[...], DA_SUBLN_EPS) * (1.0 - lam_init)
    o_ref[...] = o.astype(o_ref.dtype)


def _da_kernel(lam_ref, w_ref, q_ref, kt_ref, v_ref, o_ref, vx, *, tk, n_chunks, lam_init):
    o1, o2 = _flash_loop(_da_split(q_ref[...]), kt_ref, v_ref, vx, tk, n_chunks)
    _da_finish(o1, o2, lam_ref, w_ref, o_ref, lam_init)


def _da_ctx_kernel(lam_ref, w_ref, q_ref, kt_ref, v_ref, latent_out_ref, o_ref, *, off, n_ctx, lam_init):
    del latent_out_ref
    q1, q2 = _da_split(q_ref[...])
    kt = kt_ref[:, off:off + n_ctx]
    v = v_ref[...]
    o1 = _softmax_once(jnp.dot(q1, kt, preferred_element_type=F32), v)
    o2 = _softmax_once(jnp.dot(q2, kt, preferred_element_type=F32), v)
    _da_finish(o1, o2, lam_ref, w_ref, o_ref, lam_init)


def _gq_kernel(q_ref, kt_ref, v_ref, o_ref, vx, *, tk, n_chunks):
    (o,) = _flash_loop((q_ref[...],), kt_ref, v_ref, vx, tk, n_chunks)
    o_ref[...] = o.astype(o_ref.dtype)


def _gq_ctx_kernel(q_ref, kt_ref, v_ref, latent_out_ref, o_ref, *, off, n_ctx):
    del latent_out_ref
    kt = kt_ref[:, off:off + n_ctx]
    o_ref[...] = _softmax_once(jnp.dot(q_ref[...], kt, preferred_element_type=F32), v_ref[...]).astype(o_ref.dtype)


def _attention(params, layer, q, kt, proj, v_off, kv_group, n_lat, main_kernel, ctx_kernel, name):
    r, width = q.shape
    n_heads = width // LANES
    n_ctx = r - n_lat
    _, n_chunks, _, tk = kt.shape
    tq = _tile(n_lat, 512, 16)
    ctx_blk = n_lat // n_ctx
    off = n_lat - (n_chunks - 1) * tk
    param_specs = [_layer_spec(p.shape[1:], layer) for p in params]
    out = pl.pallas_call(
        functools.partial(main_kernel, tk=tk, n_chunks=n_chunks),
        out_shape=jax.ShapeDtypeStruct((r, width), BF16),
        grid=(n_heads, n_lat // tq),
        in_specs=param_specs + [
            pl.BlockSpec((tq, LANES), lambda h, i: (i, h)),
            pl.BlockSpec((None, n_chunks, LANES, tk), lambda h, i: (h // kv_group, 0, 0, 0)),
            pl.BlockSpec((r, LANES), lambda h, i: (0, v_off // LANES + h // kv_group)),
        ],
        out_specs=pl.BlockSpec((tq, LANES), lambda h, i: (i, h)),
        scratch_shapes=[pltpu.VMEM((r, 2 * LANES), BF16)],
        compiler_params=_cparams("parallel", "arbitrary"),
        name=name,
    )(*params, q, kt, proj)
    return pl.pallas_call(
        functools.partial(ctx_kernel, off=off, n_ctx=n_ctx),
        out_shape=jax.ShapeDtypeStruct((r, width), BF16),
        grid=(n_heads,),
        in_specs=param_specs + [
            pl.BlockSpec((n_ctx, LANES), lambda h: (ctx_blk, h)),
            pl.BlockSpec((None, None, LANES, tk), lambda h: (h // kv_group, n_chunks - 1, 0, 0)),
            pl.BlockSpec((n_ctx, LANES), lambda h: (ctx_blk, v_off // LANES + h // kv_group)),
            pl.BlockSpec(memory_space=pl.ANY),
        ],
        out_specs=pl.BlockSpec((n_ctx, LANES), lambda h: (ctx_blk, h)),
        input_output_aliases={len(params) + 3: 0},
        compiler_params=_cparams("parallel"),
        name=name + "_ctx",
    )(*params, q, kt, proj, out)


def _diff_attention(lam_params, subln_w, layer, qa, kat, proj, n_lat, lam_init):
    return _attention((lam_params, subln_w), layer, qa, kat, proj, OFF_VA, 1, n_lat,
                      functools.partial(_da_kernel, lam_init=lam_init),
                      functools.partial(_da_ctx_kernel, lam_init=lam_init), "diff_attention")


def _gq_attention(qb, kbt, proj, n_lat):
    return _attention((), 0, qb, kbt, proj, OFF_VB, GQ_GROUP, n_lat, _gq_kernel, _gq_ctx_kernel, "gq_attention")


def _s5_matrices(a_re, a_im, log_dt, b_re, b_im, c_re, c_im, d_skip):
    t = S5_CHUNK
    g, p, h = S5_GROUPS, S5_STATE, S5_GROUP
    a_re = jnp.minimum(a_re.astype(F32), S5_MAX_REAL)
    a_im = a_im.astype(F32)
    dt = jnp.exp(log_dt.astype(F32))[..., None]
    mag = jnp.exp(a_re * dt)
    lr = mag * jnp.cos(a_im * dt)
    li = mag * jnp.sin(a_im * dt)
    den = jnp.square(a_re) + jnp.square(a_im)
    nr = lr - 1.0
    cr = (nr * a_re + li * a_im) / den
    ci = (li * a_re - nr * a_im) / den
    bbr = cr[..., None] * b_re - ci[..., None] * b_im
    bbi = cr[..., None] * b_im + ci[..., None] * b_re
    j = jnp.arange(t + 1, dtype=F32)
    pmag = jnp.exp((a_re * dt)[..., None] * j)
    pr = pmag * jnp.cos((a_im * dt)[..., None] * j)
    pi = pmag * jnp.sin((a_im * dt)[..., None] * j)
    ct_re = c_re.astype(F32).transpose(0, 1, 3, 2)[:, :, :, None, :]
    ct_im = c_im.astype(F32).transpose(0, 1, 3, 2)[:, :, :, None, :]
    clr = ct_re * pr[..., None] - ct_im * pi[..., None]
    cli = ct_re * pi[..., None] + ct_im * pr[..., None]
    kern = (jnp.einsum('dgpjh,dgpk->dgkjh', clr[:, :, :, :t], bbr)
            - jnp.einsum('dgpjh,dgpk->dgkjh', cli[:, :, :, :t], bbi))
    eye = jnp.eye(h, dtype=F32)[None] * d_skip.astype(F32).reshape(g, h)[:, None, :]
    k0 = kern[0][:, :, 0] + kern[1][:, :, 0] + eye
    kk = jnp.concatenate([kern[1][:, :, :0:-1], k0[:, :, None], kern[0][:, :, 1:]], axis=2)
    kk = kk.reshape(g, h, (2 * t - 1) * h)
    m = jnp.stack([kk[:, :, (t - 1 - s) * h:(2 * t - 1 - s) * h] for s in range(t)], axis=1)
    m = m.reshape(g, t * h, t * h)

    def to_state(d, pw_r, pw_i):
        pw_r = pw_r.transpose(0, 2, 1)[:, :, None, :]
        pw_i = pw_i.transpose(0, 2, 1)[:, :, None, :]
        br = bbr[d].transpose(0, 2, 1)[:, None, :, :]
        bi = bbi[d].transpose(0, 2, 1)[:, None, :, :]
        return (pw_r * br - pw_i * bi).reshape(g, t * h, p), (pw_r * bi + pw_i * br).reshape(g, t * h, p)

    wf = to_state(0, pr[0, :, :, :t][..., ::-1], pi[0, :, :, :t][..., ::-1])
    wb = to_state(1, pr[1, :, :, :t], pi[1, :, :, :t])
    w = jnp.concatenate([wf[0], wf[1], wb[0], wb[1]], axis=-1)

    flat = lambda c: c.reshape(g, p, t * h)
    n = jnp.concatenate([flat(clr[0][:, :, 1:]), -flat(cli[0][:, :, 1:]),
                         flat(clr[1][:, :, :0:-1]), -flat(cli[1][:, :, :0:-1])],
                        axis=1)

    def coef(d):
        r_, i_ = pr[d, :, :, t], pi[d, :, :, t]
        return [jnp.concatenate([r_, r_], -1), jnp.concatenate([-i_, i_], -1), jnp.concatenate([i_, -i_], -1)]

    coefs = jnp.stack(coef(0) + coef(1))
    return m.astype(BF16), w.astype(BF16), n.astype(BF16), coefs


def _s5_lane_perm():
    i = np.arange(S5_BLOCK_GROUPS * S5_CW)
    tok, grp, ch = i // LANES, (i // S5_GROUP) % S5_BLOCK_GROUPS, i % S5_GROUP
    perm = np.zeros((i.size, i.size), np.float32)
    perm[i, grp * S5_CW + tok * S5_GROUP + ch] = 1.0
    return perm


def _s5_a_kernel(*refs):
    u_refs = refs[:S5_CHUNK]
    p_ref, m_ref, w_ref, y_ref, s_ref = refs[S5_CHUNK:]
    u_nat = jnp.concatenate([u[...] for u in u_refs], axis=1)
    ug = jnp.dot(u_nat, p_ref[...], preferred_element_type=F32).astype(BF16)
    for gl in range(S5_BLOCK_GROUPS):
        u_g = ug[:, gl * S5_CW:(gl + 1) * S5_CW]
        y_ref[gl] = jnp.dot(u_g, m_ref[gl], preferred_element_type=F32)
        s_ref[gl] = jnp.dot(u_g, w_ref[gl], preferred_element_type=F32)


def _s5_b_kernel(s_ref, coef_ref, x_ref, sw_scr, *, n_chunks, n_lat_chunks):
    half = 2 * S5_STATE
    n_ctx_chunks = n_chunks - n_lat_chunks
    fwd_sl, bwd_sl = slice(0, half), slice(half, 2 * half)

    def swap(c, carry):
        sw_scr[c, :, fwd_sl] = pltpu.roll(s_ref[c, :, fwd_sl], S5_STATE, 1)
        sw_scr[c, :, bwd_sl] = pltpu.roll(s_ref[c, :, bwd_sl], S5_STATE, 1)
        return carry

    lax.fori_loop(0, n_chunks, swap, 0, unroll=S5_SCAN_UNROLL)

    def step(sl, a, b, bt):
        def f(c, carry):
            x, xt = carry
            x_ref[c, :, sl] = x.astype(x_ref.dtype)
            return a * x + b * xt + s_ref[c, :, sl], a * xt + bt * x + sw_scr[c, :, sl]
        return f

    fwd = step(fwd_sl, coef_ref[0], coef_ref[1], coef_ref[2])
    bwd = step(bwd_sl, coef_ref[3], coef_ref[4], coef_ref[5])
    zero = jnp.zeros_like(coef_ref[0])
    loop = functools.partial(lax.fori_loop, unroll=S5_SCAN_UNROLL)
    carry = loop(0, n_ctx_chunks, lambda i, cr: fwd(n_lat_chunks + i, cr), (zero, zero))
    loop(0, n_lat_chunks, fwd, carry)
    carry = loop(0, n_ctx_chunks, lambda i, cr: bwd(n_chunks - 1 - i, cr), (zero, zero))
    loop(0, n_lat_chunks, lambda i, cr: bwd(n_lat_chunks - 1 - i, cr), carry)


def _s5_c_kernel(x_ref, n_ref, yi_ref, q_ref, y_ref):
    ys = [(yi_ref[gl] + jnp.dot(x_ref[gl], n_ref[gl], preferred_element_type=F32)).astype(BF16)
          for gl in range(S5_BLOCK_GROUPS)]
    y_ref[...] = jnp.dot(jnp.concatenate(ys, axis=1), q_ref[...], preferred_element_type=F32).astype(y_ref.dtype)


def _s5_mixer(u, m, w, ncat, coefs, layer, n_lat):
    r = u.shape[0]
    g, t, gb = S5_GROUPS, S5_CHUNK, S5_BLOCK_GROUPS
    nc = r // t
    sw = 4 * S5_STATE
    nblk = g // gb
    perm = _s5_lane_perm()
    p_in = jnp.asarray(perm, BF16)
    p_out = jnp.asarray(perm.T, BF16)
    u_rows = u.reshape(nc, t * S5_WIDTH)
    grp = lambda a, b: pl.BlockSpec((gb, a, b), lambda i: (i, 0, 0))
    wgrp = lambda a, b: pl.BlockSpec((None, gb, a, b), lambda i: (layer, i, 0, 0))
    perm_spec = pl.BlockSpec(perm.shape, lambda i: (0, 0), pipeline_mode=pl.Buffered(1))
    tok_specs = [pl.BlockSpec((nc, LANES), lambda i, s=s: (0, s * nblk + i)) for s in range(t)]
    y_intra, s = pl.pallas_call(
        _s5_a_kernel,
        out_shape=(jax.ShapeDtypeStruct((g, nc, S5_CW), F32), jax.ShapeDtypeStruct((g, nc, sw), F32)),
        grid=(nblk,),
        in_specs=tok_specs + [perm_spec, wgrp(S5_CW, S5_CW), wgrp(S5_CW, sw)],
        out_specs=(grp(nc, S5_CW), grp(nc, sw)),
        compiler_params=_cparams("parallel"),
        name="s5_chunk_local",
    )(*([u_rows] * t), p_in, m, w)

    gs = 16
    st = s.transpose(1, 0, 2)
    xt = pl.pallas_call(
        functools.partial(_s5_b_kernel, n_chunks=nc, n_lat_chunks=n_lat // t),
        out_shape=jax.ShapeDtypeStruct((nc, g, sw), BF16),
        grid=(g // gs,),
        in_specs=[pl.BlockSpec((nc, gs, sw), lambda i: (0, i, 0)),
                  pl.BlockSpec((None, 6, gs, 2 * S5_STATE), lambda i: (layer, 0, i, 0))],
        out_specs=pl.BlockSpec((nc, gs, sw), lambda i: (0, i, 0)),
        scratch_shapes=[pltpu.VMEM((nc, gs, sw), F32)],
        compiler_params=_cparams("parallel"),
        name="s5_chunk_scan",
    )(st, coefs)

    x_in = xt.transpose(1, 0, 2)
    y_blocks = pl.pallas_call(
        _s5_c_kernel,
        out_shape=jax.ShapeDtypeStruct((nblk, nc, gb * S5_CW), BF16),
        grid=(nblk,),
        in_specs=[grp(nc, sw), wgrp(sw, S5_CW), grp(nc, S5_CW), perm_spec],
        out_specs=pl.BlockSpec((None, nc, gb * S5_CW), lambda i: (i, 0, 0)),
        compiler_params=_cparams("parallel"),
        name="s5_state_out",
    )(x_in, ncat, y_intra, p_out)
    return y_blocks.reshape(nblk, nc, t, LANES).transpose(1, 2, 0, 3).reshape(r, S5_WIDTH)


def _merge_kernel(oa_ref, ob_ref, y_ref, wpa_ref, wpb_ref, wga_ref, wgg_ref, g0_ref, g1_ref, g2_ref,
                  o_ref, gy_scr):
    @pl.when(pl.program_id(1) == 0)
    def _():
        gy_scr[...] = _gelu_tanh(y_ref[...].astype(F32)).astype(BF16)

    dot = lambda a, b: jnp.dot(a, b, preferred_element_type=F32)
    for rows in _row_halves(o_ref.shape[0]):
        pa = dot(oa_ref[rows, :], wpa_ref[...])
        pb = dot(ob_ref[rows, :], wpb_ref[...])
        gy = gy_scr[rows, :]
        pc = dot(gy, wga_ref[...]) * _sigmoid(dot(gy, wgg_ref[...]))
        sg = lambda ref: _sigmoid(ref[rows, :].astype(F32))
        o_ref[rows, :] = (sg(g0_ref) * pa + sg(g1_ref) * pb + sg(g2_ref) * pc).astype(o_ref.dtype)


def _merge(o_a, o_b, y_s5, proj, w_pa, w_pb, w_glu, layer, d):
    r = o_a.shape[0]
    tm = _tile(r, 1056, 16)
    tn = _tile(math.gcd(d, OFF_GATE), 512, LANES)
    row = lambda w: pl.BlockSpec((tm, w), lambda i, j: (i, 0))
    col = lambda k, off: pl.BlockSpec((None, k, tn), lambda i, j: (layer, 0, off // tn + j))
    gate = lambda b: pl.BlockSpec((tm, tn), lambda i, j: (i, (OFF_GATE + b * d) // tn + j))
    return pl.pallas_call(
        _merge_kernel,
        out_shape=jax.ShapeDtypeStruct((r, d), BF16),
        grid=(r // tm, d // tn),
        in_specs=[row(DA_WIDTH), row(GQ_WIDTH), row(S5_WIDTH),
                  col(DA_WIDTH, 0), col(GQ_WIDTH, 0), col(S5_WIDTH, 0), col(S5_WIDTH, d),
                  gate(0), gate(1), gate(2)],
        out_specs=pl.BlockSpec((tm, tn), lambda i, j: (i, j)),
        scratch_shapes=[pltpu.VMEM((tm, S5_WIDTH), BF16)],
        compiler_params=_cparams("parallel", "arbitrary"),
        name="branch_merge",
    )(o_a, o_b, y_s5, w_pa, w_pb, w_glu, w_glu, proj, proj, proj)


def _out_ln_kernel(m_ref, w_ref, x_ref, gm_ref, lg_ref, lb_ref, o_ref, *, tm, n_lat, alpha):
    for rows in _row_halves(tm):
        n_rows = rows.stop - rows.start
        y = jnp.dot(m_ref[rows, :], w_ref[...], preferred_element_type=F32)
        gate = _row_select(gm_ref, pl.program_id(0) * tm + rows.start, n_rows, n_lat)
        o_ref[rows, :] = _layer_norm(alpha * x_ref[rows, :] + gate * y, lg_ref[...], lb_ref[...])


def _out_ln(merged, w_o, layer, xs, mods, ln_g, ln_b, n_lat, alpha):
    r, d = xs.shape
    tm = _tile(r, 768, 32)
    return pl.pallas_call(
        functools.partial(_out_ln_kernel, tm=tm, n_lat=n_lat, alpha=alpha),
        out_shape=jax.ShapeDtypeStruct((r, d), F32),
        grid=(r // tm,),
        in_specs=[
            pl.BlockSpec((tm, d), lambda i: (i, 0)),
            pl.BlockSpec((None, d, d), lambda i: (layer, 0, 0), pipeline_mode=pl.Buffered(1)),
            pl.BlockSpec((tm, d), lambda i: (i, 0)),
            _layer_spec((8, d), layer, 2),
            _layer_spec((1, d), layer), _layer_spec((1, d), layer),
        ],
        out_specs=pl.BlockSpec((tm, d), lambda i: (i, 0)),
        compiler_params=_cparams("parallel"),
        name="out_proj_ln",
    )(merged, w_o, xs, mods, ln_g, ln_b)


def _ffn_kernel(x_ref, sh_ref, sc_ref, gf_ref, wg_ref, wu_ref, wd_ref, lg_ref, lb_ref, o_ref,
                h_scr, *, tm, n_lat, alpha):
    i = pl.program_id(0)
    f = pl.program_id(1)

    @pl.when(f == 0)
    def _():
        sh = _row_select(sh_ref, i * tm, tm, n_lat)
        sc = _row_select(sc_ref, i * tm, tm, n_lat)
        h_scr[...] = (x_ref[...] * (1.0 + sc) + sh).astype(BF16)
        o_ref[...] = jnp.zeros_like(o_ref)

    h = h_scr[...]
    gate = jnp.dot(h, wg_ref[...].astype(BF16), preferred_element_type=F32)
    up = jnp.dot(h, wu_ref[...].astype(BF16), preferred_element_type=F32)
    act = (gate * _sigmoid(gate) * up).astype(BF16)
    o_ref[...] += jnp.dot(act, wd_ref[...].astype(BF16), preferred_element_type=F32)

    @pl.when(f == pl.num_programs(1) - 1)
    def _():
        g = _row_select(gf_ref, i * tm, tm, n_lat)
        o_ref[...] = _layer_norm(alpha * x_ref[...] + g * o_ref[...], lg_ref[...], lb_ref[...])


def _ffn(xs, r, mods, w_gate, w_up, w_down, layer, ln_g, ln_b, n_lat, alpha):
    d = xs.shape[1]
    f = w_gate.shape[2]
    tm = _tile(r, 1056, 16)
    tf = _tile(f, MXU_DIM, LANES)
    modv = lambda c: _layer_spec((8, d), layer, c)
    return pl.pallas_call(
        functools.partial(_ffn_kernel, tm=tm, n_lat=n_lat, alpha=alpha),
        out_shape=jax.ShapeDtypeStruct((r, d), F32),
        grid=(r // tm, f // tf),
        in_specs=[
            pl.BlockSpec((tm, d), lambda i, j: (i, 0), pipeline_mode=pl.Buffered(1)),
            modv(3), modv(4), modv(5),
            pl.BlockSpec((None, d, tf), lambda i, j: (layer, 0, j)),
            pl.BlockSpec((None, d, tf), lambda i, j: (layer, 0, j)),
            pl.BlockSpec((None, tf, d), lambda i, j: (layer, j, 0)),
            _layer_spec((1, d), layer), _layer_spec((1, d), layer),
        ],
        out_specs=pl.BlockSpec((tm, d), lambda i, j: (i, 0)),
        scratch_shapes=[pltpu.VMEM((tm, d), BF16)],
        compiler_params=_cparams("parallel", "arbitrary"),
        name="swiglu_ln",
    )(xs, mods, mods, mods, w_gate, w_up, w_down, ln_g, ln_b)


def kernel(x, c, ctx, c_ctx, w_ada, b_ada, w_in, da_lam_q1, da_lam_k1, da_lam_q2, da_lam_k2, da_subln_w, w_pa, gq_qnorm_w, gq_knorm_w, w_pb, s5_a_re, s5_a_im, s5_log_dt, s5_b_re, s5_b_im, s5_c_re, s5_c_im, s5_d, w_glu, w_o, ln_mix_g, ln_mix_b, w_ffn_gate, w_ffn_up, w_ffn_down, ln_ffn_g, ln_ffn_b):
    batch, n_lat, d = x.shape
    n_ctx = ctx.shape[1]
    depth = w_in.shape[0]
    r = n_lat + n_ctx
    assert batch == 1 and n_lat % GRID_W == 0 and n_ctx % S5_CHUNK == 0 and n_lat % S5_CHUNK == 0
    assert n_lat % n_ctx == 0
    alpha = (2 * depth) ** 0.25
    tk = _tile(r, 768, MXU_DIM)
    assert n_ctx <= tk and (n_lat - (r // tk - 1) * tk) % LANES == 0

    xs = jnp.concatenate([x[0], ctx[0]], axis=0).astype(F32)
    c_rows = jnp.zeros((8, d), F32).at[0].set(c[0]).at[1].set(c_ctx)
    mods = _ada_mod(c_rows, w_ada, b_ada)
    tab_da = _rope_tables(n_lat, n_ctx, DA_HEAD_DIM)
    tab_gq = _rope_tables(n_lat, n_ctx, GQ_HEAD_DIM)

    w_pa, w_pb, w_glu, w_o = (w.astype(BF16) for w in (w_pa, w_pb, w_glu, w_o))
    s5_m, s5_w, s5_n, s5_coefs = jax.vmap(_s5_matrices)(
        s5_a_re, s5_a_im, s5_log_dt, s5_b_re, s5_b_im, s5_c_re, s5_c_im, s5_d)

    row = lambda v: v.astype(F32)[:, None, :]
    lam_params = jnp.stack([da_lam_q1, da_lam_k1, da_lam_q2, da_lam_k2], axis=1).astype(F32)
    subln_w, qn_w, kn_w = row(da_subln_w), row(gq_qnorm_w), row(gq_knorm_w)
    ln_mix_g, ln_mix_b, ln_ffn_g, ln_ffn_b = row(ln_mix_g), row(ln_mix_b), row(ln_ffn_g), row(ln_ffn_b)

    for l in range(depth):
        lam_init = 0.8 - 0.6 * math.exp(-0.3 * l)
        proj = _in_proj(xs, mods, w_in, l, n_lat)
        qa, kat, qb, kbt = _prep_qk(proj, tab_da, tab_gq, qn_w, kn_w, l, tk)
        o_a = _diff_attention(lam_params, subln_w, l, qa, kat, proj, n_lat, lam_init)
        o_b = _gq_attention(qb, kbt, proj, n_lat)
        y_s5 = _s5_mixer(proj[:, OFF_U:OFF_U + S5_WIDTH], s5_m, s5_w, s5_n, s5_coefs, l, n_lat)
        merged = _merge(o_a, o_b, y_s5, proj, w_pa, w_pb, w_glu, l, d)
        xs = _out_ln(merged, w_o, l, xs, mods, ln_mix_g, ln_mix_b, n_lat, alpha)
        rows_out = n_lat if l == depth - 1 else r
        xs = _ffn(xs, rows_out, mods, w_ffn_gate, w_ffn_up, w_ffn_down, l, ln_ffn_g, ln_ffn_b, n_lat, alpha)
    return xs[None].astype(x.dtype)
```

```python
import functools
import math

import jax
import jax.numpy as jnp
import numpy as np
from jax import lax
from jax.experimental import pallas as pl
from jax.experimental.pallas import tpu as pltpu

F32 = jnp.float32
BF16 = jnp.bfloat16

GRID_W = 64
DA_HEADS = 8
DA_HEAD_DIM = 64
DA_WIDTH = DA_HEADS * 2 * DA_HEAD_DIM
GQ_HEADS = 8
GQ_KV_HEADS = 2
GQ_GROUP = GQ_HEADS // GQ_KV_HEADS
GQ_HEAD_DIM = 128
GQ_WIDTH = GQ_HEADS * GQ_HEAD_DIM
GQ_KV_WIDTH = GQ_KV_HEADS * GQ_HEAD_DIM
S5_WIDTH = 1024
S5_GROUP = 16
S5_GROUPS = S5_WIDTH // S5_GROUP
S5_STATE = 64
S5_MAX_REAL = -1e-4
N_BRANCH = 3
ROPE_THETA = 10000.0
LN_EPS = 1e-6
RMS_EPS = 1e-6
DA_SUBLN_EPS = 1e-5

OFF_QA = 0
OFF_KA = OFF_QA + DA_WIDTH
OFF_VA = OFF_KA + DA_WIDTH
OFF_QB = OFF_VA + DA_WIDTH
OFF_KB = OFF_QB + GQ_WIDTH
OFF_VB = OFF_KB + GQ_KV_WIDTH
OFF_U = OFF_VB + GQ_KV_WIDTH
OFF_GATE = OFF_U + S5_WIDTH

LANES = 128
MXU_DIM = 256
V7X_VMEM_BYTES = 64 * 1024 * 1024
VMEM_LIMIT = V7X_VMEM_BYTES - 8 * 1024 * 1024

S5_CHUNK = 16
S5_CW = S5_CHUNK * S5_GROUP
S5_BLOCK_GROUPS = LANES // S5_GROUP
S5_SCAN_UNROLL = 8

LOG2E = math.log2(math.e)
NEG_BIG = -1e30


def _cparams(*sem):
    return pltpu.CompilerParams(dimension_semantics=sem, vmem_limit_bytes=VMEM_LIMIT)


def _tile(n, target, mult):
    best = None
    for t in range(mult, min(n, target) + 1, mult):
        if n % t == 0:
            best = t
    assert best is not None, (n, target, mult)
    return best


def _row_halves(tm):
    if tm % 32 == 0:
        return [slice(0, tm // 2), slice(tm // 2, tm)]
    return [slice(0, tm)]


def _layer_spec(block_shape, layer, col=0):
    return pl.BlockSpec((None,) + tuple(block_shape), lambda *_: (layer, 0, col))


def _sigmoid(x):
    return 1.0 / (1.0 + jnp.exp(-x))


def _gelu_tanh(x):
    return 0.5 * x * (1.0 + jnp.tanh(math.sqrt(2.0 / math.pi) * (x + 0.044715 * (x * x * x))))


def _row_select(vec_ref, row0, tm, n_lat):
    rows = row0 + lax.broadcasted_iota(jnp.int32, (tm, 1), 0)
    return jnp.where(rows < n_lat, vec_ref[0:1, :], vec_ref[1:2, :])


def _layer_norm(z, g, b):
    mu = jnp.mean(z, axis=-1, keepdims=True)
    zc = z - mu
    var = jnp.mean(zc * zc, axis=-1, keepdims=True)
    return zc * lax.rsqrt(var + LN_EPS) * g + b


def _mod_kernel(c_ref, w_ref, b_ref, o_ref):
    a = c_ref[...]
    a = a * _sigmoid(a)
    o_ref[0] = jnp.dot(a.astype(BF16), w_ref[0].astype(BF16), preferred_element_type=F32) + b_ref[0]


def _ada_mod(c_rows, w_ada, b_ada):
    depth, d, n6 = w_ada.shape
    tn = _tile(n6, 1024, LANES)
    return pl.pallas_call(
        _mod_kernel,
        out_shape=jax.ShapeDtypeStruct((depth, 8, n6), F32),
        grid=(depth, n6 // tn),
        in_specs=[
            pl.BlockSpec((8, d), lambda l, j: (0, 0)),
            pl.BlockSpec((1, d, tn), lambda l, j: (l, 0, j)),
            pl.BlockSpec((1, 1, tn), lambda l, j: (l, 0, j)),
        ],
        out_specs=pl.BlockSpec((1, 8, tn), lambda l, j: (l, 0, j)),
        compiler_params=_cparams("parallel", "parallel"),
        name="ada_mod",
    )(c_rows, w_ada, b_ada.reshape(depth, 1, n6))


def _in_proj_kernel(x_ref, sh_ref, sc_ref, w_ref, o_ref, h_scr, *, tm, n_lat):
    i = pl.program_id(0)

    @pl.when(pl.program_id(1) == 0)
    def _():
        sh = _row_select(sh_ref, i * tm, tm, n_lat)
        sc = _row_select(sc_ref, i * tm, tm, n_lat)
        h_scr[...] = (x_ref[...] * (1.0 + sc) + sh).astype(BF16)

    o_ref[...] = jnp.dot(h_scr[...], w_ref[...].astype(BF16), preferred_element_type=F32).astype(o_ref.dtype)


def _in_proj(xs, mods, w_in, layer, n_lat):
    r, d = xs.shape
    n = w_in.shape[2]
    tm = _tile(r, 2112, 16)
    tn = _tile(n, 512, LANES)
    return pl.pallas_call(
        functools.partial(_in_proj_kernel, tm=tm, n_lat=n_lat),
        out_shape=jax.ShapeDtypeStruct((r, n), BF16),
        grid=(r // tm, n // tn),
        in_specs=[
            pl.BlockSpec((tm, d), lambda i, j: (i, 0), pipeline_mode=pl.Buffered(1)),
            _layer_spec((8, d), layer, 0),
            _layer_spec((8, d), layer, 1),
            pl.BlockSpec((None, d, tn), lambda i, j: (layer, 0, j)),
        ],
        out_specs=pl.BlockSpec((tm, tn), lambda i, j: (i, j)),
        scratch_shapes=[pltpu.VMEM((tm, d), BF16)],
        compiler_params=_cparams("parallel", "arbitrary"),
        name="in_proj",
    )(xs, mods, mods, w_in)


def _rope(z, cos, sin_a, sin_b, quarter):
    return z * cos + pltpu.roll(z, LANES - quarter, 1) * sin_a + pltpu.roll(z, quarter, 1) * sin_b


def _rms(z, w, eps):
    return z * lax.rsqrt(jnp.mean(z * z, axis=-1, keepdims=True) + eps) * w


def _prep_kernel(qa_ref, ka_ref, qb_ref, kb_ref, tda_ref, tgq_ref, qn_ref, kn_ref,
                 qa_o, kat_o, qb_o, kbt_o):
    cda, sda_a, sda_b = tda_ref[0], tda_ref[1], tda_ref[2]
    cgq, sgq_a, sgq_b = tgq_ref[0], tgq_ref[1], tgq_ref[2]
    da_scale = DA_HEAD_DIM ** -0.5 * LOG2E
    gq_scale = GQ_HEAD_DIM ** -0.5 * LOG2E
    for h in range(DA_HEADS):
        sl = slice(h * LANES, (h + 1) * LANES)
        z = qa_ref[:, sl].astype(F32)
        qa_o[:, sl] = (_rope(z, cda, sda_a, sda_b, DA_HEAD_DIM // 4) * da_scale).astype(BF16)
        z = ka_ref[:, sl].astype(F32)
        kat_o[h, 0] = _rope(z, cda, sda_a, sda_b, DA_HEAD_DIM // 4).T.astype(BF16)
    for h in range(GQ_HEADS):
        sl = slice(h * LANES, (h + 1) * LANES)
        z = _rms(qb_ref[:, sl].astype(F32), qn_ref[...], RMS_EPS)
        qb_o[:, sl] = (_rope(z, cgq, sgq_a, sgq_b, GQ_HEAD_DIM // 4) * gq_scale).astype(BF16)
    for h in range(GQ_KV_HEADS):
        sl = slice(h * LANES, (h + 1) * LANES)
        z = _rms(kb_ref[:, sl].astype(F32), kn_ref[...], RMS_EPS)
        kbt_o[h, 0] = _rope(z, cgq, sgq_a, sgq_b, GQ_HEAD_DIM // 4).T.astype(BF16)


def _prep_qk(proj, tab_da, tab_gq, qn_w, kn_w, layer, tk):
    r = proj.shape[0]
    nk = r // tk
    blk = lambda w, off: pl.BlockSpec((tk, w), lambda i: (i, off // w))
    out = lambda w: pl.BlockSpec((tk, w), lambda i: (i, 0))
    outt = lambda heads: pl.BlockSpec((heads, 1, LANES, tk), lambda i: (0, i, 0, 0))
    return pl.pallas_call(
        _prep_kernel,
        out_shape=(
            jax.ShapeDtypeStruct((r, DA_WIDTH), BF16),
            jax.ShapeDtypeStruct((DA_HEADS, nk, LANES, tk), BF16),
            jax.ShapeDtypeStruct((r, GQ_WIDTH), BF16),
            jax.ShapeDtypeStruct((GQ_KV_HEADS, nk, LANES, tk), BF16),
        ),
        grid=(nk,),
        in_specs=[
            blk(DA_WIDTH, OFF_QA), blk(DA_WIDTH, OFF_KA), blk(GQ_WIDTH, OFF_QB), blk(GQ_KV_WIDTH, OFF_KB),
            pl.BlockSpec((3, tk, LANES), lambda i: (0, i, 0)),
            pl.BlockSpec((3, tk, LANES), lambda i: (0, i, 0)),
            _layer_spec((1, LANES), layer),
            _layer_spec((1, LANES), layer),
        ],
        out_specs=(out(DA_WIDTH), outt(DA_HEADS), out(GQ_WIDTH), outt(GQ_KV_HEADS)),
        compiler_params=_cparams("parallel"),
        name="prep_qk",
    )(proj, proj, proj, proj, tab_da, tab_gq, qn_w, kn_w)


def _rope_tables(n_lat, n_ctx, dim):
    rows = n_lat // GRID_W
    row = np.repeat(np.arange(rows, dtype=np.float64), GRID_W)
    col = np.tile(np.arange(GRID_W, dtype=np.float64), rows)
    axis_dim = dim // 2
    inv_freq = ROPE_THETA ** (-np.arange(0, axis_dim, 2, dtype=np.float64) / axis_dim)
    ang_r = row[:, None] * inv_freq[None, :]
    ang_c = col[:, None] * inv_freq[None, :]
    ang = np.concatenate([ang_r, ang_r, ang_c, ang_c], axis=-1)
    cos, sin = np.cos(ang), np.sin(ang)
    first_half = (np.arange(dim) % (dim // 2)) < (dim // 4)
    sin_a = np.where(first_half[None, :], -sin, 0.0)
    sin_b = np.where(first_half[None, :], 0.0, sin)
    tab = np.stack([cos, sin_a, sin_b])
    ident = np.stack([np.ones((n_ctx, dim)), np.zeros((n_ctx, dim)), np.zeros((n_ctx, dim))])
    tab = np.concatenate([tab, ident], axis=1)
    return jnp.asarray(np.tile(tab, (1, 1, LANES // dim)), F32)


def _flash_loop(qs, kt_ref, v_ref, vx_scr, tk, n_chunks):
    @pl.when(pl.program_id(1) == 0)
    def _():
        vx_scr[:, :LANES] = v_ref[...]
        vx_scr[:, LANES:] = jnp.ones((vx_scr.shape[0], LANES), vx_scr.dtype)

    tq = qs[0].shape[0]
    scores = lambda c: [jnp.dot(q, kt_ref[c], preferred_element_type=F32) for q in qs]
    m = [jnp.full((tq, 1), NEG_BIG, F32) for _ in qs]
    acc = [jnp.zeros((tq, 2 * LANES), F32) for _ in qs]
    s_next = scores(0)
    for c in range(n_chunks):
        s_cur = s_next
        if c + 1 < n_chunks:
            s_next = scores(c + 1)
        vx = vx_scr[c * tk:(c + 1) * tk, :]
        for i, s in enumerate(s_cur):
            m_new = jnp.maximum(m[i], jnp.max(s, axis=-1, keepdims=True))
            p = jnp.exp2((s - m_new).astype(BF16))
            acc[i] = jnp.exp2(m[i] - m_new) * acc[i] + jnp.dot(p, vx, preferred_element_type=F32)
            m[i] = m_new
    return [a[:, :LANES] / a[:, LANES:] for a in acc]


def _softmax_once(s, v):
    p = jnp.exp2(s - jnp.max(s, axis=-1, keepdims=True))
    return jnp.dot(p.astype(BF16), v, preferred_element_type=F32) / jnp.sum(p, axis=-1, keepdims=True)


def _da_split(q):
    lane = lax.broadcasted_iota(jnp.int32, (1, LANES), 1)
    zero = jnp.zeros_like(q)
    return jnp.where(lane < DA_HEAD_DIM, q, zero), jnp.where(lane < DA_HEAD_DIM, zero, q)


def _da_finish(o1, o2, lam_ref, w_ref, o_ref, lam_init):
    lp = lam_ref[...]
    lam = (jnp.exp(jnp.sum(lp[0:1] * lp[1:2], axis=-1, keepdims=True))
           - jnp.exp(jnp.sum(lp[2:3] * lp[3:4], axis=-1, keepdims=True)) + lam_init)
    o = _rms(o1 - lam * o2, w_ref[...], DA_SUBLN_EPS) * (1.0 - lam_init)
    o_ref[...] = o.astype(o_ref.dtype)


def _da_kernel(lam_ref, w_ref, q_ref, kt_ref, v_ref, o_ref, vx, *, tk, n_chunks, lam_init):
    o1, o2 = _flash_loop(_da_split(q_ref[...]), kt_ref, v_ref, vx, tk, n_chunks)
    _da_finish(o1, o2, lam_ref, w_ref, o_ref, lam_init)


def _da_ctx_kernel(lam_ref, w_ref, q_ref, kt_ref, v_ref, o_ref, *, off, n_ctx, lam_init):
    q1, q2 = _da_split(q_ref[...])
    kt = kt_ref[:, off:off + n_ctx]
    v = v_ref[...]
    o1 = _softmax_once(jnp.dot(q1, kt, preferred_element_type=F32), v)
    o2 = _softmax_once(jnp.dot(q2, kt, preferred_element_type=F32), v)
    _da_finish(o1, o2, lam_ref, w_ref, o_ref, lam_init)


def _gq_kernel(q_ref, kt_ref, v_ref, o_ref, vx, *, tk, n_chunks):
    (o,) = _flash_loop((q_ref[...],), kt_ref, v_ref, vx, tk, n_chunks)
    o_ref[...] = o.astype(o_ref.dtype)


def _gq_ctx_kernel(q_ref, kt_ref, v_ref, o_ref, *, off, n_ctx):
    kt = kt_ref[:, off:off + n_ctx]
    o_ref[...] = _softmax_once(jnp.dot(q_ref[...], kt, preferred_element_type=F32), v_ref[...]).astype(o_ref.dtype)


def _attention(params, layer, q, kt, proj, v_off, kv_group, n_lat, main_kernel, ctx_kernel, name):
    r, width = q.shape
    n_heads = width // LANES
    n_ctx = r - n_lat
    _, n_chunks, _, tk = kt.shape
    tq = _tile(n_lat, 512, 16)
    ctx_blk = n_lat // n_ctx
    off = n_lat - (n_chunks - 1) * tk
    param_specs = [_layer_spec(p.shape[1:], layer) for p in params]
    out = pl.pallas_call(
        functools.partial(main_kernel, tk=tk, n_chunks=n_chunks),
        out_shape=jax.ShapeDtypeStruct((r, width), BF16),
        grid=(n_heads, n_lat // tq),
        in_specs=param_specs + [
            pl.BlockSpec((tq, LANES), lambda h, i: (i, h)),
            pl.BlockSpec((None, n_chunks, LANES, tk), lambda h, i: (h // kv_group, 0, 0, 0)),
            pl.BlockSpec((r, LANES), lambda h, i: (0, v_off // LANES + h // kv_group)),
        ],
        out_specs=pl.BlockSpec((tq, LANES), lambda h, i: (i, h)),
        scratch_shapes=[pltpu.VMEM((r, 2 * LANES), BF16)],
        input_output_aliases={len(params): 0},
        compiler_params=_cparams("parallel", "arbitrary"),
        name=name,
    )(*params, q, kt, proj)
    return pl.pallas_call(
        functools.partial(ctx_kernel, off=off, n_ctx=n_ctx),
        out_shape=jax.ShapeDtypeStruct((r, width), BF16),
        grid=(n_heads,),
        in_specs=param_specs + [
            pl.BlockSpec((n_ctx, LANES), lambda h: (ctx_blk, h)),
            pl.BlockSpec((None, None, LANES, tk), lambda h: (h // kv_group, n_chunks - 1, 0, 0)),
            pl.BlockSpec((n_ctx, LANES), lambda h: (ctx_blk, v_off // LANES + h // kv_group)),
        ],
        out_specs=pl.BlockSpec((n_ctx, LANES), lambda h: (ctx_blk, h)),
        input_output_aliases={len(params): 0},
        compiler_params=_cparams("parallel"),
        name=name + "_ctx",
    )(*params, out, kt, proj)


def _diff_attention(lam_params, subln_w, layer, qa, kat, proj, n_lat, lam_init):
    return _attention((lam_params, subln_w), layer, qa, kat, proj, OFF_VA, 1, n_lat,
                      functools.partial(_da_kernel, lam_init=lam_init),
                      functools.partial(_da_ctx_kernel, lam_init=lam_init), "diff_attention")


def _gq_attention(qb, kbt, proj, n_lat):
    return _attention((), 0, qb, kbt, proj, OFF_VB, GQ_GROUP, n_lat, _gq_kernel, _gq_ctx_kernel, "gq_attention")


def _s5_matrices(a_re, a_im, log_dt, b_re, b_im, c_re, c_im, d_skip):
    t = S5_CHUNK
    g, p, h = S5_GROUPS, S5_STATE, S5_GROUP
    a_re = jnp.minimum(a_re.astype(F32), S5_MAX_REAL)
    a_im = a_im.astype(F32)
    dt = jnp.exp(log_dt.astype(F32))[..., None]
    mag = jnp.exp(a_re * dt)
    lr = mag * jnp.cos(a_im * dt)
    li = mag * jnp.sin(a_im * dt)
    den = jnp.square(a_re) + jnp.square(a_im)
    nr = lr - 1.0
    cr = (nr * a_re + li * a_im) / den
    ci = (li * a_re - nr * a_im) / den
    bbr = cr[..., None] * b_re - ci[..., None] * b_im
    bbi = cr[..., None] * b_im + ci[..., None] * b_re
    j = jnp.arange(t + 1, dtype=F32)
    pmag = jnp.exp((a_re * dt)[..., None] * j)
    pr = pmag * jnp.cos((a_im * dt)[..., None] * j)
    pi = pmag * jnp.sin((a_im * dt)[..., None] * j)
    ct_re = c_re.astype(F32).transpose(0, 1, 3, 2)[:, :, :, None, :]
    ct_im = c_im.astype(F32).transpose(0, 1, 3, 2)[:, :, :, None, :]
    clr = ct_re * pr[..., None] - ct_im * pi[..., None]
    cli = ct_re * pi[..., None] + ct_im * pr[..., None]
    kern = (jnp.einsum('dgpjh,dgpk->dgkjh', clr[:, :, :, :t], bbr)
            - jnp.einsum('dgpjh,dgpk->dgkjh', cli[:, :, :, :t], bbi))
    eye = jnp.eye(h, dtype=F32)[None] * d_skip.astype(F32).reshape(g, h)[:, None, :]
    k0 = kern[0][:, :, 0] + kern[1][:, :, 0] + eye
    kk = jnp.concatenate([kern[1][:, :, :0:-1], k0[:, :, None], kern[0][:, :, 1:]], axis=2)
    kk = kk.reshape(g, h, (2 * t - 1) * h)
    m = jnp.stack([kk[:, :, (t - 1 - s) * h:(2 * t - 1 - s) * h] for s in range(t)], axis=1)
    m = m.reshape(g, t * h, t * h)

    def to_state(d, pw_r, pw_i):
        pw_r = pw_r.transpose(0, 2, 1)[:, :, None, :]
        pw_i = pw_i.transpose(0, 2, 1)[:, :, None, :]
        br = bbr[d].transpose(0, 2, 1)[:, None, :, :]
        bi = bbi[d].transpose(0, 2, 1)[:, None, :, :]
        return (pw_r * br - pw_i * bi).reshape(g, t * h, p), (pw_r * bi + pw_i * br).reshape(g, t * h, p)

    wf = to_state(0, pr[0, :, :, :t][..., ::-1], pi[0, :, :, :t][..., ::-1])
    wb = to_state(1, pr[1, :, :, :t], pi[1, :, :, :t])
    w = jnp.concatenate([wf[0], wf[1], wb[0], wb[1]], axis=-1)

    flat = lambda c: c.reshape(g, p, t * h)
    n = jnp.concatenate([flat(clr[0][:, :, 1:]), -flat(cli[0][:, :, 1:]),
                         flat(clr[1][:, :, :0:-1]), -flat(cli[1][:, :, :0:-1])],
                        axis=1)

    def coef(d):
        r_, i_ = pr[d, :, :, t], pi[d, :, :, t]
        return [jnp.concatenate([r_, r_], -1), jnp.concatenate([-i_, i_], -1), jnp.concatenate([i_, -i_], -1)]

    coefs = jnp.stack(coef(0) + coef(1))
    return m.astype(BF16), w.astype(BF16), n.astype(BF16), coefs


def _s5_lane_perm():
    i = np.arange(S5_BLOCK_GROUPS * S5_CW)
    tok, grp, ch = i // LANES, (i // S5_GROUP) % S5_BLOCK_GROUPS, i % S5_GROUP
    perm = np.zeros((i.size, i.size), np.float32)
    perm[i, grp * S5_CW + tok * S5_GROUP + ch] = 1.0
    return perm


def _s5_a_kernel(*refs):
    u_refs = refs[:S5_CHUNK]
    p_ref, m_ref, w_ref, y_ref, s_ref = refs[S5_CHUNK:]
    u_nat = jnp.concatenate([u[...] for u in u_refs], axis=1)
    ug = jnp.dot(u_nat, p_ref[...], preferred_element_type=F32).astype(BF16)
    for gl in range(S5_BLOCK_GROUPS):
        u_g = ug[:, gl * S5_CW:(gl + 1) * S5_CW]
        y_ref[gl] = jnp.dot(u_g, m_ref[gl], preferred_element_type=F32)
        s_ref[gl] = jnp.dot(u_g, w_ref[gl], preferred_element_type=F32)


def _s5_b_kernel(s_ref, coef_ref, x_ref, sw_scr, *, n_chunks, n_lat_chunks):
    half = 2 * S5_STATE
    n_ctx_chunks = n_chunks - n_lat_chunks
    fwd_sl, bwd_sl = slice(0, half), slice(half, 2 * half)

    def swap(c, carry):
        sw_scr[c, :, fwd_sl] = pltpu.roll(s_ref[c, :, fwd_sl], S5_STATE, 1)
        sw_scr[c, :, bwd_sl] = pltpu.roll(s_ref[c, :, bwd_sl], S5_STATE, 1)
        return carry

    lax.fori_loop(0, n_chunks, swap, 0, unroll=S5_SCAN_UNROLL)

    def step(sl, a, b, bt):
        def f(c, carry):
            x, xt = carry
            x_ref[c, :, sl] = x.astype(x_ref.dtype)
            return a * x + b * xt + s_ref[c, :, sl], a * xt + bt * x + sw_scr[c, :, sl]
        return f

    fwd = step(fwd_sl, coef_ref[0], coef_ref[1], coef_ref[2])
    bwd = step(bwd_sl, coef_ref[3], coef_ref[4], coef_ref[5])
    zero = jnp.zeros_like(coef_ref[0])
    loop = functools.partial(lax.fori_loop, unroll=S5_SCAN_UNROLL)
    carry = loop(0, n_ctx_chunks, lambda i, cr: fwd(n_lat_chunks + i, cr), (zero, zero))
    loop(0, n_lat_chunks, fwd, carry)
    carry = loop(0, n_ctx_chunks, lambda i, cr: bwd(n_chunks - 1 - i, cr), (zero, zero))
    loop(0, n_lat_chunks, lambda i, cr: bwd(n_lat_chunks - 1 - i, cr), carry)


def _s5_c_kernel(x_ref, n_ref, yi_ref, q_ref, y_ref):
    ys = [(yi_ref[gl] + jnp.dot(x_ref[gl], n_ref[gl], preferred_element_type=F32)).astype(BF16)
          for gl in range(S5_BLOCK_GROUPS)]
    y_ref[...] = jnp.dot(jnp.concatenate(ys, axis=1), q_ref[...], preferred_element_type=F32).astype(y_ref.dtype)


def _s5_mixer(u, m, w, ncat, coefs, layer, n_lat):
    r = u.shape[0]
    g, t, gb = S5_GROUPS, S5_CHUNK, S5_BLOCK_GROUPS
    nc = r // t
    sw = 4 * S5_STATE
    nblk = g // gb
    perm = _s5_lane_perm()
    p_in = jnp.asarray(perm, BF16)
    p_out = jnp.asarray(perm.T, BF16)
    u_rows = u.reshape(nc, t * S5_WIDTH)
    grp = lambda a, b: pl.BlockSpec((gb, a, b), lambda i: (i, 0, 0))
    wgrp = lambda a, b: pl.BlockSpec((None, gb, a, b), lambda i: (layer, i, 0, 0))
    perm_spec = pl.BlockSpec(perm.shape, lambda i: (0, 0), pipeline_mode=pl.Buffered(1))
    tok_specs = [pl.BlockSpec((nc, LANES), lambda i, s=s: (0, s * nblk + i)) for s in range(t)]
    y_intra, s = pl.pallas_call(
        _s5_a_kernel,
        out_shape=(jax.ShapeDtypeStruct((g, nc, S5_CW), F32), jax.ShapeDtypeStruct((g, nc, sw), F32)),
        grid=(nblk,),
        in_specs=tok_specs + [perm_spec, wgrp(S5_CW, S5_CW), wgrp(S5_CW, sw)],
        out_specs=(grp(nc, S5_CW), grp(nc, sw)),
        compiler_params=_cparams("parallel"),
        name="s5_chunk_local",
    )(*([u_rows] * t), p_in, m, w)

    gs = 16
    st = s.transpose(1, 0, 2)
    xt = pl.pallas_call(
        functools.partial(_s5_b_kernel, n_chunks=nc, n_lat_chunks=n_lat // t),
        out_shape=jax.ShapeDtypeStruct((nc, g, sw), BF16),
        grid=(g // gs,),
        in_specs=[pl.BlockSpec((nc, gs, sw), lambda i: (0, i, 0)),
                  pl.BlockSpec((None, 6, gs, 2 * S5_STATE), lambda i: (layer, 0, i, 0))],
        out_specs=pl.BlockSpec((nc, gs, sw), lambda i: (0, i, 0)),
        scratch_shapes=[pltpu.VMEM((nc, gs, sw), F32)],
        compiler_params=_cparams("parallel"),
        name="s5_chunk_scan",
    )(st, coefs)

    x_in = xt.transpose(1, 0, 2)
    y_blocks = pl.pallas_call(
        _s5_c_kernel,
        out_shape=jax.ShapeDtypeStruct((nblk, nc, gb * S5_CW), BF16),
        grid=(nblk,),
        in_specs=[grp(nc, sw), wgrp(sw, S5_CW), grp(nc, S5_CW), perm_spec],
        out_specs=pl.BlockSpec((None, nc, gb * S5_CW), lambda i: (i, 0, 0)),
        compiler_params=_cparams("parallel"),
        name="s5_state_out",
    )(x_in, ncat, y_intra, p_out)
    return y_blocks.reshape(nblk, nc, t, LANES).transpose(1, 2, 0, 3).reshape(r, S5_WIDTH)


def _merge_kernel(oa_ref, ob_ref, y_ref, wpa_ref, wpb_ref, wga_ref, wgg_ref, g0_ref, g1_ref, g2_ref,
                  o_ref, gy_scr):
    @pl.when(pl.program_id(1) == 0)
    def _():
        gy_scr[...] = _gelu_tanh(y_ref[...].astype(F32)).astype(BF16)

    dot = lambda a, b: jnp.dot(a, b, preferred_element_type=F32)
    for rows in _row_halves(o_ref.shape[0]):
        pa = dot(oa_ref[rows, :], wpa_ref[...])
        pb = dot(ob_ref[rows, :], wpb_ref[...])
        gy = gy_scr[rows, :]
        pc = dot(gy, wga_ref[...]) * _sigmoid(dot(gy, wgg_ref[...]))
        sg = lambda ref: _sigmoid(ref[rows, :].astype(F32))
        o_ref[rows, :] = (sg(g0_ref) * pa + sg(g1_ref) * pb + sg(g2_ref) * pc).astype(o_ref.dtype)


def _merge(o_a, o_b, y_s5, proj, w_pa, w_pb, w_glu, layer, d):
    r = o_a.shape[0]
    tm = _tile(r, 1056, 16)
    tn = _tile(math.gcd(d, OFF_GATE), 512, LANES)
    row = lambda w: pl.BlockSpec((tm, w), lambda i, j: (i, 0))
    col = lambda k, off: pl.BlockSpec((None, k, tn), lambda i, j: (layer, 0, off // tn + j))
    gate = lambda b: pl.BlockSpec((tm, tn), lambda i, j: (i, (OFF_GATE + b * d) // tn + j))
    return pl.pallas_call(
        _merge_kernel,
        out_shape=jax.ShapeDtypeStruct((r, d), BF16),
        grid=(r // tm, d // tn),
        in_specs=[row(DA_WIDTH), row(GQ_WIDTH), row(S5_WIDTH),
                  col(DA_WIDTH, 0), col(GQ_WIDTH, 0), col(S5_WIDTH, 0), col(S5_WIDTH, d),
                  gate(0), gate(1), gate(2)],
        out_specs=pl.BlockSpec((tm, tn), lambda i, j: (i, j)),
        scratch_shapes=[pltpu.VMEM((tm, S5_WIDTH), BF16)],
        compiler_params=_cparams("parallel", "arbitrary"),
        name="branch_merge",
    )(o_a, o_b, y_s5, w_pa, w_pb, w_glu, w_glu, proj, proj, proj)


def _out_ln_kernel(m_ref, w_ref, x_ref, gm_ref, lg_ref, lb_ref, o_ref, *, tm, n_lat, alpha):
    for rows in _row_halves(tm):
        n_rows = rows.stop - rows.start
        y = jnp.dot(m_ref[rows, :], w_ref[...], preferred_element_type=F32)
        gate = _row_select(gm_ref, pl.program_id(0) * tm + rows.start, n_rows, n_lat)
        o_ref[rows, :] = _layer_norm(alpha * x_ref[rows, :] + gate * y, lg_ref[...], lb_ref[...])


def _out_ln(merged, w_o, layer, xs, mods, ln_g, ln_b, n_lat, alpha):
    r, d = xs.shape
    tm = _tile(r, 768, 32)
    return pl.pallas_call(
        functools.partial(_out_ln_kernel, tm=tm, n_lat=n_lat, alpha=alpha),
        out_shape=jax.ShapeDtypeStruct((r, d), F32),
        grid=(r // tm,),
        in_specs=[
            pl.BlockSpec((tm, d), lambda i: (i, 0)),
            pl.BlockSpec((None, d, d), lambda i: (layer, 0, 0), pipeline_mode=pl.Buffered(1)),
            pl.BlockSpec((tm, d), lambda i: (i, 0)),
            _layer_spec((8, d), layer, 2),
            _layer_spec((1, d), layer), _layer_spec((1, d), layer),
        ],
        out_specs=pl.BlockSpec((tm, d), lambda i: (i, 0)),
        compiler_params=_cparams("parallel"),
        name="out_proj_ln",
    )(merged, w_o, xs, mods, ln_g, ln_b)


def _ffn_kernel(x_ref, sh_ref, sc_ref, gf_ref, wg_ref, wu_ref, wd_ref, lg_ref, lb_ref, o_ref,
                h_scr, *, tm, n_lat, alpha):
    i = pl.program_id(0)
    f = pl.program_id(1)

    @pl.when(f == 0)
    def _():
        sh = _row_select(sh_ref, i * tm, tm, n_lat)
        sc = _row_select(sc_ref, i * tm, tm, n_lat)
        h_scr[...] = (x_ref[...] * (1.0 + sc) + sh).astype(BF16)
        o_ref[...] = jnp.zeros_like(o_ref)

    h = h_scr[...]
    gate = jnp.dot(h, wg_ref[...].astype(BF16), preferred_element_type=F32)
    up = jnp.dot(h, wu_ref[...].astype(BF16), preferred_element_type=F32)
    act = (gate * _sigmoid(gate) * up).astype(BF16)
    o_ref[...] += jnp.dot(act, wd_ref[...].astype(BF16), preferred_element_type=F32)

    @pl.when(f == pl.num_programs(1) - 1)
    def _():
        g = _row_select(gf_ref, i * tm, tm, n_lat)
        o_ref[...] = _layer_norm(alpha * x_ref[...] + g * o_ref[...], lg_ref[...], lb_ref[...])


def _ffn(xs, r, mods, w_gate, w_up, w_down, layer, ln_g, ln_b, n_lat, alpha):
    d = xs.shape[1]
    f = w_gate.shape[2]
    tm = _tile(r, 1056, 16)
    tf = _tile(f, MXU_DIM, LANES)
    modv = lambda c: _layer_spec((8, d), layer, c)
    return pl.pallas_call(
        functools.partial(_ffn_kernel, tm=tm, n_lat=n_lat, alpha=alpha),
        out_shape=jax.ShapeDtypeStruct((r, d), F32),
        grid=(r // tm, f // tf),
        in_specs=[
            pl.BlockSpec((tm, d), lambda i, j: (i, 0), pipeline_mode=pl.Buffered(1)),
            modv(3), modv(4), modv(5),
            pl.BlockSpec((None, d, tf), lambda i, j: (layer, 0, j)),
            pl.BlockSpec((None, d, tf), lambda i, j: (layer, 0, j)),
            pl.BlockSpec((None, tf, d), lambda i, j: (layer, j, 0)),
            _layer_spec((1, d), layer), _layer_spec((1, d), layer),
        ],
        out_specs=pl.BlockSpec((tm, d), lambda i, j: (i, 0)),
        scratch_shapes=[pltpu.VMEM((tm, d), BF16)],
        compiler_params=_cparams("parallel", "arbitrary"),
        name="swiglu_ln",
    )(xs, mods, mods, mods, w_gate, w_up, w_down, ln_g, ln_b)


def kernel(x, c, ctx, c_ctx, w_ada, b_ada, w_in, da_lam_q1, da_lam_k1, da_lam_q2, da_lam_k2, da_subln_w, w_pa, gq_qnorm_w, gq_knorm_w, w_pb, s5_a_re, s5_a_im, s5_log_dt, s5_b_re, s5_b_im, s5_c_re, s5_c_im, s5_d, w_glu, w_o, ln_mix_g, ln_mix_b, w_ffn_gate, w_ffn_up, w_ffn_down, ln_ffn_g, ln_ffn_b):
    batch, n_lat, d = x.shape
    n_ctx = ctx.shape[1]
    depth = w_in.shape[0]
    r = n_lat + n_ctx
    assert batch == 1 and n_lat % GRID_W == 0 and n_ctx % S5_CHUNK == 0 and n_lat % S5_CHUNK == 0
    assert n_lat % n_ctx == 0
    alpha = (2 * depth) ** 0.25
    tk = _tile(r, 768, MXU_DIM)
    assert n_ctx <= tk and (n_lat - (r // tk - 1) * tk) % LANES == 0

    xs = jnp.concatenate([x[0], ctx[0]], axis=0).astype(F32)
    c_rows = jnp.zeros((8, d), F32).at[0].set(c[0]).at[1].set(c_ctx)
    mods = _ada_mod(c_rows, w_ada, b_ada)
    tab_da = _rope_tables(n_lat, n_ctx, DA_HEAD_DIM)
    tab_gq = _rope_tables(n_lat, n_ctx, GQ_HEAD_DIM)

    w_pa, w_pb, w_glu, w_o = (w.astype(BF16) for w in (w_pa, w_pb, w_glu, w_o))
    s5_m, s5_w, s5_n, s5_coefs = jax.vmap(_s5_matrices)(
        s5_a_re, s5_a_im, s5_log_dt, s5_b_re, s5_b_im, s5_c_re, s5_c_im, s5_d)

    row = lambda v: v.astype(F32)[:, None, :]
    lam_params = jnp.stack([da_lam_q1, da_lam_k1, da_lam_q2, da_lam_k2], axis=1).astype(F32)
    subln_w, qn_w, kn_w = row(da_subln_w), row(gq_qnorm_w), row(gq_knorm_w)
    ln_mix_g, ln_mix_b, ln_ffn_g, ln_ffn_b = row(ln_mix_g), row(ln_mix_b), row(ln_ffn_g), row(ln_ffn_b)

    for l in range(depth):
        lam_init = 0.8 - 0.6 * math.exp(-0.3 * l)
        proj = _in_proj(xs, mods, w_in, l, n_lat)
        qa, kat, qb, kbt = _prep_qk(proj, tab_da, tab_gq, qn_w, kn_w, l, tk)
        o_a = _diff_attention(lam_params, subln_w, l, qa, kat, proj, n_lat, lam_init)
        o_b = _gq_attention(qb, kbt, proj, n_lat)
        y_s5 = _s5_mixer(proj[:, OFF_U:OFF_U + S5_WIDTH], s5_m, s5_w, s5_n, s5_coefs, l, n_lat)
        merged = _merge(o_a, o_b, y_s5, proj, w_pa, w_pb, w_glu, l, d)
        xs = _out_ln(merged, w_o, l, xs, mods, ln_mix_g, ln_mix_b, n_lat, alpha)
        rows_out = n_lat if l == depth - 1 else r
        xs = _ffn(xs, rows_out, mods, w_ffn_gate, w_ffn_up, w_ffn_down, l, ln_ffn_g, ln_ffn_b, n_lat, alpha)
    return xs[None].astype(x.dtype)
```

```python
import functools
import math

import jax
import jax.numpy as jnp
import numpy as np
from jax import lax
from jax.experimental import pallas as pl
from jax.experimental.pallas import tpu as pltpu

F32 = jnp.float32
BF16 = jnp.bfloat16

GRID_W = 64
DA_HEADS = 8
DA_HEAD_DIM = 64
DA_WIDTH = DA_HEADS * 2 * DA_HEAD_DIM
GQ_HEADS = 8
GQ_KV_HEADS = 2
GQ_GROUP = GQ_HEADS // GQ_KV_HEADS
GQ_HEAD_DIM = 128
GQ_WIDTH = GQ_HEADS * GQ_HEAD_DIM
GQ_KV_WIDTH = GQ_KV_HEADS * GQ_HEAD_DIM
S5_WIDTH = 1024
S5_GROUP = 16
S5_GROUPS = S5_WIDTH // S5_GROUP
S5_STATE = 64
S5_MAX_REAL = -1e-4
N_BRANCH = 3
ROPE_THETA = 10000.0
LN_EPS = 1e-6
RMS_EPS = 1e-6
DA_SUBLN_EPS = 1e-5

OFF_QA = 0
OFF_KA = OFF_QA + DA_WIDTH
OFF_VA = OFF_KA + DA_WIDTH
OFF_QB = OFF_VA + DA_WIDTH
OFF_KB = OFF_QB + GQ_WIDTH
OFF_VB = OFF_KB + GQ_KV_WIDTH
OFF_U = OFF_VB + GQ_KV_WIDTH
OFF_GATE = OFF_U + S5_WIDTH

LANES = 128
MXU_DIM = 256
V7X_VMEM_BYTES = 64 * 1024 * 1024
VMEM_LIMIT = V7X_VMEM_BYTES - 8 * 1024 * 1024

S5_CHUNK = 16
S5_CW = S5_CHUNK * S5_GROUP
S5_BLOCK_GROUPS = LANES // S5_GROUP
S5_SCAN_UNROLL = 8

ATTN_ROWS_PER_STEP = 512
LOG2E = math.log2(math.e)
NEG_BIG = -1e30


def _cparams(*sem):
    return pltpu.CompilerParams(dimension_semantics=sem, vmem_limit_bytes=VMEM_LIMIT)


def _tile(n, target, mult):
    best = None
    for t in range(mult, min(n, target) + 1, mult):
        if n % t == 0:
            best = t
    assert best is not None, (n, target, mult)
    return best


def _row_halves(tm):
    if tm % 32 == 0:
        return [slice(0, tm // 2), slice(tm // 2, tm)]
    return [slice(0, tm)]


def _layer_spec(block_shape, layer, col=0):
    return pl.BlockSpec((None,) + tuple(block_shape), lambda *_: (layer, 0, col))


def _sigmoid(x):
    return 1.0 / (1.0 + jnp.exp(-x))


def _gelu_tanh(x):
    return 0.5 * x * (1.0 + jnp.tanh(math.sqrt(2.0 / math.pi) * (x + 0.044715 * (x * x * x))))


def _row_select(vec_ref, row0, tm, n_lat):
    rows = row0 + lax.broadcasted_iota(jnp.int32, (tm, 1), 0)
    return jnp.where(rows < n_lat, vec_ref[0:1, :], vec_ref[1:2, :])


def _layer_norm(z, g, b):
    mu = jnp.mean(z, axis=-1, keepdims=True)
    zc = z - mu
    var = jnp.mean(zc * zc, axis=-1, keepdims=True)
    return zc * lax.rsqrt(var + LN_EPS) * g + b


def _mod_kernel(c_ref, w_ref, b_ref, o_ref):
    a = c_ref[...]
    a = a * _sigmoid(a)
    o_ref[0] = jnp.dot(a.astype(BF16), w_ref[0].astype(BF16), preferred_element_type=F32) + b_ref[0]


def _ada_mod(c_rows, w_ada, b_ada):
    depth, d, n6 = w_ada.shape
    tn = _tile(n6, 1024, LANES)
    return pl.pallas_call(
        _mod_kernel,
        out_shape=jax.ShapeDtypeStruct((depth, 8, n6), F32),
        grid=(depth, n6 // tn),
        in_specs=[
            pl.BlockSpec((8, d), lambda l, j: (0, 0)),
            pl.BlockSpec((1, d, tn), lambda l, j: (l, 0, j)),
            pl.BlockSpec((1, 1, tn), lambda l, j: (l, 0, j)),
        ],
        out_specs=pl.BlockSpec((1, 8, tn), lambda l, j: (l, 0, j)),
        compiler_params=_cparams("parallel", "parallel"),
        name="ada_mod",
    )(c_rows, w_ada, b_ada.reshape(depth, 1, n6))


def _in_proj_kernel(x_ref, sh_ref, sc_ref, w_ref, o_ref, h_scr, *, tm, n_lat):
    i = pl.program_id(0)

    @pl.when(pl.program_id(1) == 0)
    def _():
        sh = _row_select(sh_ref, i * tm, tm, n_lat)
        sc = _row_select(sc_ref, i * tm, tm, n_lat)
        h_scr[...] = (x_ref[...] * (1.0 + sc) + sh).astype(BF16)

    o_ref[...] = jnp.dot(h_scr[...], w_ref[...].astype(BF16), preferred_element_type=F32).astype(o_ref.dtype)


def _in_proj(xs, mods, w_in, layer, n_lat):
    r, d = xs.shape
    n = w_in.shape[2]
    tm = _tile(r, 2112, 16)
    tn = _tile(n, 512, LANES)
    return pl.pallas_call(
        functools.partial(_in_proj_kernel, tm=tm, n_lat=n_lat),
        out_shape=jax.ShapeDtypeStruct((r, n), BF16),
        grid=(r // tm, n // tn),
        in_specs=[
            pl.BlockSpec((tm, d), lambda i, j: (i, 0), pipeline_mode=pl.Buffered(1)),
            _layer_spec((8, d), layer, 0),
            _layer_spec((8, d), layer, 1),
            pl.BlockSpec((None, d, tn), lambda i, j: (layer, 0, j)),
        ],
        out_specs=pl.BlockSpec((tm, tn), lambda i, j: (i, j)),
        scratch_shapes=[pltpu.VMEM((tm, d), BF16)],
        compiler_params=_cparams("parallel", "arbitrary"),
        name="in_proj",
    )(xs, mods, mods, w_in)


def _rope(z, cos, sin_a, sin_b, quarter):
    return z * cos + pltpu.roll(z, LANES - quarter, 1) * sin_a + pltpu.roll(z, quarter, 1) * sin_b


def _rms(z, w, eps):
    return z * lax.rsqrt(jnp.mean(z * z, axis=-1, keepdims=True) + eps) * w


def _prep_kernel(qa_ref, ka_ref, qb_ref, kb_ref, tda_ref, tgq_ref, qn_ref, kn_ref,
                 qa_o, kat_o, qb_o, kbt_o):
    cda, sda_a, sda_b = tda_ref[0], tda_ref[1], tda_ref[2]
    cgq, sgq_a, sgq_b = tgq_ref[0], tgq_ref[1], tgq_ref[2]
    da_scale = DA_HEAD_DIM ** -0.5 * LOG2E
    gq_scale = GQ_HEAD_DIM ** -0.5 * LOG2E
    for h in range(DA_HEADS):
        sl = slice(h * LANES, (h + 1) * LANES)
        z = qa_ref[:, sl].astype(F32)
        qa_o[:, sl] = (_rope(z, cda, sda_a, sda_b, DA_HEAD_DIM // 4) * da_scale).astype(BF16)
        z = ka_ref[:, sl].astype(F32)
        kat_o[h, 0] = _rope(z, cda, sda_a, sda_b, DA_HEAD_DIM // 4).T.astype(BF16)
    for h in range(GQ_HEADS):
        sl = slice(h * LANES, (h + 1) * LANES)
        z = _rms(qb_ref[:, sl].astype(F32), qn_ref[...], RMS_EPS)
        qb_o[:, sl] = (_rope(z, cgq, sgq_a, sgq_b, GQ_HEAD_DIM // 4) * gq_scale).astype(BF16)
    for h in range(GQ_KV_HEADS):
        sl = slice(h * LANES, (h + 1) * LANES)
        z = _rms(kb_ref[:, sl].astype(F32), kn_ref[...], RMS_EPS)
        kbt_o[h, 0] = _rope(z, cgq, sgq_a, sgq_b, GQ_HEAD_DIM // 4).T.astype(BF16)


def _prep_qk(proj, tab_da, tab_gq, qn_w, kn_w, layer, tk):
    r = proj.shape[0]
    nk = r // tk
    blk = lambda w, off: pl.BlockSpec((tk, w), lambda i: (i, off // w))
    out = lambda w: pl.BlockSpec((tk, w), lambda i: (i, 0))
    outt = lambda heads: pl.BlockSpec((heads, 1, LANES, tk), lambda i: (0, i, 0, 0))
    return pl.pallas_call(
        _prep_kernel,
        out_shape=(
            jax.ShapeDtypeStruct((r, DA_WIDTH), BF16),
            jax.ShapeDtypeStruct((DA_HEADS, nk, LANES, tk), BF16),
            jax.ShapeDtypeStruct((r, GQ_WIDTH), BF16),
            jax.ShapeDtypeStruct((GQ_KV_HEADS, nk, LANES, tk), BF16),
        ),
        grid=(nk,),
        in_specs=[
            blk(DA_WIDTH, OFF_QA), blk(DA_WIDTH, OFF_KA), blk(GQ_WIDTH, OFF_QB), blk(GQ_KV_WIDTH, OFF_KB),
            pl.BlockSpec((3, tk, LANES), lambda i: (0, i, 0)),
            pl.BlockSpec((3, tk, LANES), lambda i: (0, i, 0)),
            _layer_spec((1, LANES), layer),
            _layer_spec((1, LANES), layer),
        ],
        out_specs=(out(DA_WIDTH), outt(DA_HEADS), out(GQ_WIDTH), outt(GQ_KV_HEADS)),
        compiler_params=_cparams("parallel"),
        name="prep_qk",
    )(proj, proj, proj, proj, tab_da, tab_gq, qn_w, kn_w)


def _rope_tables(n_lat, n_ctx, dim):
    rows = n_lat // GRID_W
    row = np.repeat(np.arange(rows, dtype=np.float64), GRID_W)
    col = np.tile(np.arange(GRID_W, dtype=np.float64), rows)
    axis_dim = dim // 2
    inv_freq = ROPE_THETA ** (-np.arange(0, axis_dim, 2, dtype=np.float64) / axis_dim)
    ang_r = row[:, None] * inv_freq[None, :]
    ang_c = col[:, None] * inv_freq[None, :]
    ang = np.concatenate([ang_r, ang_r, ang_c, ang_c], axis=-1)
    cos, sin = np.cos(ang), np.sin(ang)
    first_half = (np.arange(dim) % (dim // 2)) < (dim // 4)
    sin_a = np.where(first_half[None, :], -sin, 0.0)
    sin_b = np.where(first_half[None, :], 0.0, sin)
    tab = np.stack([cos, sin_a, sin_b])
    ident = np.stack([np.ones((n_ctx, dim)), np.zeros((n_ctx, dim)), np.zeros((n_ctx, dim))])
    tab = np.concatenate([tab, ident], axis=1)
    return jnp.asarray(np.tile(tab, (1, 1, LANES // dim)), F32)


def _flash_loop(qs, kt_ref, v_ref, vx_scr, tk, n_chunks):
    @pl.when(pl.program_id(1) == 0)
    def _():
        vx_scr[:, :LANES] = v_ref[...]
        vx_scr[:, LANES:] = jnp.ones((vx_scr.shape[0], LANES), vx_scr.dtype)

    tq = qs[0].shape[0]
    scores = lambda c: [jnp.dot(q, kt_ref[c], preferred_element_type=F32) for q in qs]
    m = [jnp.full((tq, 1), NEG_BIG, F32) for _ in qs]
    acc = [jnp.zeros((tq, 2 * LANES), F32) for _ in qs]
    s_next = scores(0)
    for c in range(n_chunks):
        s_cur = s_next
        if c + 1 < n_chunks:
            s_next = scores(c + 1)
        vx = vx_scr[c * tk:(c + 1) * tk, :]
        for i, s in enumerate(s_cur):
            m_new = jnp.maximum(m[i], jnp.max(s, axis=-1, keepdims=True))
            p = jnp.exp2((s - m_new).astype(BF16))
            acc[i] = jnp.exp2(m[i] - m_new) * acc[i] + jnp.dot(p, vx, preferred_element_type=F32)
            m[i] = m_new
    return [a[:, :LANES] / a[:, LANES:] for a in acc]


def _softmax_once(s, v):
    p = jnp.exp2(s - jnp.max(s, axis=-1, keepdims=True))
    return jnp.dot(p.astype(BF16), v, preferred_element_type=F32) / jnp.sum(p, axis=-1, keepdims=True)


def _da_split(q):
    lane = lax.broadcasted_iota(jnp.int32, (1, LANES), 1)
    zero = jnp.zeros_like(q)
    return jnp.where(lane < DA_HEAD_DIM, q, zero), jnp.where(lane < DA_HEAD_DIM, zero, q)


def _da_finish(o1, o2, lam_ref, w_ref, o_ref, lam_init):
    lp = lam_ref[...]
    lam = (jnp.exp(jnp.sum(lp[0:1] * lp[1:2], axis=-1, keepdims=True))
           - jnp.exp(jnp.sum(lp[2:3] * lp[3:4], axis=-1, keepdims=True)) + lam_init)
    o = _rms(o1 - lam * o2, w_ref[...], DA_SUBLN_EPS) * (1.0 - lam_init)
    o_ref[...] = o.astype(o_ref.dtype)


def _da_kernel(lam_ref, w_ref, q_ref, kt_ref, v_ref, o_ref, vx, *, tk, n_chunks, lam_init):
    o1, o2 = _flash_loop(_da_split(q_ref[...]), kt_ref, v_ref, vx, tk, n_chunks)
    _da_finish(o1, o2, lam_ref, w_ref, o_ref, lam_init)


def _da_ctx_kernel(lam_ref, w_ref, q_ref, kt_ref, v_ref, o_ref, *, off, n_ctx, lam_init):
    q1, q2 = _da_split(q_ref[...])
    kt = kt_ref[:, off:off + n_ctx]
    v = v_ref[...]
    o1 = _softmax_once(jnp.dot(q1, kt, preferred_element_type=F32), v)
    o2 = _softmax_once(jnp.dot(q2, kt, preferred_element_type=F32), v)
    _da_finish(o1, o2, lam_ref, w_ref, o_ref, lam_init)


def _gq_kernel(q_ref, kt_ref, v_ref, o_ref, vx, *, tk, n_chunks):
    (o,) = _flash_loop((q_ref[...],), kt_ref, v_ref, vx, tk, n_chunks)
    o_ref[...] = o.astype(o_ref.dtype)


def _gq_ctx_kernel(q_ref, kt_ref, v_ref, o_ref, *, off, n_ctx):
    kt = kt_ref[:, off:off + n_ctx]
    o_ref[...] = _softmax_once(jnp.dot(q_ref[...], kt, preferred_element_type=F32), v_ref[...]).astype(o_ref.dtype)


def _attention(params, layer, q, kt, proj, v_off, kv_group, n_lat, main_kernel, ctx_kernel, name, tq_rows):
    r, width = q.shape
    n_heads = width // LANES
    n_ctx = r - n_lat
    _, n_chunks, _, tk = kt.shape
    tq = _tile(n_lat, tq_rows, 16)
    ctx_blk = n_lat // n_ctx
    off = n_lat - (n_chunks - 1) * tk
    param_specs = [_layer_spec(p.shape[1:], layer) for p in params]
    out = pl.pallas_call(
        functools.partial(main_kernel, tk=tk, n_chunks=n_chunks),
        out_shape=jax.ShapeDtypeStruct((r, width), BF16),
        grid=(n_heads, n_lat // tq),
        in_specs=param_specs + [
            pl.BlockSpec((tq, LANES), lambda h, i: (i, h)),
            pl.BlockSpec((None, n_chunks, LANES, tk), lambda h, i: (h // kv_group, 0, 0, 0)),
            pl.BlockSpec((r, LANES), lambda h, i: (0, v_off // LANES + h // kv_group)),
        ],
        out_specs=pl.BlockSpec((tq, LANES), lambda h, i: (i, h)),
        scratch_shapes=[pltpu.VMEM((r, 2 * LANES), BF16)],
        input_output_aliases={len(params): 0},
        compiler_params=_cparams("parallel", "arbitrary"),
        name=name,
    )(*params, q, kt, proj)
    return pl.pallas_call(
        functools.partial(ctx_kernel, off=off, n_ctx=n_ctx),
        out_shape=jax.ShapeDtypeStruct((r, width), BF16),
        grid=(n_heads,),
        in_specs=param_specs + [
            pl.BlockSpec((n_ctx, LANES), lambda h: (ctx_blk, h)),
            pl.BlockSpec((None, None, LANES, tk), lambda h: (h // kv_group, n_chunks - 1, 0, 0)),
            pl.BlockSpec((n_ctx, LANES), lambda h: (ctx_blk, v_off // LANES + h // kv_group)),
        ],
        out_specs=pl.BlockSpec((n_ctx, LANES), lambda h: (ctx_blk, h)),
        input_output_aliases={len(params): 0},
        compiler_params=_cparams("parallel"),
        name=name + "_ctx",
    )(*params, out, kt, proj)


def _diff_attention(lam_params, subln_w, layer, qa, kat, proj, n_lat, lam_init):
    return _attention((lam_params, subln_w), layer, qa, kat, proj, OFF_VA, 1, n_lat,
                      functools.partial(_da_kernel, lam_init=lam_init),
                      functools.partial(_da_ctx_kernel, lam_init=lam_init), "diff_attention", ATTN_ROWS_PER_STEP // 2)


def _gq_attention(qb, kbt, proj, n_lat):
    return _attention((), 0, qb, kbt, proj, OFF_VB, GQ_GROUP, n_lat, _gq_kernel, _gq_ctx_kernel, "gq_attention",
                      ATTN_ROWS_PER_STEP)


def _s5_matrices(a_re, a_im, log_dt, b_re, b_im, c_re, c_im, d_skip):
    t = S5_CHUNK
    g, p, h = S5_GROUPS, S5_STATE, S5_GROUP
    a_re = jnp.minimum(a_re.astype(F32), S5_MAX_REAL)
    a_im = a_im.astype(F32)
    dt = jnp.exp(log_dt.astype(F32))[..., None]
    mag = jnp.exp(a_re * dt)
    lr = mag * jnp.cos(a_im * dt)
    li = mag * jnp.sin(a_im * dt)
    den = jnp.square(a_re) + jnp.square(a_im)
    nr = lr - 1.0
    cr = (nr * a_re + li * a_im) / den
    ci = (li * a_re - nr * a_im) / den
    bbr = cr[..., None] * b_re - ci[..., None] * b_im
    bbi = cr[..., None] * b_im + ci[..., None] * b_re
    j = jnp.arange(t + 1, dtype=F32)
    pmag = jnp.exp((a_re * dt)[..., None] * j)
    pr = pmag * jnp.cos((a_im * dt)[..., None] * j)
    pi = pmag * jnp.sin((a_im * dt)[..., None] * j)
    ct_re = c_re.astype(F32).transpose(0, 1, 3, 2)[:, :, :, None, :]
    ct_im = c_im.astype(F32).transpose(0, 1, 3, 2)[:, :, :, None, :]
    clr = ct_re * pr[..., None] - ct_im * pi[..., None]
    cli = ct_re * pi[..., None] + ct_im * pr[..., None]
    kern = (jnp.einsum('dgpjh,dgpk->dgkjh', clr[:, :, :, :t], bbr)
            - jnp.einsum('dgpjh,dgpk->dgkjh', cli[:, :, :, :t], bbi))
    eye = jnp.eye(h, dtype=F32)[None] * d_skip.astype(F32).reshape(g, h)[:, None, :]
    k0 = kern[0][:, :, 0] + kern[1][:, :, 0] + eye
    kk = jnp.concatenate([kern[1][:, :, :0:-1], k0[:, :, None], kern[0][:, :, 1:]], axis=2)
    kk = kk.reshape(g, h, (2 * t - 1) * h)
    m = jnp.stack([kk[:, :, (t - 1 - s) * h:(2 * t - 1 - s) * h] for s in range(t)], axis=1)
    m = m.reshape(g, t * h, t * h)

    def to_state(d, pw_r, pw_i):
        pw_r = pw_r.transpose(0, 2, 1)[:, :, None, :]
        pw_i = pw_i.transpose(0, 2, 1)[:, :, None, :]
        br = bbr[d].transpose(0, 2, 1)[:, None, :, :]
        bi = bbi[d].transpose(0, 2, 1)[:, None, :, :]
        return (pw_r * br - pw_i * bi).reshape(g, t * h, p), (pw_r * bi + pw_i * br).reshape(g, t * h, p)

    wf = to_state(0, pr[0, :, :, :t][..., ::-1], pi[0, :, :, :t][..., ::-1])
    wb = to_state(1, pr[1, :, :, :t], pi[1, :, :, :t])
    w = jnp.concatenate([wf[0], wf[1], wb[0], wb[1]], axis=-1)

    flat = lambda c: c.reshape(g, p, t * h)
    n = jnp.concatenate([flat(clr[0][:, :, 1:]), -flat(cli[0][:, :, 1:]),
                         flat(clr[1][:, :, :0:-1]), -flat(cli[1][:, :, :0:-1])],
                        axis=1)

    def coef(d):
        r_, i_ = pr[d, :, :, t], pi[d, :, :, t]
        return [jnp.concatenate([r_, r_], -1), jnp.concatenate([-i_, i_], -1), jnp.concatenate([i_, -i_], -1)]

    coefs = jnp.stack(coef(0) + coef(1))
    return m.astype(BF16), w.astype(BF16), n.astype(BF16), coefs


def _s5_lane_perm():
    i = np.arange(S5_BLOCK_GROUPS * S5_CW)
    tok, grp, ch = i // LANES, (i // S5_GROUP) % S5_BLOCK_GROUPS, i % S5_GROUP
    perm = np.zeros((i.size, i.size), np.float32)
    perm[i, grp * S5_CW + tok * S5_GROUP + ch] = 1.0
    return perm


def _s5_a_kernel(*refs):
    u_refs = refs[:S5_CHUNK]
    p_ref, m_ref, w_ref, y_ref, s_ref = refs[S5_CHUNK:]
    u_nat = jnp.concatenate([u[...] for u in u_refs], axis=1)
    ug = jnp.dot(u_nat, p_ref[...], preferred_element_type=F32).astype(BF16)
    for gl in range(S5_BLOCK_GROUPS):
        u_g = ug[:, gl * S5_CW:(gl + 1) * S5_CW]
        y_ref[gl] = jnp.dot(u_g, m_ref[gl], preferred_element_type=F32)
        s_ref[gl] = jnp.dot(u_g, w_ref[gl], preferred_element_type=F32)


def _s5_b_kernel(s_ref, coef_ref, x_ref, *, n_chunks, n_lat_chunks):
    half = 2 * S5_STATE
    n_ctx_chunks = n_chunks - n_lat_chunks
    fwd_sl, bwd_sl = slice(0, half), slice(half, 2 * half)

    def step(sl, a, b, bt):
        def f(c, carry):
            x, xt = carry
            s = s_ref[c, :, sl]
            x_ref[c, :, sl] = x.astype(x_ref.dtype)
            return a * x + b * xt + s, a * xt + bt * x + pltpu.roll(s, S5_STATE, 1)
        return f

    fwd = step(fwd_sl, coef_ref[0], coef_ref[1], coef_ref[2])
    bwd = step(bwd_sl, coef_ref[3], coef_ref[4], coef_ref[5])
    zero = jnp.zeros_like(coef_ref[0])
    loop = functools.partial(lax.fori_loop, unroll=S5_SCAN_UNROLL)

    def both(f_first, b_first):
        return lambda i, cr: (fwd(f_first + i, cr[0]), bwd(b_first - i, cr[1]))

    carry = loop(0, n_ctx_chunks, both(n_lat_chunks, n_chunks - 1), ((zero, zero), (zero, zero)))
    loop(0, n_lat_chunks, both(0, n_lat_chunks - 1), carry)


def _s5_c_kernel(x_ref, n_ref, yi_ref, q_ref, y_ref):
    ys = [(yi_ref[gl] + jnp.dot(x_ref[gl], n_ref[gl], preferred_element_type=F32)).astype(BF16)
          for gl in range(S5_BLOCK_GROUPS)]
    y_ref[...] = jnp.dot(jnp.concatenate(ys, axis=1), q_ref[...], preferred_element_type=F32).astype(y_ref.dtype)


def _s5_mixer(u, m, w, ncat, coefs, layer, n_lat):
    r = u.shape[0]
    g, t, gb = S5_GROUPS, S5_CHUNK, S5_BLOCK_GROUPS
    nc = r // t
    sw = 4 * S5_STATE
    nblk = g // gb
    perm = _s5_lane_perm()
    p_in = jnp.asarray(perm, BF16)
    p_out = jnp.asarray(perm.T, BF16)
    u_rows = u.reshape(nc, t * S5_WIDTH)
    grp = lambda a, b: pl.BlockSpec((gb, a, b), lambda i: (i, 0, 0))
    wgrp = lambda a, b: pl.BlockSpec((None, gb, a, b), lambda i: (layer, i, 0, 0))
    perm_spec = pl.BlockSpec(perm.shape, lambda i: (0, 0), pipeline_mode=pl.Buffered(1))
    tok_specs = [pl.BlockSpec((nc, LANES), lambda i, s=s: (0, s * nblk + i)) for s in range(t)]
    y_intra, s = pl.pallas_call(
        _s5_a_kernel,
        out_shape=(jax.ShapeDtypeStruct((g, nc, S5_CW), F32), jax.ShapeDtypeStruct((g, nc, sw), F32)),
        grid=(nblk,),
        in_specs=tok_specs + [perm_spec, wgrp(S5_CW, S5_CW), wgrp(S5_CW, sw)],
        out_specs=(grp(nc, S5_CW), grp(nc, sw)),
        compiler_params=_cparams("parallel"),
        name="s5_chunk_local",
    )(*([u_rows] * t), p_in, m, w)

    gs = 16
    st = s.transpose(1, 0, 2)
    xt = pl.pallas_call(
        functools.partial(_s5_b_kernel, n_chunks=nc, n_lat_chunks=n_lat // t),
        out_shape=jax.ShapeDtypeStruct((nc, g, sw), BF16),
        grid=(g // gs,),
        in_specs=[pl.BlockSpec((nc, gs, sw), lambda i: (0, i, 0)),
                  pl.BlockSpec((None, 6, gs, 2 * S5_STATE), lambda i: (layer, 0, i, 0))],
        out_specs=pl.BlockSpec((nc, gs, sw), lambda i: (0, i, 0)),
        compiler_params=_cparams("parallel"),
        name="s5_chunk_scan",
    )(st, coefs)

    x_in = xt.transpose(1, 0, 2)
    y_blocks = pl.pallas_call(
        _s5_c_kernel,
        out_shape=jax.ShapeDtypeStruct((nblk, nc, gb * S5_CW), BF16),
        grid=(nblk,),
        in_specs=[grp(nc, sw), wgrp(sw, S5_CW), grp(nc, S5_CW), perm_spec],
        out_specs=pl.BlockSpec((None, nc, gb * S5_CW), lambda i: (i, 0, 0)),
        compiler_params=_cparams("parallel"),
        name="s5_state_out",
    )(x_in, ncat, y_intra, p_out)
    return y_blocks.reshape(nblk, nc, t, LANES).transpose(1, 2, 0, 3).reshape(r, S5_WIDTH)


def _merge_kernel(oa_ref, ob_ref, y_ref, wpa_ref, wpb_ref, wga_ref, wgg_ref, g0_ref, g1_ref, g2_ref,
                  o_ref, gy_scr):
    @pl.when(pl.program_id(1) == 0)
    def _():
        gy_scr[...] = _gelu_tanh(y_ref[...].astype(F32)).astype(BF16)

    dot = lambda a, b: jnp.dot(a, b, preferred_element_type=F32)
    for rows in _row_halves(o_ref.shape[0]):
        pa = dot(oa_ref[rows, :], wpa_ref[...])
        pb = dot(ob_ref[rows, :], wpb_ref[...])
        gy = gy_scr[rows, :]
        pc = dot(gy, wga_ref[...]) * _sigmoid(dot(gy, wgg_ref[...]))
        sg = lambda ref: _sigmoid(ref[rows, :].astype(F32))
        o_ref[rows, :] = (sg(g0_ref) * pa + sg(g1_ref) * pb + sg(g2_ref) * pc).astype(o_ref.dtype)


def _merge(o_a, o_b, y_s5, proj, w_pa, w_pb, w_glu, layer, d):
    r = o_a.shape[0]
    tm = _tile(r, 1056, 16)
    tn = _tile(math.gcd(d, OFF_GATE), 512, LANES)
    row = lambda w: pl.BlockSpec((tm, w), lambda i, j: (i, 0))
    col = lambda k, off: pl.BlockSpec((None, k, tn), lambda i, j: (layer, 0, off // tn + j))
    gate = lambda b: pl.BlockSpec((tm, tn), lambda i, j: (i, (OFF_GATE + b * d) // tn + j))
    return pl.pallas_call(
        _merge_kernel,
        out_shape=jax.ShapeDtypeStruct((r, d), BF16),
        grid=(r // tm, d // tn),
        in_specs=[row(DA_WIDTH), row(GQ_WIDTH), row(S5_WIDTH),
                  col(DA_WIDTH, 0), col(GQ_WIDTH, 0), col(S5_WIDTH, 0), col(S5_WIDTH, d),
                  gate(0), gate(1), gate(2)],
        out_specs=pl.BlockSpec((tm, tn), lambda i, j: (i, j)),
        scratch_shapes=[pltpu.VMEM((tm, S5_WIDTH), BF16)],
        compiler_params=_cparams("parallel", "arbitrary"),
        name="branch_merge",
    )(o_a, o_b, y_s5, w_pa, w_pb, w_glu, w_glu, proj, proj, proj)


def _out_ln_kernel(m_ref, w_ref, x_ref, gm_ref, lg_ref, lb_ref, o_ref, *, tm, n_lat, alpha):
    for rows in _row_halves(tm):
        n_rows = rows.stop - rows.start
        y = jnp.dot(m_ref[rows, :], w_ref[...], preferred_element_type=F32)
        gate = _row_select(gm_ref, pl.program_id(0) * tm + rows.start, n_rows, n_lat)
        o_ref[rows, :] = _layer_norm(alpha * x_ref[rows, :] + gate * y, lg_ref[...], lb_ref[...])


def _out_ln(merged, w_o, layer, xs, mods, ln_g, ln_b, n_lat, alpha):
    r, d = xs.shape
    tm = _tile(r, 768, 32)
    return pl.pallas_call(
        functools.partial(_out_ln_kernel, tm=tm, n_lat=n_lat, alpha=alpha),
        out_shape=jax.ShapeDtypeStruct((r, d), F32),
        grid=(r // tm,),
        in_specs=[
            pl.BlockSpec((tm, d), lambda i: (i, 0)),
            pl.BlockSpec((None, d, d), lambda i: (layer, 0, 0), pipeline_mode=pl.Buffered(1)),
            pl.BlockSpec((tm, d), lambda i: (i, 0)),
            _layer_spec((8, d), layer, 2),
            _layer_spec((1, d), layer), _layer_spec((1, d), layer),
        ],
        out_specs=pl.BlockSpec((tm, d), lambda i: (i, 0)),
        compiler_params=_cparams("parallel"),
        name="out_proj_ln",
    )(merged, w_o, xs, mods, ln_g, ln_b)


def _ffn_kernel(x_ref, sh_ref, sc_ref, gf_ref, wg_ref, wu_ref, wd_ref, lg_ref, lb_ref, o_ref,
                h_scr, *, tm, n_lat, alpha):
    i = pl.program_id(0)
    f = pl.program_id(1)

    @pl.when(f == 0)
    def _():
        sh = _row_select(sh_ref, i * tm, tm, n_lat)
        sc = _row_select(sc_ref, i * tm, tm, n_lat)
        h_scr[...] = (x_ref[...] * (1.0 + sc) + sh).astype(BF16)
        o_ref[...] = jnp.zeros_like(o_ref)

    h = h_scr[...]
    gate = jnp.dot(h, wg_ref[...].astype(BF16), preferred_element_type=F32)
    up = jnp.dot(h, wu_ref[...].astype(BF16), preferred_element_type=F32)
    act = (gate * _sigmoid(gate) * up).astype(BF16)
    o_ref[...] += jnp.dot(act, wd_ref[...].astype(BF16), preferred_element_type=F32)

    @pl.when(f == pl.num_programs(1) - 1)
    def _():
        g = _row_select(gf_ref, i * tm, tm, n_lat)
        o_ref[...] = _layer_norm(alpha * x_ref[...] + g * o_ref[...], lg_ref[...], lb_ref[...])


def _ffn(xs, r, mods, w_gate, w_up, w_down, layer, ln_g, ln_b, n_lat, alpha):
    d = xs.shape[1]
    f = w_gate.shape[2]
    tm = _tile(r, 1056, 16)
    tf = _tile(f, MXU_DIM, LANES)
    modv = lambda c: _layer_spec((8, d), layer, c)
    return pl.pallas_call(
        functools.partial(_ffn_kernel, tm=tm, n_lat=n_lat, alpha=alpha),
        out_shape=jax.ShapeDtypeStruct((r, d), F32),
        grid=(r // tm, f // tf),
        in_specs=[
            pl.BlockSpec((tm, d), lambda i, j: (i, 0), pipeline_mode=pl.Buffered(1)),
            modv(3), modv(4), modv(5),
            pl.BlockSpec((None, d, tf), lambda i, j: (layer, 0, j)),
            pl.BlockSpec((None, d, tf), lambda i, j: (layer, 0, j)),
            pl.BlockSpec((None, tf, d), lambda i, j: (layer, j, 0)),
            _layer_spec((1, d), layer), _layer_spec((1, d), layer),
        ],
        out_specs=pl.BlockSpec((tm, d), lambda i, j: (i, 0)),
        scratch_shapes=[pltpu.VMEM((tm, d), BF16)],
        compiler_params=_cparams("parallel", "arbitrary"),
        name="swiglu_ln",
    )(xs, mods, mods, mods, w_gate, w_up, w_down, ln_g, ln_b)


def kernel(x, c, ctx, c_ctx, w_ada, b_ada, w_in, da_lam_q1, da_lam_k1, da_lam_q2, da_lam_k2, da_subln_w, w_pa, gq_qnorm_w, gq_knorm_w, w_pb, s5_a_re, s5_a_im, s5_log_dt, s5_b_re, s5_b_im, s5_c_re, s5_c_im, s5_d, w_glu, w_o, ln_mix_g, ln_mix_b, w_ffn_gate, w_ffn_up, w_ffn_down, ln_ffn_g, ln_ffn_b):
    batch, n_lat, d = x.shape
    n_ctx = ctx.shape[1]
    depth = w_in.shape[0]
    r = n_lat + n_ctx
    assert batch == 1 and n_lat % GRID_W == 0 and n_ctx % S5_CHUNK == 0 and n_lat % S5_CHUNK == 0
    assert n_lat % n_ctx == 0
    alpha = (2 * depth) ** 0.25
    tk = _tile(r, 768, MXU_DIM)
    assert n_ctx <= tk and (n_lat - (r // tk - 1) * tk) % LANES == 0

    xs = jnp.concatenate([x[0], ctx[0]], axis=0).astype(F32)
    c_rows = jnp.zeros((8, d), F32).at[0].set(c[0]).at[1].set(c_ctx)
    mods = _ada_mod(c_rows, w_ada, b_ada)
    tab_da = _rope_tables(n_lat, n_ctx, DA_HEAD_DIM)
    tab_gq = _rope_tables(n_lat, n_ctx, GQ_HEAD_DIM)

    w_pa, w_pb, w_glu, w_o = (w.astype(BF16) for w in (w_pa, w_pb, w_glu, w_o))
    s5_m, s5_w, s5_n, s5_coefs = jax.vmap(_s5_matrices)(
        s5_a_re, s5_a_im, s5_log_dt, s5_b_re, s5_b_im, s5_c_re, s5_c_im, s5_d)

    row = lambda v: v.astype(F32)[:, None, :]
    lam_params = jnp.stack([da_lam_q1, da_lam_k1, da_lam_q2, da_lam_k2], axis=1).astype(F32)
    subln_w, qn_w, kn_w = row(da_subln_w), row(gq_qnorm_w), row(gq_knorm_w)
    ln_mix_g, ln_mix_b, ln_ffn_g, ln_ffn_b = row(ln_mix_g), row(ln_mix_b), row(ln_ffn_g), row(ln_ffn_b)

    for l in range(depth):
        lam_init = 0.8 - 0.6 * math.exp(-0.3 * l)
        proj = _in_proj(xs, mods, w_in, l, n_lat)
        qa, kat, qb, kbt = _prep_qk(proj, tab_da, tab_gq, qn_w, kn_w, l, tk)
        o_a = _diff_attention(lam_params, subln_w, l, qa, kat, proj, n_lat, lam_init)
        o_b = _gq_attention(qb, kbt, proj, n_lat)
        y_s5 = _s5_mixer(proj[:, OFF_U:OFF_U + S5_WIDTH], s5_m, s5_w, s5_n, s5_coefs, l, n_lat)
        merged = _merge(o_a, o_b, y_s5, proj, w_pa, w_pb, w_glu, l, d)
        xs = _out_ln(merged, w_o, l, xs, mods, ln_mix_g, ln_mix_b, n_lat, alpha)
        rows_out = n_lat if l == depth - 1 else r
        xs = _ffn(xs, rows_out, mods, w_ffn_gate, w_ffn_up, w_ffn_down, l, ln_ffn_g, ln_ffn_b, n_lat, alpha)
    return xs[None].astype(x.dtype)
```

```python
import functools
import math

import jax
import jax.numpy as jnp
import numpy as np
from jax import lax
from jax.experimental import pallas as pl
from jax.experimental.pallas import tpu as pltpu

F32 = jnp.float32
BF16 = jnp.bfloat16

GRID_W = 64
DA_HEADS = 8
DA_HEAD_DIM = 64
DA_WIDTH = DA_HEADS * 2 * DA_HEAD_DIM
GQ_HEADS = 8
GQ_KV_HEADS = 2
GQ_GROUP = GQ_HEADS // GQ_KV_HEADS
GQ_HEAD_DIM = 128
GQ_WIDTH = GQ_HEADS * GQ_HEAD_DIM
GQ_KV_WIDTH = GQ_KV_HEADS * GQ_HEAD_DIM
S5_WIDTH = 1024
S5_GROUP = 16
S5_GROUPS = S5_WIDTH // S5_GROUP
S5_STATE = 64
S5_MAX_REAL = -1e-4
N_BRANCH = 3
ROPE_THETA = 10000.0
LN_EPS = 1e-6
RMS_EPS = 1e-6
DA_SUBLN_EPS = 1e-5

OFF_QA = 0
OFF_KA = OFF_QA + DA_WIDTH
OFF_VA = OFF_KA + DA_WIDTH
OFF_QB = OFF_VA + DA_WIDTH
OFF_KB = OFF_QB + GQ_WIDTH
OFF_VB = OFF_KB + GQ_KV_WIDTH
OFF_U = OFF_VB + GQ_KV_WIDTH
OFF_GATE = OFF_U + S5_WIDTH

LANES = 128
MXU_DIM = 256
V7X_VMEM_BYTES = 64 * 1024 * 1024
VMEM_LIMIT = V7X_VMEM_BYTES - 8 * 1024 * 1024

S5_CHUNK = 16
S5_CW = S5_CHUNK * S5_GROUP
S5_BLOCK_GROUPS = LANES // S5_GROUP
S5_SCAN_UNROLL = 8

ATTN_ROWS_PER_STEP = 512
LOG2E = math.log2(math.e)
NEG_BIG = -1e30


def _cparams(*sem):
    return pltpu.CompilerParams(dimension_semantics=sem, vmem_limit_bytes=VMEM_LIMIT)


def _tile(n, target, mult):
    best = None
    for t in range(mult, min(n, target) + 1, mult):
        if n % t == 0:
            best = t
    assert best is not None, (n, target, mult)
    return best


def _row_halves(tm):
    if tm % 32 == 0:
        return [slice(0, tm // 2), slice(tm // 2, tm)]
    return [slice(0, tm)]


def _layer_spec(block_shape, layer, col=0):
    return pl.BlockSpec((None,) + tuple(block_shape), lambda *_: (layer, 0, col))


def _sigmoid(x):
    return 1.0 / (1.0 + jnp.exp(-x))


def _gelu_tanh(x):
    return 0.5 * x * (1.0 + jnp.tanh(math.sqrt(2.0 / math.pi) * (x + 0.044715 * (x * x * x))))


def _row_select(vec_ref, row0, tm, n_lat):
    rows = row0 + lax.broadcasted_iota(jnp.int32, (tm, 1), 0)
    return jnp.where(rows < n_lat, vec_ref[0:1, :], vec_ref[1:2, :])


def _layer_norm(z, g, b):
    mu = jnp.mean(z, axis=-1, keepdims=True)
    zc = z - mu
    var = jnp.mean(zc * zc, axis=-1, keepdims=True)
    return zc * lax.rsqrt(var + LN_EPS) * g + b


def _mod_kernel(c_ref, w_ref, b_ref, o_ref):
    a = c_ref[...]
    a = a * _sigmoid(a)
    o_ref[0] = jnp.dot(a.astype(BF16), w_ref[0].astype(BF16), preferred_element_type=F32) + b_ref[0]


def _ada_mod(c_rows, w_ada, b_ada):
    depth, d, n6 = w_ada.shape
    tn = _tile(n6, 1024, LANES)
    return pl.pallas_call(
        _mod_kernel,
        out_shape=jax.ShapeDtypeStruct((depth, 8, n6), F32),
        grid=(depth, n6 // tn),
        in_specs=[
            pl.BlockSpec((8, d), lambda l, j: (0, 0)),
            pl.BlockSpec((1, d, tn), lambda l, j: (l, 0, j)),
            pl.BlockSpec((1, 1, tn), lambda l, j: (l, 0, j)),
        ],
        out_specs=pl.BlockSpec((1, 8, tn), lambda l, j: (l, 0, j)),
        compiler_params=_cparams("parallel", "parallel"),
        name="ada_mod",
    )(c_rows, w_ada, b_ada.reshape(depth, 1, n6))


def _in_proj_kernel(x_ref, sh_ref, sc_ref, w_ref, o_ref, h_scr, *, tm, n_lat):
    i = pl.program_id(0)

    @pl.when(pl.program_id(1) == 0)
    def _():
        sh = _row_select(sh_ref, i * tm, tm, n_lat)
        sc = _row_select(sc_ref, i * tm, tm, n_lat)
        h_scr[...] = (x_ref[...] * (1.0 + sc) + sh).astype(BF16)

    o_ref[...] = jnp.dot(h_scr[...], w_ref[...].astype(BF16), preferred_element_type=F32).astype(o_ref.dtype)


def _in_proj(xs, mods, w_in, layer, n_lat):
    r, d = xs.shape
    n = w_in.shape[2]
    tm = _tile(r, 2112, 16)
    tn = _tile(n, 512, LANES)
    return pl.pallas_call(
        functools.partial(_in_proj_kernel, tm=tm, n_lat=n_lat),
        out_shape=jax.ShapeDtypeStruct((r, n), BF16),
        grid=(r // tm, n // tn),
        in_specs=[
            pl.BlockSpec((tm, d), lambda i, j: (i, 0), pipeline_mode=pl.Buffered(1)),
            _layer_spec((8, d), layer, 0),
            _layer_spec((8, d), layer, 1),
            pl.BlockSpec((None, d, tn), lambda i, j: (layer, 0, j)),
        ],
        out_specs=pl.BlockSpec((tm, tn), lambda i, j: (i, j)),
        scratch_shapes=[pltpu.VMEM((tm, d), BF16)],
        compiler_params=_cparams("parallel", "arbitrary"),
        name="in_proj",
    )(xs, mods, mods, w_in)


def _rope(z, cos, sin_a, sin_b, quarter):
    return z * cos + pltpu.roll(z, LANES - quarter, 1) * sin_a + pltpu.roll(z, quarter, 1) * sin_b


def _rms(z, w, eps):
    return z * lax.rsqrt(jnp.mean(z * z, axis=-1, keepdims=True) + eps) * w


def _prep_kernel(qa_ref, ka_ref, qb_ref, kb_ref, tda_ref, tgq_ref, qn_ref, kn_ref,
                 qa_o, kat_o, qb_o, kbt_o):
    cda, sda_a, sda_b = tda_ref[0], tda_ref[1], tda_ref[2]
    cgq, sgq_a, sgq_b = tgq_ref[0], tgq_ref[1], tgq_ref[2]
    da_scale = DA_HEAD_DIM ** -0.5 * LOG2E
    gq_scale = GQ_HEAD_DIM ** -0.5 * LOG2E
    for h in range(DA_HEADS):
        sl = slice(h * LANES, (h + 1) * LANES)
        z = qa_ref[:, sl].astype(F32)
        qa_o[:, sl] = (_rope(z, cda, sda_a, sda_b, DA_HEAD_DIM // 4) * da_scale).astype(BF16)
        z = ka_ref[:, sl].astype(F32)
        kat_o[h, 0] = _rope(z, cda, sda_a, sda_b, DA_HEAD_DIM // 4).T.astype(BF16)
    for h in range(GQ_HEADS):
        sl = slice(h * LANES, (h + 1) * LANES)
        z = _rms(qb_ref[:, sl].astype(F32), qn_ref[...], RMS_EPS)
        qb_o[:, sl] = (_rope(z, cgq, sgq_a, sgq_b, GQ_HEAD_DIM // 4) * gq_scale).astype(BF16)
    for h in range(GQ_KV_HEADS):
        sl = slice(h * LANES, (h + 1) * LANES)
        z = _rms(kb_ref[:, sl].astype(F32), kn_ref[...], RMS_EPS)
        kbt_o[h, 0] = _rope(z, cgq, sgq_a, sgq_b, GQ_HEAD_DIM // 4).T.astype(BF16)


def _prep_qk(proj, tab_da, tab_gq, qn_w, kn_w, layer, tk):
    r = proj.shape[0]
    nk = r // tk
    blk = lambda w, off: pl.BlockSpec((tk, w), lambda i: (i, off // w))
    out = lambda w: pl.BlockSpec((tk, w), lambda i: (i, 0))
    outt = lambda heads: pl.BlockSpec((heads, 1, LANES, tk), lambda i: (0, i, 0, 0))
    return pl.pallas_call(
        _prep_kernel,
        out_shape=(
            jax.ShapeDtypeStruct((r, DA_WIDTH), BF16),
            jax.ShapeDtypeStruct((DA_HEADS, nk, LANES, tk), BF16),
            jax.ShapeDtypeStruct((r, GQ_WIDTH), BF16),
            jax.ShapeDtypeStruct((GQ_KV_HEADS, nk, LANES, tk), BF16),
        ),
        grid=(nk,),
        in_specs=[
            blk(DA_WIDTH, OFF_QA), blk(DA_WIDTH, OFF_KA), blk(GQ_WIDTH, OFF_QB), blk(GQ_KV_WIDTH, OFF_KB),
            pl.BlockSpec((3, tk, LANES), lambda i: (0, i, 0)),
            pl.BlockSpec((3, tk, LANES), lambda i: (0, i, 0)),
            _layer_spec((1, LANES), layer),
            _layer_spec((1, LANES), layer),
        ],
        out_specs=(out(DA_WIDTH), outt(DA_HEADS), out(GQ_WIDTH), outt(GQ_KV_HEADS)),
        compiler_params=_cparams("parallel"),
        name="prep_qk",
    )(proj, proj, proj, proj, tab_da, tab_gq, qn_w, kn_w)


def _rope_tables(n_lat, n_ctx, dim):
    rows = n_lat // GRID_W
    row = np.repeat(np.arange(rows, dtype=np.float64), GRID_W)
    col = np.tile(np.arange(GRID_W, dtype=np.float64), rows)
    axis_dim = dim // 2
    inv_freq = ROPE_THETA ** (-np.arange(0, axis_dim, 2, dtype=np.float64) / axis_dim)
    ang_r = row[:, None] * inv_freq[None, :]
    ang_c = col[:, None] * inv_freq[None, :]
    ang = np.concatenate([ang_r, ang_r, ang_c, ang_c], axis=-1)
    cos, sin = np.cos(ang), np.sin(ang)
    first_half = (np.arange(dim) % (dim // 2)) < (dim // 4)
    sin_a = np.where(first_half[None, :], -sin, 0.0)
    sin_b = np.where(first_half[None, :], 0.0, sin)
    tab = np.stack([cos, sin_a, sin_b])
    ident = np.stack([np.ones((n_ctx, dim)), np.zeros((n_ctx, dim)), np.zeros((n_ctx, dim))])
    tab = np.concatenate([tab, ident], axis=1)
    return jnp.asarray(np.tile(tab, (1, 1, LANES // dim)), F32)


def _flash_loop(qs, kt_ref, v_ref, vx_scr, tk, n_chunks):
    @pl.when(pl.program_id(1) == 0)
    def _():
        vx_scr[:, :LANES] = v_ref[...]
        vx_scr[:, LANES:] = jnp.ones((vx_scr.shape[0], LANES), vx_scr.dtype)

    tq = qs[0].shape[0]
    scores = lambda c: [jnp.dot(q, kt_ref[c], preferred_element_type=F32) for q in qs]
    m = [jnp.full((tq, 1), NEG_BIG, F32) for _ in qs]
    acc = [jnp.zeros((tq, 2 * LANES), F32) for _ in qs]
    s_next = scores(0)
    for c in range(n_chunks):
        s_cur = s_next
        if c + 1 < n_chunks:
            s_next = scores(c + 1)
        vx = vx_scr[c * tk:(c + 1) * tk, :]
        for i, s in enumerate(s_cur):
            m_new = jnp.maximum(m[i], jnp.max(s, axis=-1, keepdims=True))
            p = jnp.exp2((s - m_new).astype(BF16))
            acc[i] = jnp.exp2(m[i] - m_new) * acc[i] + jnp.dot(p, vx, preferred_element_type=F32)
            m[i] = m_new
    return [a[:, :LANES] / a[:, LANES:] for a in acc]


def _softmax_once(s, v):
    p = jnp.exp2(s - jnp.max(s, axis=-1, keepdims=True))
    return jnp.dot(p.astype(BF16), v, preferred_element_type=F32) / jnp.sum(p, axis=-1, keepdims=True)


def _da_split(q):
    lane = lax.broadcasted_iota(jnp.int32, (1, LANES), 1)
    zero = jnp.zeros_like(q)
    return jnp.where(lane < DA_HEAD_DIM, q, zero), jnp.where(lane < DA_HEAD_DIM, zero, q)


def _da_finish(o1, o2, lam_ref, w_ref, o_ref, lam_init):
    lp = lam_ref[...]
    lam = (jnp.exp(jnp.sum(lp[0:1] * lp[1:2], axis=-1, keepdims=True))
           - jnp.exp(jnp.sum(lp[2:3] * lp[3:4], axis=-1, keepdims=True)) + lam_init)
    o = _rms(o1 - lam * o2, w_ref[...], DA_SUBLN_EPS) * (1.0 - lam_init)
    o_ref[...] = o.astype(o_ref.dtype)


def _da_kernel(lam_ref, w_ref, q_ref, kt_ref, v_ref, o_ref, vx, *, tk, n_chunks, lam_init):
    o1, o2 = _flash_loop(_da_split(q_ref[...]), kt_ref, v_ref, vx, tk, n_chunks)
    _da_finish(o1, o2, lam_ref, w_ref, o_ref, lam_init)


def _da_ctx_kernel(lam_ref, w_ref, q_ref, kt_ref, v_ref, o_ref, *, off, n_ctx, lam_init):
    q1, q2 = _da_split(q_ref[...])
    kt = kt_ref[:, off:off + n_ctx]
    v = v_ref[...]
    o1 = _softmax_once(jnp.dot(q1, kt, preferred_element_type=F32), v)
    o2 = _softmax_once(jnp.dot(q2, kt, preferred_element_type=F32), v)
    _da_finish(o1, o2, lam_ref, w_ref, o_ref, lam_init)


def _gq_kernel(q_ref, kt_ref, v_ref, o_ref, vx, *, tk, n_chunks):
    (o,) = _flash_loop((q_ref[...],), kt_ref, v_ref, vx, tk, n_chunks)
    o_ref[...] = o.astype(o_ref.dtype)


def _gq_ctx_kernel(q_ref, kt_ref, v_ref, o_ref, *, off, n_ctx):
    kt = kt_ref[:, off:off + n_ctx]
    o_ref[...] = _softmax_once(jnp.dot(q_ref[...], kt, preferred_element_type=F32), v_ref[...]).astype(o_ref.dtype)


def _attention(params, layer, q, kt, proj, v_off, kv_group, n_lat, main_kernel, ctx_kernel, name, tq_rows):
    r, width = q.shape
    n_heads = width // LANES
    n_ctx = r - n_lat
    _, n_chunks, _, tk = kt.shape
    tq = _tile(n_lat, tq_rows, 16)
    ctx_blk = n_lat // n_ctx
    off = n_lat - (n_chunks - 1) * tk
    param_specs = [_layer_spec(p.shape[1:], layer) for p in params]
    out = pl.pallas_call(
        functools.partial(main_kernel, tk=tk, n_chunks=n_chunks),
        out_shape=jax.ShapeDtypeStruct((r, width), BF16),
        grid=(n_heads, n_lat // tq),
        in_specs=param_specs + [
            pl.BlockSpec((tq, LANES), lambda h, i: (i, h)),
            pl.BlockSpec((None, n_chunks, LANES, tk), lambda h, i: (h // kv_group, 0, 0, 0)),
            pl.BlockSpec((r, LANES), lambda h, i: (0, v_off // LANES + h // kv_group)),
        ],
        out_specs=pl.BlockSpec((tq, LANES), lambda h, i: (i, h)),
        scratch_shapes=[pltpu.VMEM((r, 2 * LANES), BF16)],
        input_output_aliases={len(params): 0},
        compiler_params=_cparams("parallel", "arbitrary"),
        name=name,
    )(*params, q, kt, proj)
    return pl.pallas_call(
        functools.partial(ctx_kernel, off=off, n_ctx=n_ctx),
        out_shape=jax.ShapeDtypeStruct((r, width), BF16),
        grid=(n_heads,),
        in_specs=param_specs + [
            pl.BlockSpec((n_ctx, LANES), lambda h: (ctx_blk, h)),
            pl.BlockSpec((None, None, LANES, tk), lambda h: (h // kv_group, n_chunks - 1, 0, 0)),
            pl.BlockSpec((n_ctx, LANES), lambda h: (ctx_blk, v_off // LANES + h // kv_group)),
        ],
        out_specs=pl.BlockSpec((n_ctx, LANES), lambda h: (ctx_blk, h)),
        input_output_aliases={len(params): 0},
        compiler_params=_cparams("parallel"),
        name=name + "_ctx",
    )(*params, out, kt, proj)


def _diff_attention(lam_params, subln_w, layer, qa, kat, proj, n_lat, lam_init):
    return _attention((lam_params, subln_w), layer, qa, kat, proj, OFF_VA, 1, n_lat,
                      functools.partial(_da_kernel, lam_init=lam_init),
                      functools.partial(_da_ctx_kernel, lam_init=lam_init), "diff_attention", ATTN_ROWS_PER_STEP // 2)


def _gq_attention(qb, kbt, proj, n_lat):
    return _attention((), 0, qb, kbt, proj, OFF_VB, GQ_GROUP, n_lat, _gq_kernel, _gq_ctx_kernel, "gq_attention",
                      ATTN_ROWS_PER_STEP)


def _s5_matrices(a_re, a_im, log_dt, b_re, b_im, c_re, c_im, d_skip):
    t = S5_CHUNK
    g, p, h = S5_GROUPS, S5_STATE, S5_GROUP
    a_re = jnp.minimum(a_re.astype(F32), S5_MAX_REAL)
    a_im = a_im.astype(F32)
    dt = jnp.exp(log_dt.astype(F32))[..., None]
    mag = jnp.exp(a_re * dt)
    lr = mag * jnp.cos(a_im * dt)
    li = mag * jnp.sin(a_im * dt)
    den = jnp.square(a_re) + jnp.square(a_im)
    nr = lr - 1.0
    cr = (nr * a_re + li * a_im) / den
    ci = (li * a_re - nr * a_im) / den
    bbr = cr[..., None] * b_re - ci[..., None] * b_im
    bbi = cr[..., None] * b_im + ci[..., None] * b_re
    j = jnp.arange(t + 1, dtype=F32)
    pmag = jnp.exp((a_re * dt)[..., None] * j)
    pr = pmag * jnp.cos((a_im * dt)[..., None] * j)
    pi = pmag * jnp.sin((a_im * dt)[..., None] * j)
    ct_re = c_re.astype(F32).transpose(0, 1, 3, 2)[:, :, :, None, :]
    ct_im = c_im.astype(F32).transpose(0, 1, 3, 2)[:, :, :, None, :]
    clr = ct_re * pr[..., None] - ct_im * pi[..., None]
    cli = ct_re * pi[..., None] + ct_im * pr[..., None]
    eye = jnp.eye(h, dtype=F32)[None] * d_skip.astype(F32).reshape(g, h)[:, None, :]
    lhs = jnp.concatenate([bbr[0], -bbi[0], bbr[1], -bbi[1]], axis=1).transpose(0, 2, 1)
    slots = lambda c, lo, hi: jnp.pad(c.reshape(g, p, t * h), ((0, 0), (0, 0), (lo * h, hi * h)))
    rhs = jnp.concatenate([slots(clr[0][:, :, :t], t - 1, 1), slots(cli[0][:, :, :t], t - 1, 1),
                           slots(clr[1][:, :, t - 1::-1], 0, t), slots(cli[1][:, :, t - 1::-1], 0, t)],
                          axis=1)
    skip = jnp.pad(eye, ((0, 0), (0, 0), ((t - 1) * h, t * h)))

    def to_state(d, pw_r, pw_i):
        pw_r = pw_r.transpose(0, 2, 1)[:, :, None, :]
        pw_i = pw_i.transpose(0, 2, 1)[:, :, None, :]
        br = bbr[d].transpose(0, 2, 1)[:, None, :, :]
        bi = bbi[d].transpose(0, 2, 1)[:, None, :, :]
        return (pw_r * br - pw_i * bi).reshape(g, t * h, p), (pw_r * bi + pw_i * br).reshape(g, t * h, p)

    wf = to_state(0, pr[0, :, :, :t][..., ::-1], pi[0, :, :, :t][..., ::-1])
    wb = to_state(1, pr[1, :, :, :t], pi[1, :, :, :t])
    w = jnp.concatenate([wf[0], wf[1], wb[0], wb[1]], axis=-1)

    flat = lambda c: c.reshape(g, p, t * h)
    n = jnp.concatenate([flat(clr[0][:, :, 1:]), -flat(cli[0][:, :, 1:]),
                         flat(clr[1][:, :, :0:-1]), -flat(cli[1][:, :, :0:-1])],
                        axis=1)

    def coef(d):
        r_, i_ = pr[d, :, :, t], pi[d, :, :, t]
        return [jnp.concatenate([r_, r_], -1), jnp.concatenate([-i_, i_], -1), jnp.concatenate([i_, -i_], -1)]

    coefs = jnp.stack(coef(0) + coef(1))
    return (lhs, rhs, skip), w.astype(BF16), n.astype(BF16), coefs


def _s5_lag_kernel(lhs_ref, rhs_ref, skip_ref, sel_ref, m_ref, kk_scr):
    for gl in range(S5_BLOCK_GROUPS):
        kk = jnp.dot(lhs_ref[gl], rhs_ref[gl], precision=lax.Precision.HIGHEST, preferred_element_type=F32)
        kk_scr[gl * S5_GROUP:(gl + 1) * S5_GROUP, :] = kk + skip_ref[gl]
    kk_all = kk_scr[...].astype(BF16)
    for s in range(S5_CHUNK):
        win = jnp.dot(kk_all, sel_ref[s], preferred_element_type=F32).astype(m_ref.dtype)
        for gl in range(S5_BLOCK_GROUPS):
            m_ref[gl, s * S5_GROUP:(s + 1) * S5_GROUP, :] = win[gl * S5_GROUP:(gl + 1) * S5_GROUP, :]


def _s5_lag_matrices(lhs, rhs, skip):
    depth, g = lhs.shape[:2]
    t, h, gb = S5_CHUNK, S5_GROUP, S5_BLOCK_GROUPS
    width = 2 * t * h
    sel = np.zeros((t, width, t * h), np.float32)
    for s in range(t):
        sel[s, (t - 1 - s) * h + np.arange(t * h), np.arange(t * h)] = 1.0
    blk = lambda a, b: pl.BlockSpec((None, gb, a, b), lambda l, i: (l, i, 0, 0))
    return pl.pallas_call(
        _s5_lag_kernel,
        out_shape=jax.ShapeDtypeStruct((depth, g, S5_CW, S5_CW), BF16),
        grid=(depth, g // gb),
        in_specs=[blk(h, lhs.shape[3]), blk(rhs.shape[2], width), blk(h, width),
                  pl.BlockSpec(sel.shape, lambda l, i: (0, 0, 0), pipeline_mode=pl.Buffered(1))],
        out_specs=blk(S5_CW, S5_CW),
        scratch_shapes=[pltpu.VMEM((gb * h, width), F32)],
        compiler_params=_cparams("parallel", "parallel"),
        name="s5_lag_matrices",
    )(lhs, rhs, skip, jnp.asarray(sel, BF16))


def _s5_lane_perm():
    i = np.arange(S5_BLOCK_GROUPS * S5_CW)
    tok, grp, ch = i // LANES, (i // S5_GROUP) % S5_BLOCK_GROUPS, i % S5_GROUP
    perm = np.zeros((i.size, i.size), np.float32)
    perm[i, grp * S5_CW + tok * S5_GROUP + ch] = 1.0
    return perm


def _s5_a_kernel(*refs):
    u_refs = refs[:S5_CHUNK]
    p_ref, m_ref, w_ref, y_ref, s_ref = refs[S5_CHUNK:]
    u_nat = jnp.concatenate([u[...] for u in u_refs], axis=1)
    ug = jnp.dot(u_nat, p_ref[...], preferred_element_type=F32).astype(BF16)
    for gl in range(S5_BLOCK_GROUPS):
        u_g = ug[:, gl * S5_CW:(gl + 1) * S5_CW]
        y_ref[gl] = jnp.dot(u_g, m_ref[gl], preferred_element_type=F32)
        s_ref[gl] = jnp.dot(u_g, w_ref[gl], preferred_element_type=F32)


def _s5_b_kernel(s_ref, coef_ref, x_ref, *, n_chunks, n_lat_chunks):
    half = 2 * S5_STATE
    n_ctx_chunks = n_chunks - n_lat_chunks
    fwd_sl, bwd_sl = slice(0, half), slice(half, 2 * half)

    def step(sl, a, b, bt):
        def f(c, carry):
            x, xt = carry
            s = s_ref[c, :, sl]
            x_ref[c, :, sl] = x.astype(x_ref.dtype)
            return a * x + b * xt + s, a * xt + bt * x + pltpu.roll(s, S5_STATE, 1)
        return f

    fwd = step(fwd_sl, coef_ref[0], coef_ref[1], coef_ref[2])
    bwd = step(bwd_sl, coef_ref[3], coef_ref[4], coef_ref[5])
    zero = jnp.zeros_like(coef_ref[0])
    loop = functools.partial(lax.fori_loop, unroll=S5_SCAN_UNROLL)

    def both(f_first, b_first):
        return lambda i, cr: (fwd(f_first + i, cr[0]), bwd(b_first - i, cr[1]))

    carry = loop(0, n_ctx_chunks, both(n_lat_chunks, n_chunks - 1), ((zero, zero), (zero, zero)))
    loop(0, n_lat_chunks, both(0, n_lat_chunks - 1), carry)


def _s5_c_kernel(x_ref, n_ref, yi_ref, q_ref, y_ref):
    ys = [(yi_ref[gl] + jnp.dot(x_ref[gl], n_ref[gl], preferred_element_type=F32)).astype(BF16)
          for gl in range(S5_BLOCK_GROUPS)]
    y_ref[...] = jnp.dot(jnp.concatenate(ys, axis=1), q_ref[...], preferred_element_type=F32).astype(y_ref.dtype)


def _s5_mixer(u, m, w, ncat, coefs, layer, n_lat):
    r = u.shape[0]
    g, t, gb = S5_GROUPS, S5_CHUNK, S5_BLOCK_GROUPS
    nc = r // t
    sw = 4 * S5_STATE
    nblk = g // gb
    perm = _s5_lane_perm()
    p_in = jnp.asarray(perm, BF16)
    p_out = jnp.asarray(perm.T, BF16)
    u_rows = u.reshape(nc, t * S5_WIDTH)
    grp = lambda a, b: pl.BlockSpec((gb, a, b), lambda i: (i, 0, 0))
    wgrp = lambda a, b: pl.BlockSpec((None, gb, a, b), lambda i: (layer, i, 0, 0))
    perm_spec = pl.BlockSpec(perm.shape, lambda i: (0, 0), pipeline_mode=pl.Buffered(1))
    tok_specs = [pl.BlockSpec((nc, LANES), lambda i, s=s: (0, s * nblk + i)) for s in range(t)]
    y_intra, s = pl.pallas_call(
        _s5_a_kernel,
        out_shape=(jax.ShapeDtypeStruct((g, nc, S5_CW), F32), jax.ShapeDtypeStruct((g, nc, sw), F32)),
        grid=(nblk,),
        in_specs=tok_specs + [perm_spec, wgrp(S5_CW, S5_CW), wgrp(S5_CW, sw)],
        out_specs=(grp(nc, S5_CW), grp(nc, sw)),
        compiler_params=_cparams("parallel"),
        name="s5_chunk_local",
    )(*([u_rows] * t), p_in, m, w)

    gs = 16
    st = s.transpose(1, 0, 2)
    xt = pl.pallas_call(
        functools.partial(_s5_b_kernel, n_chunks=nc, n_lat_chunks=n_lat // t),
        out_shape=jax.ShapeDtypeStruct((nc, g, sw), BF16),
        grid=(g // gs,),
        in_specs=[pl.BlockSpec((nc, gs, sw), lambda i: (0, i, 0)),
                  pl.BlockSpec((None, 6, gs, 2 * S5_STATE), lambda i: (layer, 0, i, 0))],
        out_specs=pl.BlockSpec((nc, gs, sw), lambda i: (0, i, 0)),
        compiler_params=_cparams("parallel"),
        name="s5_chunk_scan",
    )(st, coefs)

    x_in = xt.transpose(1, 0, 2)
    y_blocks = pl.pallas_call(
        _s5_c_kernel,
        out_shape=jax.ShapeDtypeStruct((nblk, nc, gb * S5_CW), BF16),
        grid=(nblk,),
        in_specs=[grp(nc, sw), wgrp(sw, S5_CW), grp(nc, S5_CW), perm_spec],
        out_specs=pl.BlockSpec((None, nc, gb * S5_CW), lambda i: (i, 0, 0)),
        compiler_params=_cparams("parallel"),
        name="s5_state_out",
    )(x_in, ncat, y_intra, p_out)
    return y_blocks.reshape(nblk, nc, t, LANES).transpose(1, 2, 0, 3).reshape(r, S5_WIDTH)


def _merge_kernel(oa_ref, ob_ref, y_ref, wpa_ref, wpb_ref, wga_ref, wgg_ref, g0_ref, g1_ref, g2_ref,
                  o_ref, gy_scr):
    @pl.when(pl.program_id(1) == 0)
    def _():
        gy_scr[...] = _gelu_tanh(y_ref[...].astype(F32)).astype(BF16)

    dot = lambda a, b: jnp.dot(a, b, preferred_element_type=F32)
    for rows in _row_halves(o_ref.shape[0]):
        pa = dot(oa_ref[rows, :], wpa_ref[...])
        pb = dot(ob_ref[rows, :], wpb_ref[...])
        gy = gy_scr[rows, :]
        pc = dot(gy, wga_ref[...]) * _sigmoid(dot(gy, wgg_ref[...]))
        sg = lambda ref: _sigmoid(ref[rows, :].astype(F32))
        o_ref[rows, :] = (sg(g0_ref) * pa + sg(g1_ref) * pb + sg(g2_ref) * pc).astype(o_ref.dtype)


def _merge(o_a, o_b, y_s5, proj, w_pa, w_pb, w_glu, layer, d):
    r = o_a.shape[0]
    tm = _tile(r, 1056, 16)
    tn = _tile(math.gcd(d, OFF_GATE), 512, LANES)
    row = lambda w: pl.BlockSpec((tm, w), lambda i, j: (i, 0))
    col = lambda k, off: pl.BlockSpec((None, k, tn), lambda i, j: (layer, 0, off // tn + j))
    gate = lambda b: pl.BlockSpec((tm, tn), lambda i, j: (i, (OFF_GATE + b * d) // tn + j))
    return pl.pallas_call(
        _merge_kernel,
        out_shape=jax.ShapeDtypeStruct((r, d), BF16),
        grid=(r // tm, d // tn),
        in_specs=[row(DA_WIDTH), row(GQ_WIDTH), row(S5_WIDTH),
                  col(DA_WIDTH, 0), col(GQ_WIDTH, 0), col(S5_WIDTH, 0), col(S5_WIDTH, d),
                  gate(0), gate(1), gate(2)],
        out_specs=pl.BlockSpec((tm, tn), lambda i, j: (i, j)),
        scratch_shapes=[pltpu.VMEM((tm, S5_WIDTH), BF16)],
        compiler_params=_cparams("parallel", "arbitrary"),
        name="branch_merge",
    )(o_a, o_b, y_s5, w_pa, w_pb, w_glu, w_glu, proj, proj, proj)


def _out_ln_kernel(m_ref, w_ref, x_ref, gm_ref, lg_ref, lb_ref, o_ref, *, tm, n_lat, alpha):
    for rows in _row_halves(tm):
        n_rows = rows.stop - rows.start
        y = jnp.dot(m_ref[rows, :], w_ref[...], preferred_element_type=F32)
        gate = _row_select(gm_ref, pl.program_id(0) * tm + rows.start, n_rows, n_lat)
        o_ref[rows, :] = _layer_norm(alpha * x_ref[rows, :] + gate * y, lg_ref[...], lb_ref[...])


def _out_ln(merged, w_o, layer, xs, mods, ln_g, ln_b, n_lat, alpha):
    r, d = xs.shape
    tm = _tile(r, 768, 32)
    return pl.pallas_call(
        functools.partial(_out_ln_kernel, tm=tm, n_lat=n_lat, alpha=alpha),
        out_shape=jax.ShapeDtypeStruct((r, d), F32),
        grid=(r // tm,),
        in_specs=[
            pl.BlockSpec((tm, d), lambda i: (i, 0)),
            pl.BlockSpec((None, d, d), lambda i: (layer, 0, 0), pipeline_mode=pl.Buffered(1)),
            pl.BlockSpec((tm, d), lambda i: (i, 0)),
            _layer_spec((8, d), layer, 2),
            _layer_spec((1, d), layer), _layer_spec((1, d), layer),
        ],
        out_specs=pl.BlockSpec((tm, d), lambda i: (i, 0)),
        compiler_params=_cparams("parallel"),
        name="out_proj_ln",
    )(merged, w_o, xs, mods, ln_g, ln_b)


def _ffn_kernel(x_ref, sh_ref, sc_ref, gf_ref, wg_ref, wu_ref, wd_ref, lg_ref, lb_ref, o_ref,
                h_scr, *, tm, n_lat, alpha):
    i = pl.program_id(0)
    f = pl.program_id(1)

    @pl.when(f == 0)
    def _():
        sh = _row_select(sh_ref, i * tm, tm, n_lat)
        sc = _row_select(sc_ref, i * tm, tm, n_lat)
        h_scr[...] = (x_ref[...] * (1.0 + sc) + sh).astype(BF16)
        o_ref[...] = jnp.zeros_like(o_ref)

    h = h_scr[...]
    gate = jnp.dot(h, wg_ref[...].astype(BF16), preferred_element_type=F32)
    up = jnp.dot(h, wu_ref[...].astype(BF16), preferred_element_type=F32)
    act = (gate * _sigmoid(gate) * up).astype(BF16)
    o_ref[...] += jnp.dot(act, wd_ref[...].astype(BF16), preferred_element_type=F32)

    @pl.when(f == pl.num_programs(1) - 1)
    def _():
        g = _row_select(gf_ref, i * tm, tm, n_lat)
        o_ref[...] = _layer_norm(alpha * x_ref[...] + g * o_ref[...], lg_ref[...], lb_ref[...])


def _ffn(xs, r, mods, w_gate, w_up, w_down, layer, ln_g, ln_b, n_lat, alpha):
    d = xs.shape[1]
    f = w_gate.shape[2]
    tm = _tile(r, 1056, 16)
    tf = _tile(f, MXU_DIM, LANES)
    modv = lambda c: _layer_spec((8, d), layer, c)
    return pl.pallas_call(
        functools.partial(_ffn_kernel, tm=tm, n_lat=n_lat, alpha=alpha),
        out_shape=jax.ShapeDtypeStruct((r, d), F32),
        grid=(r // tm, f // tf),
        in_specs=[
            pl.BlockSpec((tm, d), lambda i, j: (i, 0), pipeline_mode=pl.Buffered(1)),
            modv(3), modv(4), modv(5),
            pl.BlockSpec((None, d, tf), lambda i, j: (layer, 0, j)),
            pl.BlockSpec((None, d, tf), lambda i, j: (layer, 0, j)),
            pl.BlockSpec((None, tf, d), lambda i, j: (layer, j, 0)),
            _layer_spec((1, d), layer), _layer_spec((1, d), layer),
        ],
        out_specs=pl.BlockSpec((tm, d), lambda i, j: (i, 0)),
        scratch_shapes=[pltpu.VMEM((tm, d), BF16)],
        compiler_params=_cparams("parallel", "arbitrary"),
        name="swiglu_ln",
    )(xs, mods, mods, mods, w_gate, w_up, w_down, ln_g, ln_b)


def kernel(x, c, ctx, c_ctx, w_ada, b_ada, w_in, da_lam_q1, da_lam_k1, da_lam_q2, da_lam_k2, da_subln_w, w_pa, gq_qnorm_w, gq_knorm_w, w_pb, s5_a_re, s5_a_im, s5_log_dt, s5_b_re, s5_b_im, s5_c_re, s5_c_im, s5_d, w_glu, w_o, ln_mix_g, ln_mix_b, w_ffn_gate, w_ffn_up, w_ffn_down, ln_ffn_g, ln_ffn_b):
    batch, n_lat, d = x.shape
    n_ctx = ctx.shape[1]
    depth = w_in.shape[0]
    r = n_lat + n_ctx
    assert batch == 1 and n_lat % GRID_W == 0 and n_ctx % S5_CHUNK == 0 and n_lat % S5_CHUNK == 0
    assert n_lat % n_ctx == 0
    alpha = (2 * depth) ** 0.25
    tk = _tile(r, 768, MXU_DIM)
    assert n_ctx <= tk and (n_lat - (r // tk - 1) * tk) % LANES == 0

    xs = jnp.concatenate([x[0], ctx[0]], axis=0).astype(F32)
    c_rows = jnp.zeros((8, d), F32).at[0].set(c[0]).at[1].set(c_ctx)
    mods = _ada_mod(c_rows, w_ada, b_ada)
    tab_da = _rope_tables(n_lat, n_ctx, DA_HEAD_DIM)
    tab_gq = _rope_tables(n_lat, n_ctx, GQ_HEAD_DIM)

    w_pa, w_pb, w_glu, w_o = (w.astype(BF16) for w in (w_pa, w_pb, w_glu, w_o))
    s5_lag, s5_w, s5_n, s5_coefs = jax.vmap(_s5_matrices)(
        s5_a_re, s5_a_im, s5_log_dt, s5_b_re, s5_b_im, s5_c_re, s5_c_im, s5_d)
    s5_m = _s5_lag_matrices(*s5_lag)

    row = lambda v: v.astype(F32)[:, None, :]
    lam_params = jnp.stack([da_lam_q1, da_lam_k1, da_lam_q2, da_lam_k2], axis=1).astype(F32)
    subln_w, qn_w, kn_w = row(da_subln_w), row(gq_qnorm_w), row(gq_knorm_w)
    ln_mix_g, ln_mix_b, ln_ffn_g, ln_ffn_b = row(ln_mix_g), row(ln_mix_b), row(ln_ffn_g), row(ln_ffn_b)

    for l in range(depth):
        lam_init = 0.8 - 0.6 * math.exp(-0.3 * l)
        proj = _in_proj(xs, mods, w_in, l, n_lat)
        qa, kat, qb, kbt = _prep_qk(proj, tab_da, tab_gq, qn_w, kn_w, l, tk)
        o_a = _diff_attention(lam_params, subln_w, l, qa, kat, proj, n_lat, lam_init)
        o_b = _gq_attention(qb, kbt, proj, n_lat)
        y_s5 = _s5_mixer(proj[:, OFF_U:OFF_U + S5_WIDTH], s5_m, s5_w, s5_n, s5_coefs, l, n_lat)
        merged = _merge(o_a, o_b, y_s5, proj, w_pa, w_pb, w_glu, l, d)
        xs = _out_ln(merged, w_o, l, xs, mods, ln_mix_g, ln_mix_b, n_lat, alpha)
        rows_out = n_lat if l == depth - 1 else r
        xs = _ffn(xs, rows_out, mods, w_ffn_gate, w_ffn_up, w_ffn_down, l, ln_ffn_g, ln_ffn_b, n_lat, alpha)
    return xs[None].astype(x.dtype)
```
